```python
import jax, jax.numpy as jnp
from jax import lax
import numpy as np

D_MODEL = 1024
BATCH = 32
SEQ = 2048
DEPTH = 1

HEAD_DIM = 64
FOX_HEADS = 8
NSA_HEADS = 8
NSA_KV_GROUPS = 2
NSA_HEADS_PER_GROUP = NSA_HEADS // NSA_KV_GROUPS
FOX_WIDTH = FOX_HEADS * HEAD_DIM
NSA_WIDTH = NSA_HEADS * HEAD_DIM
NSA_KV_WIDTH = NSA_KV_GROUPS * HEAD_DIM
D_FF = 4 * D_MODEL
ROPE_THETA = 10000.0
Q_BLOCK = 128
CMP_BLOCK = 32
CMP_STRIDE = 16
SEL_BLOCK = 64
SEL_TOPN = 8
SEL_Q_BLOCK = 64
WINDOW = 512
RMS_EPS = 1e-6
NEG_INF = -1e30
FORCED_BONUS = 1e4
FORGET_BIAS_INIT = 2.0
ATTN_SCALE = HEAD_DIM ** -0.5
IN_SPLITS = (FOX_WIDTH, FOX_WIDTH, FOX_WIDTH, FOX_HEADS,
             NSA_WIDTH, NSA_KV_WIDTH, NSA_KV_WIDTH, NSA_KV_WIDTH, NSA_KV_WIDTH, NSA_KV_WIDTH, NSA_KV_WIDTH,
             3 * NSA_HEADS, D_MODEL, D_MODEL)
IN_COLS = 3 * FOX_WIDTH + FOX_HEADS + NSA_WIDTH + 6 * NSA_KV_WIDTH + 3 * NSA_HEADS + 2 * D_MODEL

kernel_name = 'fox_nsa_gated_hybrid_block'


def _rms_norm(x, g):
    xf = x.astype(jnp.float32)
    y = xf * lax.rsqrt(jnp.mean(xf * xf, axis=-1, keepdims=True) + RMS_EPS)
    return (y * g.astype(jnp.float32)).astype(x.dtype)


def _heads(t, n):
    b, s, _ = t.shape
    return t.reshape(b, s, n, HEAD_DIM).transpose(0, 2, 1, 3)


def _rope(t):
    s, d = t.shape[2], t.shape[3]
    inv = jnp.power(jnp.float32(ROPE_THETA), -jnp.arange(0, d, 2, dtype=jnp.float32) / d)
    ang = jnp.arange(s, dtype=jnp.float32)[:, None] * inv[None, :]
    cos, sin = jnp.cos(ang), jnp.sin(ang)
    tf = t.astype(jnp.float32)
    t1, t2 = tf[..., : d // 2], tf[..., d // 2:]
    return jnp.concatenate([t1 * cos - t2 * sin, t2 * cos + t1 * sin], axis=-1).astype(t.dtype)


def _masked_softmax(s, mask):
    s = jnp.where(mask, s, NEG_INF)
    m = jnp.max(s, axis=-1, keepdims=True)
    p = jnp.where(mask, jnp.exp(s - m), 0.0)
    return p / jnp.maximum(jnp.sum(p, axis=-1, keepdims=True), 1e-30)


def _forgetting_attention(q, k, v, f_logit):
    b, h, s, d = q.shape
    nb = s // Q_BLOCK
    c = jnp.cumsum(jax.nn.log_sigmoid(f_logit.astype(jnp.float32)), axis=1).transpose(0, 2, 1)
    kf, vf = k.astype(jnp.float32), v.astype(jnp.float32)
    qb = q.astype(jnp.float32).reshape(b, h, nb, Q_BLOCK, d).transpose(2, 0, 1, 3, 4)
    cb = c.reshape(b, h, nb, Q_BLOCK).transpose(2, 0, 1, 3)
    k_pos = jnp.arange(s)

    def one_block(args):
        i, q_i, c_i = args
        sc = jnp.einsum('bhqd,bhkd->bhqk', q_i, kf) * ATTN_SCALE + c_i[..., None] - c[:, :, None, :]
        q_pos = i * Q_BLOCK + jnp.arange(Q_BLOCK)
        mask = k_pos[None, :] <= q_pos[:, None]
        p = _masked_softmax(sc, mask)
        return jnp.einsum('bhqk,bhkd->bhqd', p, vf)

    out = lax.map(one_block, (jnp.arange(nb), qb, cb))
    out = out.transpose(1, 0, 3, 2, 4).reshape(b, s, h * d)
    return out.astype(q.dtype)


def _compress(t, pos, w1, w2):
    s = t.shape[2]
    n_cmp = (s - CMP_BLOCK) // CMP_STRIDE + 1
    idx = jnp.arange(n_cmp)[:, None] * CMP_STRIDE + jnp.arange(CMP_BLOCK)[None, :]
    blocks = t[:, :, idx, :] + pos
    flat = blocks.reshape(blocks.shape[0], blocks.shape[1], n_cmp, CMP_BLOCK * HEAD_DIM)
    return jax.nn.silu(flat @ w1) @ w2


def _selected_attention(qg, k, v, idx):
    b, g, hg, s, d = qg.shape
    n_sel = s // SEL_BLOCK
    n_top = idx.shape[-1]
    nqb = s // SEL_Q_BLOCK
    ksb = k.reshape(b, g, n_sel, SEL_BLOCK, d)
    vsb = v.reshape(b, g, n_sel, SEL_BLOCK, d)
    qb = qg.reshape(b, g, hg, nqb, SEL_Q_BLOCK, d).transpose(3, 0, 1, 2, 4, 5)
    ib = idx.reshape(b, g, nqb, SEL_Q_BLOCK, n_top).transpose(2, 0, 1, 3, 4)
    bi = jnp.arange(b)[:, None, None, None]
    gi = jnp.arange(g)[None, :, None, None]

    def one_block(args):
        i, q_i, idx_i = args
        kg = ksb[bi, gi, idx_i]
        vg = vsb[bi, gi, idx_i]
        sc = jnp.einsum('bghqd,bgqnld->bghqnl', q_i, kg) * ATTN_SCALE
        key_pos = idx_i[..., None] * SEL_BLOCK + jnp.arange(SEL_BLOCK)
        q_pos = i * SEL_Q_BLOCK + jnp.arange(SEL_Q_BLOCK)
        mask = key_pos <= q_pos[None, None, :, None, None]
        m = n_top * SEL_BLOCK
        p = _masked_softmax(sc.reshape(b, g, hg, SEL_Q_BLOCK, m),
                            mask.reshape(b, g, 1, SEL_Q_BLOCK, m))
        return jnp.einsum('bghqm,bgqmd->bghqd', p, vg.reshape(b, g, SEL_Q_BLOCK, m, d))

    out = lax.map(one_block, (jnp.arange(nqb), qb, ib))
    return out.transpose(1, 2, 3, 0, 4, 5).reshape(b, g, hg, s, d)


def _window_attention(qg, k, v):
    b, g, hg, s, d = qg.shape
    nb = s // Q_BLOCK
    span = WINDOW + Q_BLOCK
    kp = jnp.pad(k, ((0, 0), (0, 0), (WINDOW, 0), (0, 0)))
    vp = jnp.pad(v, ((0, 0), (0, 0), (WINDOW, 0), (0, 0)))
    qb = qg.reshape(b, g, hg, nb, Q_BLOCK, d).transpose(3, 0, 1, 2, 4, 5)

    def one_block(args):
        i, q_i = args
        start = i * Q_BLOCK
        k_i = lax.dynamic_slice_in_dim(kp, start, span, axis=2)
        v_i = lax.dynamic_slice_in_dim(vp, start, span, axis=2)
        q_pos = start + jnp.arange(Q_BLOCK)
        k_pos = start - WINDOW + jnp.arange(span)
        mask = ((k_pos[None, :] <= q_pos[:, None]) & (k_pos[None, :] > q_pos[:, None] - WINDOW)
                & (k_pos[None, :] >= 0))
        sc = jnp.einsum('bghqd,bgkd->bghqk', q_i, k_i) * ATTN_SCALE
        p = _masked_softmax(sc, mask)
        return jnp.einsum('bghqk,bgkd->bghqd', p, v_i)

    out = lax.map(one_block, (jnp.arange(nb), qb))
    return out.transpose(1, 2, 3, 0, 4, 5).reshape(b, g, hg, s, d)


def _native_sparse_attention(nq, kc, vc, ks, vs, kw, vw, gate_logits,
                             pos_k, w1_k, w2_k, pos_v, w1_v, w2_v):
    b, s, _ = nq.shape
    g, hg = NSA_KV_GROUPS, NSA_HEADS_PER_GROUP
    f32 = jnp.float32
    q = _rope(_heads(nq, NSA_HEADS)).astype(f32)
    qg = q.reshape(b, g, hg, s, HEAD_DIM)
    k_cmp = _compress(_rope(_heads(kc, g)).astype(f32), pos_k, w1_k, w2_k)
    v_cmp = _compress(_heads(vc, g).astype(f32), pos_v, w1_v, w2_v)
    n_cmp = k_cmp.shape[2]
    t_pos = jnp.arange(s)
    cmp_end = jnp.arange(n_cmp) * CMP_STRIDE + CMP_BLOCK - 1
    cmp_mask = cmp_end[None, :] <= t_pos[:, None]
    sc = jnp.einsum('bghtd,bgnd->bghtn', qg, k_cmp) * ATTN_SCALE
    p_cmp = _masked_softmax(sc, cmp_mask)
    o_cmp = jnp.einsum('bghtn,bgnd->bghtd', p_cmp, v_cmp)
    n_sel = s // SEL_BLOCK
    ci = jnp.arange(n_cmp)[:, None] * CMP_STRIDE
    sj = jnp.arange(n_sel)[None, :] * SEL_BLOCK
    overlap = ((ci < sj + SEL_BLOCK) & (ci + CMP_BLOCK > sj)).astype(f32)
    p_slc = jnp.einsum('bgtn,ns->bgts', jnp.sum(p_cmp, axis=2), overlap)
    cur = t_pos // SEL_BLOCK
    jj = jnp.arange(n_sel)
    forced = (jj[None, :] == 0) | (jj[None, :] == cur[:, None]) | (jj[None, :] == cur[:, None] - 1)
    valid = jj[None, :] <= cur[:, None]
    score = jnp.where(valid, p_slc + FORCED_BONUS * forced.astype(f32), NEG_INF)
    n_top = min(SEL_TOPN, n_sel)
    _, idx = lax.top_k(score, n_top)
    k_sel = _rope(_heads(ks, g)).astype(f32)
    v_sel = _heads(vs, g).astype(f32)
    o_sel = _selected_attention(qg, k_sel, v_sel, idx)
    k_win = _rope(_heads(kw, g)).astype(f32)
    v_win = _heads(vw, g).astype(f32)
    o_win = _window_attention(qg, k_win, v_win)
    gates = jax.nn.sigmoid(gate_logits.astype(f32)).reshape(b, s, NSA_HEADS, 3).transpose(0, 2, 1, 3)
    gates = gates.reshape(b, g, hg, s, 3)
    o = gates[..., 0:1] * o_cmp + gates[..., 1:2] * o_sel + gates[..., 2:3] * o_win
    o = o.reshape(b, NSA_HEADS, s, HEAD_DIM).transpose(0, 2, 1, 3).reshape(b, s, NSA_WIDTH)
    return o.astype(nq.dtype)


def setup_inputs(seed: int = 0) -> dict:
    key = jax.random.key(seed)
    ks = jax.random.split(key, 20)
    f32 = jnp.float32

    def nrm(k, shape, scale):
        return jax.random.normal(k, shape, f32) * scale

    return {
        'x': nrm(ks[0], (BATCH, SEQ, D_MODEL), 1.0),
        'norm_mix_pre': 1.0 + nrm(ks[1], (DEPTH, D_MODEL), 0.05),
        'norm_mix_post': 1.0 + nrm(ks[2], (DEPTH, D_MODEL), 0.05),
        'norm_mlp_pre': 1.0 + nrm(ks[3], (DEPTH, D_MODEL), 0.05),
        'norm_mlp_post': 1.0 + nrm(ks[4], (DEPTH, D_MODEL), 0.05),
        'w_in': nrm(ks[5], (DEPTH, D_MODEL, IN_COLS), D_MODEL ** -0.5),
        'b_forget': FORGET_BIAS_INIT + nrm(ks[6], (DEPTH, FOX_HEADS), 0.1),
        'cmp_pos_k': nrm(ks[7], (DEPTH, CMP_BLOCK, HEAD_DIM), 0.1),
        'cmp_w1_k': nrm(ks[8], (DEPTH, CMP_BLOCK * HEAD_DIM, HEAD_DIM), (CMP_BLOCK * HEAD_DIM) ** -0.5),
        'cmp_w2_k': nrm(ks[9], (DEPTH, HEAD_DIM, HEAD_DIM), HEAD_DIM ** -0.5),
        'cmp_pos_v': nrm(ks[10], (DEPTH, CMP_BLOCK, HEAD_DIM), 0.1),
        'cmp_w1_v': nrm(ks[11], (DEPTH, CMP_BLOCK * HEAD_DIM, HEAD_DIM), (CMP_BLOCK * HEAD_DIM) ** -0.5),
        'cmp_w2_v': nrm(ks[12], (DEPTH, HEAD_DIM, HEAD_DIM), HEAD_DIM ** -0.5),
        'w_fox_out': nrm(ks[13], (DEPTH, FOX_WIDTH, D_MODEL), FOX_WIDTH ** -0.5),
        'w_nsa_out': nrm(ks[14], (DEPTH, NSA_WIDTH, D_MODEL), NSA_WIDTH ** -0.5),
        'w_o': nrm(ks[15], (DEPTH, D_MODEL, D_MODEL), D_MODEL ** -0.5),
        'w_up': nrm(ks[16], (DEPTH, D_MODEL, D_FF), D_MODEL ** -0.5),
        'w_down': nrm(ks[17], (DEPTH, D_FF, D_MODEL), D_FF ** -0.5),
    }


def reference(x, norm_mix_pre, norm_mix_post, norm_mlp_pre, norm_mlp_post, w_in, b_forget,
              cmp_pos_k, cmp_w1_k, cmp_w2_k, cmp_pos_v, cmp_w1_v, cmp_w2_v,
              w_fox_out, w_nsa_out, w_o, w_up, w_down):
    split_points = [int(p) for p in np.cumsum(IN_SPLITS)[:-1]]
    for layer in range(DEPTH):
        h = _rms_norm(x, norm_mix_pre[layer])
        proj = h @ w_in[layer]
        (fq, fk, fv, ff, nq, kc, vc, ksl, vsl, kwn, vwn, ng, ga, gb) = jnp.split(proj, split_points, axis=-1)
        fox = _forgetting_attention(_heads(fq, FOX_HEADS), _heads(fk, FOX_HEADS), _heads(fv, FOX_HEADS),
                                    ff + b_forget[layer])
        nsa = _native_sparse_attention(nq, kc, vc, ksl, vsl, kwn, vwn, ng,
                                       cmp_pos_k[layer], cmp_w1_k[layer], cmp_w2_k[layer],
                                       cmp_pos_v[layer], cmp_w1_v[layer], cmp_w2_v[layer])
        mix = jax.nn.sigmoid(ga) * (fox @ w_fox_out[layer]) + jax.nn.sigmoid(gb) * (nsa @ w_nsa_out[layer])
        x = x + _rms_norm(mix @ w_o[layer], norm_mix_post[layer])
        h = _rms_norm(x, norm_mlp_pre[layer])
        u = jnp.square(jax.nn.relu(h @ w_up[layer]))
        x = x + _rms_norm(u @ w_down[layer], norm_mlp_post[layer])
    return x
```

```python
import functools

import numpy as np
import jax
import jax.numpy as jnp
from jax import lax
from jax.experimental import pallas as pl
from jax.experimental.pallas import tpu as pltpu

HEAD_DIM = 64
FOX_HEADS = 8
NSA_HEADS = 8
NSA_KV_GROUPS = 2
NSA_HEADS_PER_GROUP = NSA_HEADS // NSA_KV_GROUPS
FOX_WIDTH = FOX_HEADS * HEAD_DIM
NSA_WIDTH = NSA_HEADS * HEAD_DIM
NSA_KV_WIDTH = NSA_KV_GROUPS * HEAD_DIM
ROPE_THETA = 10000.0
CMP_BLOCK = 32
CMP_STRIDE = 16
SEL_BLOCK = 64
SEL_TOPN = 8
WINDOW = 512
RMS_EPS = 1e-6
NEG_INF = -1e30
FORCED_BONUS = 1e4
ATTN_SCALE = HEAD_DIM ** -0.5

LANES = 128
VMEM_LIMIT_BYTES = 52 * 1024 * 1024
SEL_LANE0 = HEAD_DIM
MAX_SEL_BLOCKS = 32

F32 = jnp.float32
BF16 = jnp.bfloat16
HIGHEST = lax.Precision.HIGHEST

PROJ_TM = 512
FOX_TQ = 256
CMP_TQ = 256
NSA_TQ = 128
TAIL_TM = 256
FF_CHUNK = 1024


def _nt_dot(a, b):
    return lax.dot_general(a, b, (((1,), (1,)), ((), ())), preferred_element_type=F32)


def _sigmoid(t):
    return 1.0 / (1.0 + jnp.exp(-t))


def _resident(shape):
    nd = len(shape)
    return pl.BlockSpec(shape, lambda *_: (0,) * nd, pipeline_mode=pl.Buffered(1))


def _proj_layout():
    widths = [('fq', FOX_WIDTH), ('fk', FOX_WIDTH), ('fv', FOX_WIDTH),
              ('nq', NSA_HEADS * LANES),
              ('ksl', NSA_KV_GROUPS * LANES), ('vsl', NSA_KV_GROUPS * LANES),
              ('kwn', NSA_KV_GROUPS * LANES), ('vwn', NSA_KV_GROUPS * LANES),
              ('kc', NSA_KV_WIDTH), ('vc', NSA_KV_WIDTH),
              ('misc', LANES), ('gates', NSA_KV_GROUPS * LANES)]
    out, c = {}, 0
    for name, w in widths:
        out[name] = (c, w)
        c += w
    return out, c


def _rearrange_w_in(w_in, d_model):
    splits = (FOX_WIDTH, FOX_WIDTH, FOX_WIDTH, FOX_HEADS,
              NSA_WIDTH, NSA_KV_WIDTH, NSA_KV_WIDTH, NSA_KV_WIDTH, NSA_KV_WIDTH, NSA_KV_WIDTH, NSA_KV_WIDTH,
              3 * NSA_HEADS, d_model, d_model)
    pts = [int(p) for p in np.cumsum(splits)[:-1]]
    fq, fk, fv, ff, nq, kc, vc, ksl, vsl, kwn, vwn, ng, ga, gb = jnp.split(w_in, pts, axis=-1)
    d = w_in.shape[0]

    def pad_heads(w, n):
        w = w.reshape(d, n, HEAD_DIM)
        return jnp.concatenate([w, jnp.zeros_like(w)], axis=-1).reshape(d, n * LANES)

    def dup_heads(w, n):
        w = w.reshape(d, n, HEAD_DIM)
        return jnp.concatenate([w, w], axis=-1).reshape(d, n * LANES)

    misc = jnp.concatenate([ff, jnp.zeros((d, LANES - FOX_HEADS), w_in.dtype)], axis=-1)
    per_group = 3 * NSA_HEADS_PER_GROUP
    ngg = ng.reshape(d, NSA_KV_GROUPS, per_group)
    gates = jnp.concatenate([ngg, jnp.zeros((d, NSA_KV_GROUPS, LANES - per_group), w_in.dtype)],
                            axis=-1).reshape(d, NSA_KV_GROUPS * LANES)
    w_small = jnp.concatenate(
        [fq, fk, fv, pad_heads(nq, NSA_HEADS),
         pad_heads(ksl, NSA_KV_GROUPS), dup_heads(vsl, NSA_KV_GROUPS),
         pad_heads(kwn, NSA_KV_GROUPS), dup_heads(vwn, NSA_KV_GROUPS),
         kc, vc, misc, gates], axis=-1)
    return w_small.astype(BF16), ga.astype(BF16), gb.astype(BF16)


def _rope_tables(seq):
    half = HEAD_DIM // 2
    inv = jnp.power(jnp.float32(ROPE_THETA), -jnp.arange(0, HEAD_DIM, 2, dtype=F32) / HEAD_DIM)
    ang = jnp.arange(seq, dtype=F32)[:, None] * inv[None, :]
    cos, sin = jnp.cos(ang), jnp.sin(ang)
    zero = jnp.zeros_like(sin)
    cos_t = jnp.concatenate([cos, cos, cos, cos], axis=-1)
    s_lo = jnp.concatenate([-sin, zero, -sin, zero], axis=-1)
    s_hi = jnp.concatenate([zero, sin, zero, sin], axis=-1)
    assert cos_t.shape == (seq, 4 * half)
    return cos_t, s_lo, s_hi


def _proj_kernel(x_ref, g_ref, w_ref, wga_ref, wgb_ref, cos_ref, slo_ref, shi_ref, bmisc_ref,
                 fq_ref, fk_ref, fv_ref, nq_ref, ksl_ref, vsl_ref, kwn_ref, vwn_ref,
                 kc_ref, vc_ref, misc_ref, gates_ref, ga_ref, gb_ref, *, tm, seq, layout):
    x = x_ref[...]
    h = x * lax.rsqrt(jnp.mean(x * x, axis=-1, keepdims=True) + RMS_EPS) * g_ref[...]
    h = h.astype(BF16)
    cos, slo, shi = cos_ref[...], slo_ref[...], shi_ref[...]
    half = HEAD_DIM // 2

    def mm(name):
        c0, n = layout[name]
        return jnp.dot(h, w_ref[:, c0:c0 + n], preferred_element_type=F32)

    def rope_block(t):
        return t * cos + pltpu.roll(t, LANES - half, axis=1) * slo + pltpu.roll(t, half, axis=1) * shi

    def rope(t):
        n = t.shape[1] // LANES
        return [rope_block(t[:, b * LANES:(b + 1) * LANES]) for b in range(n)]

    fq_ref[...] = (mm('fq') * ATTN_SCALE).astype(BF16)
    fk_ref[...] = mm('fk').astype(BF16)
    fv_ref[...] = mm('fv').astype(BF16)

    for b, blk in enumerate(rope(mm('nq'))):
        nq_ref[:, b * LANES:(b + 1) * LANES] = (blk * ATTN_SCALE).astype(BF16)

    lane = lax.broadcasted_iota(jnp.int32, (tm, LANES), 1)
    pos = (pl.program_id(0) % (seq // tm)) * tm + lax.broadcasted_iota(jnp.int32, (tm, LANES), 0)
    onehot = (lane - SEL_LANE0) == (pos // SEL_BLOCK)
    for b, blk in enumerate(rope(mm('ksl'))):
        ksl_ref[:, b * LANES:(b + 1) * LANES] = jnp.where(onehot, 1.0, blk).astype(BF16)
    vsl_ref[...] = mm('vsl').astype(BF16)
    for b, blk in enumerate(rope(mm('kwn'))):
        kwn_ref[:, b * LANES:(b + 1) * LANES] = blk.astype(BF16)
    vwn_ref[...] = mm('vwn').astype(BF16)

    kc_ref[...] = rope(mm('kc'))[0]
    vc_ref[...] = mm('vc')

    t = mm('misc') + bmisc_ref[...]
    misc_ref[...] = -(jnp.maximum(-t, 0.0) + jnp.log(1.0 + jnp.exp(-jnp.abs(t))))
    gates_ref[...] = _sigmoid(mm('gates'))

    ga_ref[...] = _sigmoid(jnp.dot(h, wga_ref[...], preferred_element_type=F32)).astype(BF16)
    gb_ref[...] = _sigmoid(jnp.dot(h, wgb_ref[...], preferred_element_type=F32)).astype(BF16)


def _proj(x2, g, w_small, w_ga, w_gb, tables, bmisc, seq):
    n_tok, d = x2.shape
    tm = PROJ_TM
    layout, n_cols = _proj_layout()
    assert n_tok % tm == 0 and seq % tm == 0 and w_small.shape == (d, n_cols)
    tiles_per_seq = seq // tm

    def row(w):
        return pl.BlockSpec((tm, w), lambda i: (i, 0))

    def tab():
        return pl.BlockSpec((tm, LANES), lambda i: (i % tiles_per_seq, 0))

    outs = [('fq', BF16), ('fk', BF16), ('fv', BF16), ('nq', BF16), ('ksl', BF16), ('vsl', BF16),
            ('kwn', BF16), ('vwn', BF16), ('kc', F32), ('vc', F32), ('misc', F32), ('gates', F32)]
    out_shape = [jax.ShapeDtypeStruct((n_tok, layout[n][1]), dt) for n, dt in outs]
    out_specs = [row(layout[n][1]) for n, _ in outs]
    out_shape += [jax.ShapeDtypeStruct((n_tok, d), BF16)] * 2
    out_specs += [row(d), row(d)]
    return pl.pallas_call(
        functools.partial(_proj_kernel, tm=tm, seq=seq, layout=layout),
        grid=(n_tok // tm,),
        in_specs=[row(d), _resident((1, d)), _resident(w_small.shape), _resident(w_ga.shape),
                  _resident(w_gb.shape), tab(), tab(), tab(), _resident((1, LANES))],
        out_specs=out_specs,
        out_shape=out_shape,
        compiler_params=pltpu.CompilerParams(dimension_semantics=("arbitrary",),
                                             vmem_limit_bytes=VMEM_LIMIT_BYTES),
        name="proj",
    )(x2, g, w_small, w_ga, w_gb, *tables, bmisc)


def _cumsum_kernel(m_ref, ccol_ref, crow_ref, *, seq):
    r = lax.broadcasted_iota(jnp.int32, (LANES, LANES), 0)
    c = lax.broadcasted_iota(jnp.int32, (LANES, LANES), 1)
    tri = (r >= c).astype(F32)
    carry = jnp.zeros((1, LANES), F32)
    for b in range(seq // LANES):
        blk = m_ref[0, b * LANES:(b + 1) * LANES, :]
        cs = jnp.dot(tri, blk, precision=HIGHEST, preferred_element_type=F32) + carry
        ccol_ref[0, b * LANES:(b + 1) * LANES, :] = cs
        carry = cs[LANES - 1:LANES, :]
    crow_ref[0] = ccol_ref[0].T[0:FOX_HEADS, :]


def _cumsum(misc3):
    b, seq, _ = misc3.shape
    assert seq % LANES == 0
    return pl.pallas_call(
        functools.partial(_cumsum_kernel, seq=seq),
        grid=(b,),
        in_specs=[pl.BlockSpec((1, seq, LANES), lambda i: (i, 0, 0))],
        out_specs=[pl.BlockSpec((1, seq, LANES), lambda i: (i, 0, 0)),
                   pl.BlockSpec((1, FOX_HEADS, seq), lambda i: (i, 0, 0))],
        out_shape=[jax.ShapeDtypeStruct((b, seq, LANES), F32),
                   jax.ShapeDtypeStruct((b, FOX_HEADS, seq), F32)],
        compiler_params=pltpu.CompilerParams(dimension_semantics=("arbitrary",),
                                             vmem_limit_bytes=VMEM_LIMIT_BYTES),
        name="cumsum",
    )(misc3)


def _flash_step(s, v, m_s, l_s, acc_s):
    m_prev = m_s[...]
    m_new = jnp.maximum(m_prev, jnp.max(s, axis=-1, keepdims=True))
    alpha = jnp.exp(m_prev - m_new)
    p = jnp.exp(s - m_new)
    l_s[...] = alpha * l_s[...] + jnp.sum(p, axis=-1, keepdims=True)
    acc_s[...] = alpha * acc_s[...] + jnp.dot(p.astype(BF16), v, preferred_element_type=F32)
    m_s[...] = m_new


def _flash_init(m_s, l_s, acc_s):
    m_s[...] = jnp.full(m_s.shape, NEG_INF, F32)
    l_s[...] = jnp.zeros(l_s.shape, F32)
    acc_s[...] = jnp.zeros(acc_s.shape, F32)


def _fox_kernel(q_ref, k_ref, v_ref, ccol_ref, crow_ref, o_ref, m_s, l_s, acc_s, *, tq):
    i = pl.program_id(1)
    lane = lax.broadcasted_iota(jnp.int32, (tq, LANES), 1)
    causal = (lax.broadcasted_iota(jnp.int32, (tq, tq), 0) >= lax.broadcasted_iota(jnp.int32, (tq, tq), 1))
    for hp in range(FOX_HEADS // 2):
        cols = slice(hp * LANES, (hp + 1) * LANES)
        qp = q_ref[0, :, cols]
        outs = []
        for hh in range(2):
            h = 2 * hp + hh
            qh = jnp.where((lane < HEAD_DIM) == (hh == 0), qp, jnp.zeros_like(qp))
            cq = ccol_ref[0, :, h:h + 1]
            _flash_init(m_s, l_s, acc_s)

            def tile(j, masked, qh=qh, cq=cq, h=h, cols=cols):
                rows = pl.ds(pl.multiple_of(j * tq, tq), tq)
                s = _nt_dot(qh, k_ref[0, rows, cols]) + cq - crow_ref[0, h, pl.ds(j, 1), :]
                if masked:
                    s = jnp.where(causal, s, NEG_INF)
                _flash_step(s, v_ref[0, rows, cols], m_s, l_s, acc_s)

            def body(j, carry, tile=tile):
                tile(j, False)
                return carry

            lax.fori_loop(0, i, body, 0)
            tile(i, True)
            outs.append(acc_s[...] / l_s[...])
        o_ref[0, :, cols] = jnp.where(lane < HEAD_DIM, outs[0], outs[1]).astype(BF16)


def _fox(fq, fk, fv, ccol, crow):
    b, seq, w = fq.shape
    tq = FOX_TQ
    assert seq % tq == 0
    nq = seq // tq
    crow4 = crow.reshape(b, FOX_HEADS, nq, tq)
    whole = pl.BlockSpec((1, seq, w), lambda bi, i: (bi, 0, 0))
    return pl.pallas_call(
        functools.partial(_fox_kernel, tq=tq),
        grid=(b, nq),
        in_specs=[pl.BlockSpec((1, tq, w), lambda bi, i: (bi, i, 0)), whole, whole,
                  pl.BlockSpec((1, tq, LANES), lambda bi, i: (bi, i, 0)),
                  pl.BlockSpec((1, FOX_HEADS, nq, tq), lambda bi, i: (bi, 0, 0, 0))],
        out_specs=pl.BlockSpec((1, tq, w), lambda bi, i: (bi, i, 0)),
        out_shape=jax.ShapeDtypeStruct((b, seq, w), BF16),
        scratch_shapes=[pltpu.VMEM((tq, 1), F32), pltpu.VMEM((tq, 1), F32), pltpu.VMEM((tq, LANES), F32)],
        compiler_params=pltpu.CompilerParams(dimension_semantics=("arbitrary", "arbitrary"),
                                             vmem_limit_bytes=VMEM_LIMIT_BYTES),
        name="fox",
    )(fq, fk, fv, ccol, crow4)


def _compress_kernel(yk_ref, yv_ref, posk_ref, w1k_ref, w2k_ref, posv_ref, w1v_ref, w2v_ref,
                     kc_ref, vc_ref, *, nb):
    half = CMP_STRIDE * HEAD_DIM

    def comp(y_ref, pos_ref, w1_ref, w2_ref):
        y = y_ref[0, 0]
        pos = pos_ref[...]
        a = jnp.dot(y + pos[:, :half], w1_ref[0:half, :], precision=HIGHEST, preferred_element_type=F32)
        bm = jnp.dot(y + pos[:, half:], w1_ref[half:2 * half, :], precision=HIGHEST,
                     preferred_element_type=F32)
        pre = a + pltpu.roll(bm, nb - 1, axis=0)
        return jnp.dot(pre * _sigmoid(pre), w2_ref[...], precision=HIGHEST, preferred_element_type=F32)

    kc_ref[0, 0] = comp(yk_ref, posk_ref, w1k_ref, w2k_ref).astype(BF16)
    vc_ref[0, 0] = comp(yv_ref, posv_ref, w1v_ref, w2v_ref).astype(BF16)


def _compress(kc, vc, b, seq, pos_k, w1_k, w2_k, pos_v, w1_v, w2_v):
    g = NSA_KV_GROUPS
    nb = seq // CMP_STRIDE
    half = CMP_STRIDE * HEAD_DIM
    assert CMP_BLOCK == 2 * CMP_STRIDE and seq % CMP_STRIDE == 0 and nb % 8 == 0

    def regroup(t):
        t = t.reshape(b, nb, CMP_STRIDE, g, HEAD_DIM)
        return t.transpose(0, 3, 1, 2, 4).reshape(b, g, nb, half)

    zeros1 = jnp.zeros((CMP_BLOCK * HEAD_DIM, HEAD_DIM), F32)
    zeros2 = jnp.zeros((HEAD_DIM, HEAD_DIM), F32)
    w1k = jnp.concatenate([w1_k, zeros1], axis=-1)
    w1v = jnp.concatenate([w1_v, zeros1], axis=-1)
    w2k = jnp.concatenate([jnp.concatenate([w2_k, zeros2], axis=-1), jnp.zeros((HEAD_DIM, LANES), F32)], axis=0)
    w2v = jnp.concatenate([jnp.concatenate([w2_v, w2_v], axis=-1), jnp.zeros((HEAD_DIM, LANES), F32)], axis=0)
    yspec = pl.BlockSpec((1, 1, nb, half), lambda bi, gi: (bi, gi, 0, 0))
    ospec = pl.BlockSpec((1, 1, nb, LANES), lambda bi, gi: (bi, gi, 0, 0))
    return pl.pallas_call(
        functools.partial(_compress_kernel, nb=nb),
        grid=(b, g),
        in_specs=[yspec, yspec,
                  _resident((1, 2 * half)), _resident(w1k.shape), _resident(w2k.shape),
                  _resident((1, 2 * half)), _resident(w1v.shape), _resident(w2v.shape)],
        out_specs=[ospec, ospec],
        out_shape=[jax.ShapeDtypeStruct((b, g, nb, LANES), BF16)] * 2,
        compiler_params=pltpu.CompilerParams(dimension_semantics=("arbitrary", "arbitrary"),
                                             vmem_limit_bytes=VMEM_LIMIT_BYTES),
        name="compress",
    )(regroup(kc), regroup(vc), pos_k.reshape(1, 2 * half), w1k, w2k, pos_v.reshape(1, 2 * half), w1v, w2v)


def _overlap_matrix(nb, n_cmp, n_sel):
    ci = np.arange(nb)[:, None] * CMP_STRIDE
    sj = (np.arange(LANES)[None, :] - SEL_LANE0) * SEL_BLOCK
    jj = np.arange(LANES)[None, :] - SEL_LANE0
    ov = (ci < sj + SEL_BLOCK) & (ci + CMP_BLOCK > sj) & (jj >= 0) & (jj < n_sel) & (np.arange(nb)[:, None] < n_cmp)
    return jnp.asarray(ov.astype(np.float32))


def _cmpsel_kernel(q_ref, kc_ref, vc_ref, ov_ref, qs_ref, ocmp_ref, *, tq, n_cmp, nb):
    i = pl.program_id(2)
    lane = lax.broadcasted_iota(jnp.int32, (tq, LANES), 1)
    t_pos = i * tq + lax.broadcasted_iota(jnp.int32, (tq, LANES), 0)
    nidx = lax.broadcasted_iota(jnp.int32, (tq, nb), 1)
    t_pos_n = i * tq + lax.broadcasted_iota(jnp.int32, (tq, nb), 0)
    cmask = (nidx * CMP_STRIDE + (CMP_BLOCK - 1) <= t_pos_n) & (nidx < n_cmp)
    kc = kc_ref[0, 0]
    vc = vc_ref[0, 0]
    psum = jnp.zeros((tq, nb), F32)
    outs, qf = [], []
    for h in range(NSA_HEADS_PER_GROUP):
        q = q_ref[0, :, h * LANES:(h + 1) * LANES]
        qf.append(q.astype(F32))
        s = jnp.where(cmask, _nt_dot(q, kc), NEG_INF)
        m = jnp.max(s, axis=-1, keepdims=True)
        p = jnp.where(cmask, jnp.exp(s - m), 0.0)
        p = p / jnp.maximum(jnp.sum(p, axis=-1, keepdims=True), 1e-30)
        psum = psum + p
        outs.append(jnp.dot(p.astype(BF16), vc, preferred_element_type=F32))
    for hp in range(NSA_HEADS_PER_GROUP // 2):
        ocmp_ref[0, :, hp * LANES:(hp + 1) * LANES] = jnp.where(lane < HEAD_DIM, outs[2 * hp], outs[2 * hp + 1])

    p_slc = jnp.dot(psum, ov_ref[...], precision=HIGHEST, preferred_element_type=F32)
    j = lane - SEL_LANE0
    cur = t_pos // SEL_BLOCK
    in_range = (j >= 0) & (j < MAX_SEL_BLOCKS)
    forced = (j == 0) | (j == cur) | (j == cur - 1)
    score = jnp.where(j <= cur, p_slc + FORCED_BONUS * forced.astype(F32), NEG_INF)
    score = jnp.where(in_range, score, -3e38)
    lane_f = lane.astype(F32)
    sel = jnp.zeros((tq, LANES), jnp.bool_)
    for _ in range(SEL_TOPN):
        mx = jnp.max(score, axis=-1, keepdims=True)
        first = jnp.min(jnp.where(score == mx, lane_f, 1e4), axis=-1, keepdims=True)
        chosen = lane_f == first
        sel = sel | chosen
        score = jnp.where(chosen, -jnp.inf, score)
    bias = jnp.where(in_range & jnp.logical_not(sel), NEG_INF, 0.0)
    for h in range(NSA_HEADS_PER_GROUP):
        qs_ref[0, :, h * LANES:(h + 1) * LANES] = (qf[h] + bias).astype(BF16)


def _cmpsel(nq3, kcmp, vcmp):
    b, seq, _ = nq3.shape
    g, hg = NSA_KV_GROUPS, NSA_HEADS_PER_GROUP
    tq = CMP_TQ
    nb = kcmp.shape[2]
    n_cmp = (seq - CMP_BLOCK) // CMP_STRIDE + 1
    n_sel = seq // SEL_BLOCK
    assert seq % tq == 0 and n_sel <= MAX_SEL_BLOCKS and n_sel >= SEL_TOPN
    ov = _overlap_matrix(nb, n_cmp, n_sel)
    qspec = pl.BlockSpec((1, tq, hg * LANES), lambda bi, gi, i: (bi, i, gi))
    cspec = pl.BlockSpec((1, 1, nb, LANES), lambda bi, gi, i: (bi, gi, 0, 0))
    return pl.pallas_call(
        functools.partial(_cmpsel_kernel, tq=tq, n_cmp=n_cmp, nb=nb),
        grid=(b, g, seq // tq),
        in_specs=[qspec, cspec, cspec, _resident(ov.shape)],
        out_specs=[qspec, pl.BlockSpec((1, tq, hg * HEAD_DIM), lambda bi, gi, i: (bi, i, gi))],
        out_shape=[jax.ShapeDtypeStruct(nq3.shape, BF16),
                   jax.ShapeDtypeStruct((b, seq, NSA_WIDTH), F32)],
        compiler_params=pltpu.CompilerParams(dimension_semantics=("arbitrary",) * 3,
                                             vmem_limit_bytes=VMEM_LIMIT_BYTES),
        name="cmpsel",
    )(nq3, kcmp, vcmp, ov)


def _nsa_kernel(qs_ref, ks_ref, vs_ref, kw_ref, vw_ref, ocmp_ref, gates_ref, o_ref,
                m_s, l_s, acc_s, *, tq):
    i = pl.program_id(2)
    hg = NSA_HEADS_PER_GROUP
    rows_all = hg * tq
    wt = WINDOW // tq + 1
    q = jnp.concatenate([qs_ref[0, :, h * LANES:(h + 1) * LANES] for h in range(hg)], axis=0)

    rt = lax.broadcasted_iota(jnp.int32, (rows_all, tq), 0) & (tq - 1)
    causal = rt >= lax.broadcasted_iota(jnp.int32, (rows_all, tq), 1)
    _flash_init(m_s, l_s, acc_s)

    def tile(j, masked):
        rows = pl.ds(pl.multiple_of(j * tq, tq), tq)
        s = _nt_dot(q, ks_ref[0, rows, :])
        if masked:
            s = jnp.where(causal, s, NEG_INF)
        _flash_step(s, vs_ref[0, rows, :], m_s, l_s, acc_s)

    def body(j, carry):
        tile(j, False)
        return carry

    lax.fori_loop(0, i, body, 0)
    tile(i, True)
    o_sel = acc_s[...] / l_s[...]

    st = jnp.maximum(i - (wt - 1), 0)
    wrows = pl.ds(pl.multiple_of(st * tq, tq), wt * tq)
    q_pos = i * tq + (lax.broadcasted_iota(jnp.int32, (rows_all, wt * tq), 0) & (tq - 1))
    k_pos = st * tq + lax.broadcasted_iota(jnp.int32, (rows_all, wt * tq), 1)
    wmask = (k_pos <= q_pos) & (k_pos > q_pos - WINDOW)
    s = jnp.where(wmask, _nt_dot(q, kw_ref[0, wrows, :]), NEG_INF)
    m = jnp.max(s, axis=-1, keepdims=True)
    p = jnp.where(wmask, jnp.exp(s - m), 0.0)
    o_win = jnp.dot(p.astype(BF16), vw_ref[0, wrows, :], preferred_element_type=F32)
    o_win = o_win / jnp.maximum(jnp.sum(p, axis=-1, keepdims=True), 1e-30)

    lane = lax.broadcasted_iota(jnp.int32, (tq, LANES), 1)
    left = lane < HEAD_DIM
    gates = gates_ref[0]
    for hp in range(hg // 2):
        h0, h1 = 2 * hp, 2 * hp + 1

        def pair(a):
            return jnp.where(left, a[h0 * tq:(h0 + 1) * tq], a[h1 * tq:(h1 + 1) * tq])

        def gate(branch):
            return jnp.where(left, gates[:, 3 * h0 + branch:3 * h0 + branch + 1],
                             gates[:, 3 * h1 + branch:3 * h1 + branch + 1])

        o = gate(0) * ocmp_ref[0, :, hp * LANES:(hp + 1) * LANES] + gate(1) * pair(o_sel) + gate(2) * pair(o_win)
        o_ref[0, :, hp * LANES:(hp + 1) * LANES] = o.astype(BF16)


def _nsa(qs, ksl, vsl, kwn, vwn, ocmp, gates):
    b, seq, _ = qs.shape
    g, hg = NSA_KV_GROUPS, NSA_HEADS_PER_GROUP
    tq = NSA_TQ
    assert seq % tq == 0 and WINDOW % tq == 0 and seq >= WINDOW + tq and tq & (tq - 1) == 0
    kv = pl.BlockSpec((1, seq, LANES), lambda bi, gi, i: (bi, 0, gi))
    ospec = pl.BlockSpec((1, tq, hg * HEAD_DIM), lambda bi, gi, i: (bi, i, gi))
    return pl.pallas_call(
        functools.partial(_nsa_kernel, tq=tq),
        grid=(b, g, seq // tq),
        in_specs=[pl.BlockSpec((1, tq, hg * LANES), lambda bi, gi, i: (bi, i, gi)), kv, kv, kv, kv,
                  ospec, pl.BlockSpec((1, tq, LANES), lambda bi, gi, i: (bi, i, gi))],
        out_specs=ospec,
        out_shape=jax.ShapeDtypeStruct((b, seq, NSA_WIDTH), BF16),
        scratch_shapes=[pltpu.VMEM((hg * tq, 1), F32), pltpu.VMEM((hg * tq, 1), F32),
                        pltpu.VMEM((hg * tq, LANES), F32)],
        compiler_params=pltpu.CompilerParams(dimension_semantics=("arbitrary",) * 3,
                                             vmem_limit_bytes=VMEM_LIMIT_BYTES),
        name="nsa",
    )(qs, ksl, vsl, kwn, vwn, ocmp, gates)


def _rms(y, g):
    return y * lax.rsqrt(jnp.mean(y * y, axis=-1, keepdims=True) + RMS_EPS) * g


def _tail_kernel(x_ref, fox_ref, nsa_ref, ga_ref, gb_ref, wf_ref, wn_ref, wo_ref, wup_ref, wdn_ref,
                 gmix_ref, gpre_ref, gpost_ref, o_ref, *, d_ff):
    a = jnp.dot(fox_ref[...], wf_ref[...], preferred_element_type=F32)
    b = jnp.dot(nsa_ref[...], wn_ref[...], preferred_element_type=F32)
    mix = ga_ref[...].astype(F32) * a + gb_ref[...].astype(F32) * b
    y = jnp.dot(mix.astype(BF16), wo_ref[...], preferred_element_type=F32)
    x1 = x_ref[...] + _rms(y, gmix_ref[...])
    h = _rms(x1, gpre_ref[...]).astype(BF16)
    acc = jnp.zeros(x1.shape, F32)
    for c in range(d_ff // FF_CHUNK):
        cols = slice(c * FF_CHUNK, (c + 1) * FF_CHUNK)
        u = jnp.maximum(jnp.dot(h, wup_ref[:, cols], preferred_element_type=F32), 0.0)
        acc = acc + jnp.dot((u * u).astype(BF16), wdn_ref[cols, :], preferred_element_type=F32)
    o_ref[...] = x1 + _rms(acc, gpost_ref[...])


def _tail(x2, fox, nsa, sga, sgb, wf, wn, wo, wup, wdn, g_mix_post, g_mlp_pre, g_mlp_post):
    n_tok, d = x2.shape
    d_ff = wup.shape[1]
    tm = TAIL_TM
    assert n_tok % tm == 0 and d_ff % FF_CHUNK == 0

    def row(w):
        return pl.BlockSpec((tm, w), lambda i: (i, 0))

    return pl.pallas_call(
        functools.partial(_tail_kernel, d_ff=d_ff),
        grid=(n_tok // tm,),
        in_specs=[row(d), row(fox.shape[1]), row(nsa.shape[1]), row(d), row(d),
                  _resident(wf.shape), _resident(wn.shape), _resident(wo.shape),
                  _resident(wup.shape), _resident(wdn.shape),
                  _resident((1, d)), _resident((1, d)), _resident((1, d))],
        out_specs=row(d),
        out_shape=jax.ShapeDtypeStruct((n_tok, d), F32),
        compiler_params=pltpu.CompilerParams(dimension_semantics=("arbitrary",),
                                             vmem_limit_bytes=VMEM_LIMIT_BYTES),
        name="tail",
    )(x2, fox, nsa, sga, sgb, wf, wn, wo, wup, wdn, g_mix_post, g_mlp_pre, g_mlp_post)


def _layer(x, norm_mix_pre, norm_mix_post, norm_mlp_pre, norm_mlp_post, w_in, b_forget,
           cmp_pos_k, cmp_w1_k, cmp_w2_k, cmp_pos_v, cmp_w1_v, cmp_w2_v,
           w_fox_out, w_nsa_out, w_o, w_up, w_down):
    b, seq, d = x.shape
    n_tok = b * seq
    x2 = x.reshape(n_tok, d)
    w_small, w_ga, w_gb = _rearrange_w_in(w_in, d)
    bmisc = jnp.concatenate([b_forget.astype(F32), jnp.zeros((LANES - FOX_HEADS,), F32)]).reshape(1, LANES)
    (fq, fk, fv, nq, ksl, vsl, kwn, vwn, kc, vc, misc, gates, sga, sgb) = _proj(
        x2, norm_mix_pre.reshape(1, d), w_small, w_ga, w_gb, _rope_tables(seq), bmisc, seq)

    def seq3(t):
        return t.reshape(b, seq, t.shape[-1])

    ccol, crow = _cumsum(seq3(misc))
    fox = _fox(seq3(fq), seq3(fk), seq3(fv), ccol, crow)

    kcmp, vcmp = _compress(kc, vc, b, seq, cmp_pos_k, cmp_w1_k, cmp_w2_k, cmp_pos_v, cmp_w1_v, cmp_w2_v)
    qs, ocmp = _cmpsel(seq3(nq), kcmp, vcmp)
    nsa = _nsa(qs, seq3(ksl), seq3(vsl), seq3(kwn), seq3(vwn), ocmp, seq3(gates))

    out = _tail(x2, fox.reshape(n_tok, FOX_WIDTH), nsa.reshape(n_tok, NSA_WIDTH), sga, sgb,
                w_fox_out.astype(BF16), w_nsa_out.astype(BF16), w_o.astype(BF16),
                w_up.astype(BF16), w_down.astype(BF16),
                norm_mix_post.reshape(1, d), norm_mlp_pre.reshape(1, d), norm_mlp_post.reshape(1, d))
    return out.reshape(b, seq, d)


def kernel(x, norm_mix_pre, norm_mix_post, norm_mlp_pre, norm_mlp_post, w_in, b_forget, cmp_pos_k, cmp_w1_k, cmp_w2_k, cmp_pos_v, cmp_w1_v, cmp_w2_v, w_fox_out, w_nsa_out, w_o, w_up, w_down):
    params = (norm_mix_pre, norm_mix_post, norm_mlp_pre, norm_mlp_post, w_in, b_forget,
              cmp_pos_k, cmp_w1_k, cmp_w2_k, cmp_pos_v, cmp_w1_v, cmp_w2_v,
              w_fox_out, w_nsa_out, w_o, w_up, w_down)
    for layer in range(w_in.shape[0]):
        x = _layer(x, *[p[layer] for p in params])
    return x
```

```python
import functools

import numpy as np
import jax
import jax.numpy as jnp
from jax import lax
from jax.experimental import pallas as pl
from jax.experimental.pallas import tpu as pltpu

HEAD_DIM = 64
FOX_HEADS = 8
NSA_HEADS = 8
NSA_KV_GROUPS = 2
NSA_HEADS_PER_GROUP = NSA_HEADS // NSA_KV_GROUPS
FOX_WIDTH = FOX_HEADS * HEAD_DIM
NSA_WIDTH = NSA_HEADS * HEAD_DIM
NSA_KV_WIDTH = NSA_KV_GROUPS * HEAD_DIM
ROPE_THETA = 10000.0
CMP_BLOCK = 32
CMP_STRIDE = 16
SEL_BLOCK = 64
SEL_TOPN = 8
WINDOW = 512
RMS_EPS = 1e-6
NEG_INF = -1e30
FORCED_BONUS = 1e4
ATTN_SCALE = HEAD_DIM ** -0.5
LOG2E = 1.4426950408889634
Q_SCALE = ATTN_SCALE * LOG2E

LANES = 128
VMEM_LIMIT_BYTES = 52 * 1024 * 1024
SEL_LANE0 = HEAD_DIM
MAX_SEL_BLOCKS = 32

F32 = jnp.float32
BF16 = jnp.bfloat16
HIGHEST = lax.Precision.HIGHEST

PROJ_TM = 512
FOX_TQ = 256
CMP_TQ = 256
NSA_TQ = 128
TAIL_TM = 256
FF_CHUNK = 1024
KV_STEP = 512
BIAS_LANE0 = HEAD_DIM


def _nt_dot(a, b):
    return lax.dot_general(a, b, (((1,), (1,)), ((), ())), preferred_element_type=F32)


def _sigmoid(t):
    return 1.0 / (1.0 + jnp.exp(-t))


def _resident(shape):
    nd = len(shape)
    return pl.BlockSpec(shape, lambda *_: (0,) * nd, pipeline_mode=pl.Buffered(1))


def _proj_layout():
    widths = [('fq', FOX_WIDTH), ('fk', FOX_WIDTH), ('fv', FOX_WIDTH),
              ('nq', NSA_HEADS * LANES),
              ('ksl', NSA_KV_GROUPS * LANES), ('vsl', NSA_KV_GROUPS * LANES),
              ('kwn', NSA_KV_GROUPS * LANES), ('vwn', NSA_KV_GROUPS * LANES),
              ('kc', NSA_KV_WIDTH), ('vc', NSA_KV_WIDTH),
              ('misc', LANES), ('gates', NSA_KV_GROUPS * LANES)]
    out, c = {}, 0
    for name, w in widths:
        out[name] = (c, w)
        c += w
    return out, c


def _rearrange_w_in(w_in, d_model):
    splits = (FOX_WIDTH, FOX_WIDTH, FOX_WIDTH, FOX_HEADS,
              NSA_WIDTH, NSA_KV_WIDTH, NSA_KV_WIDTH, NSA_KV_WIDTH, NSA_KV_WIDTH, NSA_KV_WIDTH, NSA_KV_WIDTH,
              3 * NSA_HEADS, d_model, d_model)
    pts = [int(p) for p in np.cumsum(splits)[:-1]]
    fq, fk, fv, ff, nq, kc, vc, ksl, vsl, kwn, vwn, ng, ga, gb = jnp.split(w_in, pts, axis=-1)
    d = w_in.shape[0]

    def pad_heads(w, n):
        w = w.reshape(d, n, HEAD_DIM)
        return jnp.concatenate([w, jnp.zeros_like(w)], axis=-1).reshape(d, n * LANES)

    def dup_heads(w, n):
        w = w.reshape(d, n, HEAD_DIM)
        return jnp.concatenate([w, w], axis=-1).reshape(d, n * LANES)

    misc = jnp.concatenate([ff, jnp.zeros((d, LANES - FOX_HEADS), w_in.dtype)], axis=-1)
    per_group = 3 * NSA_HEADS_PER_GROUP
    ngg = ng.reshape(d, NSA_KV_GROUPS, per_group)
    gates = jnp.concatenate([ngg, jnp.zeros((d, NSA_KV_GROUPS, LANES - per_group), w_in.dtype)],
                            axis=-1).reshape(d, NSA_KV_GROUPS * LANES)
    w_small = jnp.concatenate(
        [fq, fk, fv, pad_heads(nq, NSA_HEADS),
         pad_heads(ksl, NSA_KV_GROUPS), dup_heads(vsl, NSA_KV_GROUPS),
         pad_heads(kwn, NSA_KV_GROUPS), dup_heads(vwn, NSA_KV_GROUPS),
         kc, vc, misc, gates], axis=-1)
    return w_small.astype(BF16), ga.astype(BF16), gb.astype(BF16)


def _rope_tables(seq):
    half = HEAD_DIM // 2
    inv = jnp.power(jnp.float32(ROPE_THETA), -jnp.arange(0, HEAD_DIM, 2, dtype=F32) / HEAD_DIM)
    ang = jnp.arange(seq, dtype=F32)[:, None] * inv[None, :]
    cos, sin = jnp.cos(ang), jnp.sin(ang)
    zero = jnp.zeros_like(sin)
    cos_t = jnp.concatenate([cos, cos, cos, cos], axis=-1)
    s_lo = jnp.concatenate([-sin, zero, -sin, zero], axis=-1)
    s_hi = jnp.concatenate([zero, sin, zero, sin], axis=-1)
    assert cos_t.shape == (seq, 4 * half)
    return cos_t, s_lo, s_hi


def _proj_kernel(x_ref, g_ref, w_ref, wga_ref, wgb_ref, cos_ref, slo_ref, shi_ref, bmisc_ref,
                 fq_ref, fk_ref, fv_ref, nq_ref, ksl_ref, vsl_ref, kwn_ref, vwn_ref,
                 kc_ref, vc_ref, misc_ref, gates_ref, ga_ref, gb_ref, *, tm, seq, layout):
    x = x_ref[...]
    h = x * lax.rsqrt(jnp.mean(x * x, axis=-1, keepdims=True) + RMS_EPS) * g_ref[...]
    h = h.astype(BF16)
    cos, slo, shi = cos_ref[...], slo_ref[...], shi_ref[...]
    half = HEAD_DIM // 2

    def mm(name):
        c0, n = layout[name]
        return jnp.dot(h, w_ref[:, c0:c0 + n], preferred_element_type=F32)

    def rope_block(t):
        return t * cos + pltpu.roll(t, LANES - half, axis=1) * slo + pltpu.roll(t, half, axis=1) * shi

    def rope(t):
        n = t.shape[1] // LANES
        return [rope_block(t[:, b * LANES:(b + 1) * LANES]) for b in range(n)]

    lane = lax.broadcasted_iota(jnp.int32, (tm, LANES), 1)
    left = lane < HEAD_DIM

    def split_heads(t, o_ref, scale):
        for p in range(FOX_HEADS // 2):
            blk = t[:, p * LANES:(p + 1) * LANES] * scale
            o_ref[:, (2 * p) * LANES:(2 * p + 1) * LANES] = jnp.where(left, blk, 0.0).astype(BF16)
            o_ref[:, (2 * p + 1) * LANES:(2 * p + 2) * LANES] = jnp.where(
                left, pltpu.roll(blk, HEAD_DIM, axis=1), 0.0).astype(BF16)

    split_heads(mm('fq'), fq_ref, Q_SCALE)
    split_heads(mm('fk'), fk_ref, 1.0)
    fv = mm('fv')
    for p in range(FOX_HEADS // 2):
        blk = fv[:, p * LANES:(p + 1) * LANES]
        fv_ref[:, (2 * p) * LANES:(2 * p + 1) * LANES] = jnp.where(
            left, blk, (lane == HEAD_DIM).astype(F32)).astype(BF16)
        fv_ref[:, (2 * p + 1) * LANES:(2 * p + 2) * LANES] = jnp.where(
            left, (lane == 0).astype(F32), blk).astype(BF16)

    for b, blk in enumerate(rope(mm('nq'))):
        nq_ref[:, b * LANES:(b + 1) * LANES] = (blk * Q_SCALE).astype(BF16)

    pos = (pl.program_id(0) % (seq // tm)) * tm + lax.broadcasted_iota(jnp.int32, (tm, LANES), 0)
    onehot = (lane - SEL_LANE0) == (pos // SEL_BLOCK)
    for b, blk in enumerate(rope(mm('ksl'))):
        ksl_ref[:, b * LANES:(b + 1) * LANES] = jnp.where(onehot, 1.0, blk).astype(BF16)
    vsl_ref[...] = mm('vsl').astype(BF16)
    for b, blk in enumerate(rope(mm('kwn'))):
        kwn_ref[:, b * LANES:(b + 1) * LANES] = blk.astype(BF16)
    vwn_ref[...] = mm('vwn').astype(BF16)

    kc_ref[...] = rope(mm('kc'))[0]
    vc_ref[...] = mm('vc')

    t = mm('misc') + bmisc_ref[...]
    misc_ref[...] = -(jnp.maximum(-t, 0.0) + jnp.log(1.0 + jnp.exp(-jnp.abs(t))))
    gates_ref[...] = _sigmoid(mm('gates'))

    ga_ref[...] = _sigmoid(jnp.dot(h, wga_ref[...], preferred_element_type=F32)).astype(BF16)
    gb_ref[...] = _sigmoid(jnp.dot(h, wgb_ref[...], preferred_element_type=F32)).astype(BF16)


def _proj(x2, g, w_small, w_ga, w_gb, tables, bmisc, seq):
    n_tok, d = x2.shape
    tm = PROJ_TM
    layout, n_cols = _proj_layout()
    assert n_tok % tm == 0 and seq % tm == 0 and w_small.shape == (d, n_cols)
    tiles_per_seq = seq // tm

    def row(w):
        return pl.BlockSpec((tm, w), lambda i: (i, 0))

    def tab():
        return pl.BlockSpec((tm, LANES), lambda i: (i % tiles_per_seq, 0))

    outs = [('fq', BF16), ('fk', BF16), ('fv', BF16), ('nq', BF16), ('ksl', BF16), ('vsl', BF16),
            ('kwn', BF16), ('vwn', BF16), ('kc', F32), ('vc', F32), ('misc', F32), ('gates', F32)]
    widths = {n: layout[n][1] for n, _ in outs}
    widths.update(fq=FOX_HEADS * LANES, fk=FOX_HEADS * LANES, fv=FOX_HEADS * LANES)
    out_shape = [jax.ShapeDtypeStruct((n_tok, widths[n]), dt) for n, dt in outs]
    out_specs = [row(widths[n]) for n, _ in outs]
    out_shape += [jax.ShapeDtypeStruct((n_tok, d), BF16)] * 2
    out_specs += [row(d), row(d)]
    return pl.pallas_call(
        functools.partial(_proj_kernel, tm=tm, seq=seq, layout=layout),
        grid=(n_tok // tm,),
        in_specs=[row(d), _resident((1, d)), _resident(w_small.shape), _resident(w_ga.shape),
                  _resident(w_gb.shape), tab(), tab(), tab(), _resident((1, LANES))],
        out_specs=out_specs,
        out_shape=out_shape,
        compiler_params=pltpu.CompilerParams(dimension_semantics=("arbitrary",),
                                             vmem_limit_bytes=VMEM_LIMIT_BYTES),
        name="proj",
    )(x2, g, w_small, w_ga, w_gb, *tables, bmisc)


def _cumsum_kernel(m_ref, ccol_ref, *, seq):
    r = lax.broadcasted_iota(jnp.int32, (LANES, LANES), 0)
    c = lax.broadcasted_iota(jnp.int32, (LANES, LANES), 1)
    tri = (r >= c).astype(F32)
    carry = jnp.zeros((1, LANES), F32)
    for b in range(seq // LANES):
        blk = m_ref[0, b * LANES:(b + 1) * LANES, :]
        cs = jnp.dot(tri, blk, precision=HIGHEST, preferred_element_type=F32) + carry
        ccol_ref[0, b * LANES:(b + 1) * LANES, :] = cs
        carry = cs[LANES - 1:LANES, :]


def _cumsum(misc3):
    b, seq, _ = misc3.shape
    assert seq % LANES == 0
    return pl.pallas_call(
        functools.partial(_cumsum_kernel, seq=seq),
        grid=(b,),
        in_specs=[pl.BlockSpec((1, seq, LANES), lambda i: (i, 0, 0))],
        out_specs=pl.BlockSpec((1, seq, LANES), lambda i: (i, 0, 0)),
        out_shape=jax.ShapeDtypeStruct((b, seq, LANES), F32),
        compiler_params=pltpu.CompilerParams(dimension_semantics=("arbitrary",),
                                             vmem_limit_bytes=VMEM_LIMIT_BYTES),
        name="cumsum",
    )(misc3)


def _softmax_step(s, pv, m_ref, acc_ref, rows):
    m_prev = m_ref[rows]
    m_new = jnp.maximum(m_prev, jnp.max(s, axis=-1, keepdims=True))
    p = jnp.exp2(s - m_new).astype(BF16)
    acc_ref[rows] = jnp.exp2(m_prev - m_new) * acc_ref[rows] + pv(p)
    m_ref[rows] = m_new


def _split3(c):
    hi = c.astype(BF16).astype(F32)
    r = c - hi
    mid = r.astype(BF16).astype(F32)
    lo = (r - mid).astype(BF16).astype(F32)
    return hi, mid, lo


def _fox_kernel(q_ref, k_ref, v_ref, ccol_ref, o_ref, kx_s, m_s, acc_s, *, tq, tk, seq):
    i = pl.program_id(1)
    lane = lax.broadcasted_iota(jnp.int32, (tq, LANES), 1)
    left = lane < HEAD_DIM
    b0 = BIAS_LANE0

    @pl.when(i == 0)
    def _build_keys():
        def chunk(r, carry):
            rows = pl.ds(pl.multiple_of(r * tq, tq), tq)
            for h in range(FOX_HEADS):
                cols = slice(h * LANES, (h + 1) * LANES)
                hi, mid, lo = _split3(ccol_ref[0, rows, h:h + 1] * LOG2E)
                e = jnp.where(lane < b0 + 3, 1.0,
                              jnp.where(lane == b0 + 3, -hi,
                                        jnp.where(lane == b0 + 4, -mid, jnp.where(lane == b0 + 5, -lo, 0.0))))
                kx_s[rows, cols] = jnp.where(left, k_ref[0, rows, cols].astype(F32), e).astype(BF16)
            return carry

        lax.fori_loop(0, seq // tq, chunk, 0)

    qrows = pl.ds(pl.multiple_of(i * tq, tq), tq)
    n_full = (i * tq) // tk
    q_pos = i * tq + lax.broadcasted_iota(jnp.int32, (tq, tk), 0)
    k_off = lax.broadcasted_iota(jnp.int32, (tq, tk), 1)
    for hp in range(FOX_HEADS // 2):
        heads = (2 * hp, 2 * hp + 1)
        qx = []
        for h in heads:
            hi, mid, lo = _split3(ccol_ref[0, qrows, h:h + 1] * LOG2E)
            e = jnp.where(lane == b0, hi,
                          jnp.where(lane == b0 + 1, mid,
                                    jnp.where(lane == b0 + 2, lo, jnp.where(lane < b0 + 6, 1.0, 0.0))))
            qx.append(jnp.where(left, q_ref[0, :, h * LANES:(h + 1) * LANES].astype(F32), e).astype(BF16))
        m_s[...] = jnp.full(m_s.shape, NEG_INF, F32)
        acc_s[...] = jnp.zeros(acc_s.shape, F32)

        def step(j, masked, qx=qx, heads=heads):
            rows = pl.ds(pl.multiple_of(j * tk, tk), tk)
            for slot, h in enumerate(heads):
                cols = slice(h * LANES, (h + 1) * LANES)
                s = _nt_dot(qx[slot], kx_s[rows, cols])
                if masked:
                    s = jnp.where(j * tk + k_off <= q_pos, s, NEG_INF)
                _softmax_step(s, lambda p, cols=cols: jnp.dot(p, v_ref[0, rows, cols], preferred_element_type=F32),
                              m_s, acc_s, slot)

        def body(j, carry, step=step):
            step(j, False)
            return carry

        lax.fori_loop(0, n_full, body, 0)
        step(n_full, True)
        a0, a1 = acc_s[0], acc_s[1]
        o = jnp.where(left, a0 / a0[:, HEAD_DIM:HEAD_DIM + 1], a1 / a1[:, 0:1])
        o_ref[0, :, hp * LANES:(hp + 1) * LANES] = o.astype(BF16)


def _fox(fq, fk, fv, ccol):
    b, seq, w = fq.shape
    tq, tk = FOX_TQ, KV_STEP
    assert seq % tq == 0 and seq % tk == 0 and tk % tq == 0 and w == FOX_HEADS * LANES
    whole = pl.BlockSpec((1, seq, w), lambda bi, i: (bi, 0, 0))
    return pl.pallas_call(
        functools.partial(_fox_kernel, tq=tq, tk=tk, seq=seq),
        grid=(b, seq // tq),
        in_specs=[pl.BlockSpec((1, tq, w), lambda bi, i: (bi, i, 0)), whole, whole,
                  pl.BlockSpec((1, seq, LANES), lambda bi, i: (bi, 0, 0))],
        out_specs=pl.BlockSpec((1, tq, FOX_WIDTH), lambda bi, i: (bi, i, 0)),
        out_shape=jax.ShapeDtypeStruct((b, seq, FOX_WIDTH), BF16),
        scratch_shapes=[pltpu.VMEM((seq, w), BF16), pltpu.VMEM((2, tq, 1), F32),
                        pltpu.VMEM((2, tq, LANES), F32)],
        compiler_params=pltpu.CompilerParams(dimension_semantics=("arbitrary", "arbitrary"),
                                             vmem_limit_bytes=VMEM_LIMIT_BYTES),
        name="fox",
    )(fq, fk, fv, ccol)


def _compress_kernel(yk_ref, yv_ref, posk_ref, w1k_ref, w2k_ref, posv_ref, w1v_ref, w2v_ref,
                     kc_ref, vc_ref, *, nb):
    half = CMP_STRIDE * HEAD_DIM

    def comp(y_ref, pos_ref, w1_ref, w2_ref):
        y = y_ref[0, 0]
        pos = pos_ref[...]
        a = jnp.dot(y + pos[:, :half], w1_ref[0:half, :], precision=HIGHEST, preferred_element_type=F32)
        bm = jnp.dot(y + pos[:, half:], w1_ref[half:2 * half, :], precision=HIGHEST,
                     preferred_element_type=F32)
        pre = a + pltpu.roll(bm, nb - 1, axis=0)
        return jnp.dot(pre * _sigmoid(pre), w2_ref[...], precision=HIGHEST, preferred_element_type=F32)

    kc_ref[0, 0] = comp(yk_ref, posk_ref, w1k_ref, w2k_ref).astype(BF16)
    vc_ref[0, 0] = comp(yv_ref, posv_ref, w1v_ref, w2v_ref).astype(BF16)


def _compress(kc, vc, b, seq, pos_k, w1_k, w2_k, pos_v, w1_v, w2_v):
    g = NSA_KV_GROUPS
    nb = seq // CMP_STRIDE
    half = CMP_STRIDE * HEAD_DIM
    assert CMP_BLOCK == 2 * CMP_STRIDE and seq % CMP_STRIDE == 0 and nb % 8 == 0

    def regroup(t):
        t = t.reshape(b, nb, CMP_STRIDE, g, HEAD_DIM)
        return t.transpose(0, 3, 1, 2, 4).reshape(b, g, nb, half)

    zeros1 = jnp.zeros((CMP_BLOCK * HEAD_DIM, HEAD_DIM), F32)
    zeros2 = jnp.zeros((HEAD_DIM, HEAD_DIM), F32)
    w1k = jnp.concatenate([w1_k, zeros1], axis=-1)
    w1v = jnp.concatenate([w1_v, zeros1], axis=-1)
    w2k = jnp.concatenate([jnp.concatenate([w2_k, zeros2], axis=-1), jnp.zeros((HEAD_DIM, LANES), F32)], axis=0)
    w2v = jnp.concatenate([jnp.concatenate([w2_v, w2_v], axis=-1), jnp.zeros((HEAD_DIM, LANES), F32)], axis=0)
    yspec = pl.BlockSpec((1, 1, nb, half), lambda bi, gi: (bi, gi, 0, 0))
    ospec = pl.BlockSpec((1, 1, nb, LANES), lambda bi, gi: (bi, gi, 0, 0))
    return pl.pallas_call(
        functools.partial(_compress_kernel, nb=nb),
        grid=(b, g),
        in_specs=[yspec, yspec,
                  _resident((1, 2 * half)), _resident(w1k.shape), _resident(w2k.shape),
                  _resident((1, 2 * half)), _resident(w1v.shape), _resident(w2v.shape)],
        out_specs=[ospec, ospec],
        out_shape=[jax.ShapeDtypeStruct((b, g, nb, LANES), BF16)] * 2,
        compiler_params=pltpu.CompilerParams(dimension_semantics=("arbitrary", "arbitrary"),
                                             vmem_limit_bytes=VMEM_LIMIT_BYTES),
        name="compress",
    )(regroup(kc), regroup(vc), pos_k.reshape(1, 2 * half), w1k, w2k, pos_v.reshape(1, 2 * half), w1v, w2v)


def _overlap_matrix(nb, n_cmp, n_sel):
    ci = np.arange(nb)[:, None] * CMP_STRIDE
    sj = (np.arange(LANES)[None, :] - SEL_LANE0) * SEL_BLOCK
    jj = np.arange(LANES)[None, :] - SEL_LANE0
    ov = (ci < sj + SEL_BLOCK) & (ci + CMP_BLOCK > sj) & (jj >= 0) & (jj < n_sel) & (np.arange(nb)[:, None] < n_cmp)
    return jnp.asarray(ov.astype(np.float32))


def _cmpsel_kernel(q_ref, kc_ref, vc_ref, ov_ref, qs_ref, ocmp_ref, *, tq, n_cmp, nb):
    i = pl.program_id(2)
    lane = lax.broadcasted_iota(jnp.int32, (tq, LANES), 1)
    t_pos = i * tq + lax.broadcasted_iota(jnp.int32, (tq, LANES), 0)
    nidx = lax.broadcasted_iota(jnp.int32, (tq, nb), 1)
    t_pos_n = i * tq + lax.broadcasted_iota(jnp.int32, (tq, nb), 0)
    cmask = (nidx * CMP_STRIDE + (CMP_BLOCK - 1) <= t_pos_n) & (nidx < n_cmp)
    kc = kc_ref[0, 0]
    vc = vc_ref[0, 0]
    psum = jnp.zeros((tq, nb), F32)
    outs, qf = [], []
    for h in range(NSA_HEADS_PER_GROUP):
        q = q_ref[0, :, h * LANES:(h + 1) * LANES]
        qf.append(q.astype(F32))
        s = jnp.where(cmask, _nt_dot(q, kc), NEG_INF)
        m = jnp.max(s, axis=-1, keepdims=True)
        p = jnp.where(cmask, jnp.exp2(s - m), 0.0)
        p = p / jnp.maximum(jnp.sum(p, axis=-1, keepdims=True), 1e-30)
        psum = psum + p
        outs.append(jnp.dot(p.astype(BF16), vc, preferred_element_type=F32))
    for hp in range(NSA_HEADS_PER_GROUP // 2):
        ocmp_ref[0, :, hp * LANES:(hp + 1) * LANES] = jnp.where(lane < HEAD_DIM, outs[2 * hp], outs[2 * hp + 1])

    p_slc = jnp.dot(psum, ov_ref[...], precision=HIGHEST, preferred_element_type=F32)
    j = lane - SEL_LANE0
    cur = t_pos // SEL_BLOCK
    in_range = (j >= 0) & (j < MAX_SEL_BLOCKS)
    forced = (j == 0) | (j == cur) | (j == cur - 1)
    score = jnp.where(j <= cur, p_slc + FORCED_BONUS * forced.astype(F32), NEG_INF)
    score = jnp.where(in_range, score, -3e38)
    lane_f = lane.astype(F32)
    sel = jnp.zeros((tq, LANES), jnp.bool_)
    for _ in range(SEL_TOPN):
        mx = jnp.max(score, axis=-1, keepdims=True)
        first = jnp.min(jnp.where(score == mx, lane_f, 1e4), axis=-1, keepdims=True)
        chosen = lane_f == first
        sel = sel | chosen
        score = jnp.where(chosen, -jnp.inf, score)
    bias = jnp.where(in_range & jnp.logical_not(sel), NEG_INF, 0.0)
    for h in range(NSA_HEADS_PER_GROUP):
        qs_ref[0, :, h * LANES:(h + 1) * LANES] = (qf[h] + bias).astype(BF16)


def _cmpsel(nq3, kcmp, vcmp):
    b, seq, _ = nq3.shape
    g, hg = NSA_KV_GROUPS, NSA_HEADS_PER_GROUP
    tq = CMP_TQ
    nb = kcmp.shape[2]
    n_cmp = (seq - CMP_BLOCK) // CMP_STRIDE + 1
    n_sel = seq // SEL_BLOCK
    assert seq % tq == 0 and n_sel <= MAX_SEL_BLOCKS and n_sel >= SEL_TOPN
    ov = _overlap_matrix(nb, n_cmp, n_sel)
    qspec = pl.BlockSpec((1, tq, hg * LANES), lambda bi, gi, i: (bi, i, gi))
    cspec = pl.BlockSpec((1, 1, nb, LANES), lambda bi, gi, i: (bi, gi, 0, 0))
    return pl.pallas_call(
        functools.partial(_cmpsel_kernel, tq=tq, n_cmp=n_cmp, nb=nb),
        grid=(b, g, seq // tq),
        in_specs=[qspec, cspec, cspec, _resident(ov.shape)],
        out_specs=[qspec, pl.BlockSpec((1, tq, hg * HEAD_DIM), lambda bi, gi, i: (bi, i, gi))],
        out_shape=[jax.ShapeDtypeStruct(nq3.shape, BF16),
                   jax.ShapeDtypeStruct((b, seq, NSA_WIDTH), F32)],
        compiler_params=pltpu.CompilerParams(dimension_semantics=("arbitrary",) * 3,
                                             vmem_limit_bytes=VMEM_LIMIT_BYTES),
        name="cmpsel",
    )(nq3, kcmp, vcmp, ov)


def _nsa_kernel(qs_ref, ks_ref, vs_ref, kw_ref, vw_ref, ocmp_ref, gates_ref, o_ref,
                vse_s, vso_s, vwe_s, vwo_s, m_s, acc_s, *, tq, tk, seq):
    i = pl.program_id(2)
    hg = NSA_HEADS_PER_GROUP
    rows_all = hg * tq
    half = rows_all // 2
    wt = WINDOW // tq + 1

    @pl.when(i == 0)
    def _build_values():
        lane = lax.broadcasted_iota(jnp.int32, (seq, LANES), 1)
        one_e = (lane == HEAD_DIM).astype(BF16)
        one_o = (lane == 0).astype(BF16)
        for src, even, odd in ((vs_ref, vse_s, vso_s), (vw_ref, vwe_s, vwo_s)):
            v = src[0]
            even[...] = jnp.where(lane < HEAD_DIM, v, one_e)
            odd[...] = jnp.where(lane < HEAD_DIM, one_o, v)

    order = tuple(range(0, hg, 2)) + tuple(range(1, hg, 2))
    q = jnp.concatenate([qs_ref[0, :, h * LANES:(h + 1) * LANES] for h in order], axis=0)

    def pv(p, ve, vo):
        return jnp.concatenate([jnp.dot(p[:half], ve, preferred_element_type=F32),
                                jnp.dot(p[half:], vo, preferred_element_type=F32)], axis=0)

    n_full = (i * tq) // tk
    q_pos = i * tq + (lax.broadcasted_iota(jnp.int32, (rows_all, tk), 0) & (tq - 1))
    k_off = lax.broadcasted_iota(jnp.int32, (rows_all, tk), 1)
    m_s[...] = jnp.full(m_s.shape, NEG_INF, F32)
    acc_s[...] = jnp.zeros(acc_s.shape, F32)

    def step(j, masked):
        rows = pl.ds(pl.multiple_of(j * tk, tk), tk)
        s = _nt_dot(q, ks_ref[0, rows, :])
        if masked:
            s = jnp.where(j * tk + k_off <= q_pos, s, NEG_INF)
        _softmax_step(s, lambda p: pv(p, vse_s[rows, :], vso_s[rows, :]), m_s, acc_s, slice(None))

    def body(j, carry):
        step(j, False)
        return carry

    lax.fori_loop(0, n_full, body, 0)
    step(n_full, True)
    a_sel = acc_s[...]

    st = jnp.maximum(i - (wt - 1), 0)
    wrows = pl.ds(pl.multiple_of(st * tq, tq), wt * tq)
    qw_pos = i * tq + (lax.broadcasted_iota(jnp.int32, (rows_all, wt * tq), 0) & (tq - 1))
    kw_pos = st * tq + lax.broadcasted_iota(jnp.int32, (rows_all, wt * tq), 1)
    wmask = (kw_pos <= qw_pos) & (kw_pos > qw_pos - WINDOW)
    s = jnp.where(wmask, _nt_dot(q, kw_ref[0, wrows, :]), NEG_INF)
    p = jnp.exp2(s - jnp.max(s, axis=-1, keepdims=True)).astype(BF16)
    a_win = pv(p, vwe_s[wrows, :], vwo_s[wrows, :])

    lane = lax.broadcasted_iota(jnp.int32, (tq, LANES), 1)
    left = lane < HEAD_DIM
    gates = gates_ref[0]
    for hp in range(hg // 2):
        h0, h1 = 2 * hp, 2 * hp + 1

        def pair(a):
            ae = a[hp * tq:(hp + 1) * tq]
            ao = a[half + hp * tq:half + (hp + 1) * tq]
            return jnp.where(left, ae / jnp.maximum(ae[:, HEAD_DIM:HEAD_DIM + 1], 1e-30),
                             ao / jnp.maximum(ao[:, 0:1], 1e-30))

        def gate(branch):
            return jnp.where(left, gates[:, 3 * h0 + branch:3 * h0 + branch + 1],
                             gates[:, 3 * h1 + branch:3 * h1 + branch + 1])

        o = gate(0) * ocmp_ref[0, :, hp * LANES:(hp + 1) * LANES] + gate(1) * pair(a_sel) + gate(2) * pair(a_win)
        o_ref[0, :, hp * LANES:(hp + 1) * LANES] = o.astype(BF16)


def _nsa(qs, ksl, vsl, kwn, vwn, ocmp, gates):
    b, seq, _ = qs.shape
    g, hg = NSA_KV_GROUPS, NSA_HEADS_PER_GROUP
    tq, tk = NSA_TQ, KV_STEP
    assert seq % tk == 0 and tk % tq == 0 and WINDOW % tq == 0 and seq >= WINDOW + tq and tq & (tq - 1) == 0
    assert hg % 2 == 0
    kv = pl.BlockSpec((1, seq, LANES), lambda bi, gi, i: (bi, 0, gi))
    ospec = pl.BlockSpec((1, tq, hg * HEAD_DIM), lambda bi, gi, i: (bi, i, gi))
    return pl.pallas_call(
        functools.partial(_nsa_kernel, tq=tq, tk=tk, seq=seq),
        grid=(b, g, seq // tq),
        in_specs=[pl.BlockSpec((1, tq, hg * LANES), lambda bi, gi, i: (bi, i, gi)), kv, kv, kv, kv,
                  ospec, pl.BlockSpec((1, tq, LANES), lambda bi, gi, i: (bi, i, gi))],
        out_specs=ospec,
        out_shape=jax.ShapeDtypeStruct((b, seq, NSA_WIDTH), BF16),
        scratch_shapes=[pltpu.VMEM((seq, LANES), BF16)] * 4 + [pltpu.VMEM((hg * tq, 1), F32),
                                                               pltpu.VMEM((hg * tq, LANES), F32)],
        compiler_params=pltpu.CompilerParams(dimension_semantics=("arbitrary",) * 3,
                                             vmem_limit_bytes=VMEM_LIMIT_BYTES),
        name="nsa",
    )(qs, ksl, vsl, kwn, vwn, ocmp, gates)


def _rms(y, g):
    return y * lax.rsqrt(jnp.mean(y * y, axis=-1, keepdims=True) + RMS_EPS) * g


def _tail_kernel(x_ref, fox_ref, nsa_ref, ga_ref, gb_ref, wf_ref, wn_ref, wo_ref, wup_ref, wdn_ref,
                 gmix_ref, gpre_ref, gpost_ref, o_ref, *, d_ff):
    a = jnp.dot(fox_ref[...], wf_ref[...], preferred_element_type=F32)
    b = jnp.dot(nsa_ref[...], wn_ref[...], preferred_element_type=F32)
    mix = ga_ref[...].astype(F32) * a + gb_ref[...].astype(F32) * b
    y = jnp.dot(mix.astype(BF16), wo_ref[...], preferred_element_type=F32)
    x1 = x_ref[...] + _rms(y, gmix_ref[...])
    h = _rms(x1, gpre_ref[...]).astype(BF16)
    acc = jnp.zeros(x1.shape, F32)
    for c in range(d_ff // FF_CHUNK):
        cols = slice(c * FF_CHUNK, (c + 1) * FF_CHUNK)
        u = jnp.maximum(jnp.dot(h, wup_ref[:, cols], preferred_element_type=F32), 0.0)
        acc = acc + jnp.dot((u * u).astype(BF16), wdn_ref[cols, :], preferred_element_type=F32)
    o_ref[...] = x1 + _rms(acc, gpost_ref[...])


def _tail(x2, fox, nsa, sga, sgb, wf, wn, wo, wup, wdn, g_mix_post, g_mlp_pre, g_mlp_post):
    n_tok, d = x2.shape
    d_ff = wup.shape[1]
    tm = TAIL_TM
    assert n_tok % tm == 0 and d_ff % FF_CHUNK == 0

    def row(w):
        return pl.BlockSpec((tm, w), lambda i: (i, 0))

    return pl.pallas_call(
        functools.partial(_tail_kernel, d_ff=d_ff),
        grid=(n_tok // tm,),
        in_specs=[row(d), row(fox.shape[1]), row(nsa.shape[1]), row(d), row(d),
                  _resident(wf.shape), _resident(wn.shape), _resident(wo.shape),
                  _resident(wup.shape), _resident(wdn.shape),
                  _resident((1, d)), _resident((1, d)), _resident((1, d))],
        out_specs=row(d),
        out_shape=jax.ShapeDtypeStruct((n_tok, d), F32),
        compiler_params=pltpu.CompilerParams(dimension_semantics=("arbitrary",),
                                             vmem_limit_bytes=VMEM_LIMIT_BYTES),
        name="tail",
    )(x2, fox, nsa, sga, sgb, wf, wn, wo, wup, wdn, g_mix_post, g_mlp_pre, g_mlp_post)


def _layer(x, norm_mix_pre, norm_mix_post, norm_mlp_pre, norm_mlp_post, w_in, b_forget,
           cmp_pos_k, cmp_w1_k, cmp_w2_k, cmp_pos_v, cmp_w1_v, cmp_w2_v,
           w_fox_out, w_nsa_out, w_o, w_up, w_down):
    b, seq, d = x.shape
    n_tok = b * seq
    x2 = x.reshape(n_tok, d)
    w_small, w_ga, w_gb = _rearrange_w_in(w_in, d)
    bmisc = jnp.concatenate([b_forget.astype(F32), jnp.zeros((LANES - FOX_HEADS,), F32)]).reshape(1, LANES)
    (fq, fk, fv, nq, ksl, vsl, kwn, vwn, kc, vc, misc, gates, sga, sgb) = _proj(
        x2, norm_mix_pre.reshape(1, d), w_small, w_ga, w_gb, _rope_tables(seq), bmisc, seq)

    def seq3(t):
        return t.reshape(b, seq, t.shape[-1])

    fox = _fox(seq3(fq), seq3(fk), seq3(fv), _cumsum(seq3(misc)))

    kcmp, vcmp = _compress(kc, vc, b, seq, cmp_pos_k, cmp_w1_k, cmp_w2_k, cmp_pos_v, cmp_w1_v, cmp_w2_v)
    qs, ocmp = _cmpsel(seq3(nq), kcmp, vcmp)
    nsa = _nsa(qs, seq3(ksl), seq3(vsl), seq3(kwn), seq3(vwn), ocmp, seq3(gates))

    out = _tail(x2, fox.reshape(n_tok, FOX_WIDTH), nsa.reshape(n_tok, NSA_WIDTH), sga, sgb,
                w_fox_out.astype(BF16), w_nsa_out.astype(BF16), w_o.astype(BF16),
                w_up.astype(BF16), w_down.astype(BF16),
                norm_mix_post.reshape(1, d), norm_mlp_pre.reshape(1, d), norm_mlp_post.reshape(1, d))
    return out.reshape(b, seq, d)


def kernel(x, norm_mix_pre, norm_mix_post, norm_mlp_pre, norm_mlp_post, w_in, b_forget, cmp_pos_k, cmp_w1_k, cmp_w2_k, cmp_pos_v, cmp_w1_v, cmp_w2_v, w_fox_out, w_nsa_out, w_o, w_up, w_down):
    params = (norm_mix_pre, norm_mix_post, norm_mlp_pre, norm_mlp_post, w_in, b_forget,
              cmp_pos_k, cmp_w1_k, cmp_w2_k, cmp_pos_v, cmp_w1_v, cmp_w2_v,
              w_fox_out, w_nsa_out, w_o, w_up, w_down)
    for layer in range(w_in.shape[0]):
        x = _layer(x, *[p[layer] for p in params])
    return x
```

```python
import functools

import numpy as np
import jax
import jax.numpy as jnp
from jax import lax
from jax.experimental import pallas as pl
from jax.experimental.pallas import tpu as pltpu

HEAD_DIM = 64
FOX_HEADS = 8
NSA_HEADS = 8
NSA_KV_GROUPS = 2
NSA_HEADS_PER_GROUP = NSA_HEADS // NSA_KV_GROUPS
FOX_WIDTH = FOX_HEADS * HEAD_DIM
NSA_WIDTH = NSA_HEADS * HEAD_DIM
NSA_KV_WIDTH = NSA_KV_GROUPS * HEAD_DIM
ROPE_THETA = 10000.0
CMP_BLOCK = 32
CMP_STRIDE = 16
SEL_BLOCK = 64
SEL_TOPN = 8
WINDOW = 512
RMS_EPS = 1e-6
NEG_INF = -1e30
FORCED_BONUS = 1e4
ATTN_SCALE = HEAD_DIM ** -0.5
LOG2E = 1.4426950408889634
Q_SCALE = ATTN_SCALE * LOG2E

LANES = 128
VMEM_LIMIT_BYTES = 52 * 1024 * 1024
SEL_LANE0 = HEAD_DIM
MAX_SEL_BLOCKS = 32

F32 = jnp.float32
BF16 = jnp.bfloat16
HIGHEST = lax.Precision.HIGHEST

PROJ_TM = 512
FOX_TQ = 256
CMP_TQ = 256
NSA_TQ = 128
TAIL_TM = 256
FF_CHUNK = 1024
KV_STEP = 512
BIAS_LANE0 = HEAD_DIM


def _nt_dot(a, b):
    return lax.dot_general(a, b, (((1,), (1,)), ((), ())), preferred_element_type=F32)


def _sigmoid(t):
    return 1.0 / (1.0 + jnp.exp(-t))


def _resident(shape):
    nd = len(shape)
    return pl.BlockSpec(shape, lambda *_: (0,) * nd, pipeline_mode=pl.Buffered(1))


def _proj_layout():
    widths = [('fq', FOX_WIDTH), ('fk', FOX_WIDTH), ('fv', FOX_WIDTH),
              ('nq', NSA_HEADS * LANES),
              ('ksl', NSA_KV_GROUPS * LANES), ('vsl', NSA_KV_GROUPS * LANES),
              ('kwn', NSA_KV_GROUPS * LANES), ('vwn', NSA_KV_GROUPS * LANES),
              ('kc', NSA_KV_WIDTH), ('vc', NSA_KV_WIDTH),
              ('misc', LANES), ('gates', NSA_KV_GROUPS * LANES)]
    out, c = {}, 0
    for name, w in widths:
        out[name] = (c, w)
        c += w
    return out, c


def _rearrange_w_in(w_in, d_model):
    splits = (FOX_WIDTH, FOX_WIDTH, FOX_WIDTH, FOX_HEADS,
              NSA_WIDTH, NSA_KV_WIDTH, NSA_KV_WIDTH, NSA_KV_WIDTH, NSA_KV_WIDTH, NSA_KV_WIDTH, NSA_KV_WIDTH,
              3 * NSA_HEADS, d_model, d_model)
    pts = [int(p) for p in np.cumsum(splits)[:-1]]
    fq, fk, fv, ff, nq, kc, vc, ksl, vsl, kwn, vwn, ng, ga, gb = jnp.split(w_in, pts, axis=-1)
    d = w_in.shape[0]

    def pad_heads(w, n):
        w = w.reshape(d, n, HEAD_DIM)
        return jnp.concatenate([w, jnp.zeros_like(w)], axis=-1).reshape(d, n * LANES)

    def dup_heads(w, n):
        w = w.reshape(d, n, HEAD_DIM)
        return jnp.concatenate([w, w], axis=-1).reshape(d, n * LANES)

    misc = jnp.concatenate([ff, jnp.zeros((d, LANES - FOX_HEADS), w_in.dtype)], axis=-1)
    per_group = 3 * NSA_HEADS_PER_GROUP
    ngg = ng.reshape(d, NSA_KV_GROUPS, per_group)
    gates = jnp.concatenate([ngg, jnp.zeros((d, NSA_KV_GROUPS, LANES - per_group), w_in.dtype)],
                            axis=-1).reshape(d, NSA_KV_GROUPS * LANES)
    w_small = jnp.concatenate(
        [fq, fk, fv, pad_heads(nq, NSA_HEADS),
         pad_heads(ksl, NSA_KV_GROUPS), dup_heads(vsl, NSA_KV_GROUPS),
         pad_heads(kwn, NSA_KV_GROUPS), dup_heads(vwn, NSA_KV_GROUPS),
         kc, vc, misc, gates], axis=-1)
    return w_small.astype(BF16), ga.astype(BF16), gb.astype(BF16)


def _rope_tables(seq):
    half = HEAD_DIM // 2
    inv = jnp.power(jnp.float32(ROPE_THETA), -jnp.arange(0, HEAD_DIM, 2, dtype=F32) / HEAD_DIM)
    ang = jnp.arange(seq, dtype=F32)[:, None] * inv[None, :]
    cos, sin = jnp.cos(ang), jnp.sin(ang)
    zero = jnp.zeros_like(sin)
    cos_t = jnp.concatenate([cos, cos, cos, cos], axis=-1)
    s_lo = jnp.concatenate([-sin, zero, -sin, zero], axis=-1)
    s_hi = jnp.concatenate([zero, sin, zero, sin], axis=-1)
    assert cos_t.shape == (seq, 4 * half)
    return cos_t, s_lo, s_hi


def _proj_kernel(x_ref, g_ref, w_ref, wga_ref, wgb_ref, cos_ref, slo_ref, shi_ref, bmisc_ref,
                 fq_ref, fk_ref, fv_ref, nq_ref, ksl_ref, vsl_ref, kwn_ref, vwn_ref,
                 kc_ref, vc_ref, misc_ref, gates_ref, ga_ref, gb_ref, *, tm, seq, layout):
    x = x_ref[...]
    h = x * lax.rsqrt(jnp.mean(x * x, axis=-1, keepdims=True) + RMS_EPS) * g_ref[...]
    h = h.astype(BF16)
    cos, slo, shi = cos_ref[...], slo_ref[...], shi_ref[...]
    half = HEAD_DIM // 2

    def mm(name):
        c0, n = layout[name]
        return jnp.dot(h, w_ref[:, c0:c0 + n], preferred_element_type=F32)

    def rope_block(t):
        return t * cos + pltpu.roll(t, LANES - half, axis=1) * slo + pltpu.roll(t, half, axis=1) * shi

    def rope(t):
        n = t.shape[1] // LANES
        return [rope_block(t[:, b * LANES:(b + 1) * LANES]) for b in range(n)]

    lane = lax.broadcasted_iota(jnp.int32, (tm, LANES), 1)
    left = lane < HEAD_DIM

    def split_heads(t, o_ref, scale):
        for p in range(FOX_HEADS // 2):
            blk = t[:, p * LANES:(p + 1) * LANES] * scale
            o_ref[:, (2 * p) * LANES:(2 * p + 1) * LANES] = jnp.where(left, blk, 0.0).astype(BF16)
            o_ref[:, (2 * p + 1) * LANES:(2 * p + 2) * LANES] = jnp.where(
                left, pltpu.roll(blk, HEAD_DIM, axis=1), 0.0).astype(BF16)

    split_heads(mm('fq'), fq_ref, Q_SCALE)
    split_heads(mm('fk'), fk_ref, 1.0)
    fv = mm('fv')
    for p in range(FOX_HEADS // 2):
        blk = fv[:, p * LANES:(p + 1) * LANES]
        fv_ref[:, (2 * p) * LANES:(2 * p + 1) * LANES] = jnp.where(
            left, blk, (lane == HEAD_DIM).astype(F32)).astype(BF16)
        fv_ref[:, (2 * p + 1) * LANES:(2 * p + 2) * LANES] = jnp.where(
            left, (lane == 0).astype(F32), blk).astype(BF16)

    for b, blk in enumerate(rope(mm('nq'))):
        nq_ref[:, b * LANES:(b + 1) * LANES] = (blk * Q_SCALE).astype(BF16)

    pos = (pl.program_id(0) % (seq // tm)) * tm + lax.broadcasted_iota(jnp.int32, (tm, LANES), 0)
    onehot = (lane - SEL_LANE0) == (pos // SEL_BLOCK)
    for b, blk in enumerate(rope(mm('ksl'))):
        ksl_ref[:, b * LANES:(b + 1) * LANES] = jnp.where(onehot, 1.0, blk).astype(BF16)
    vsl_ref[...] = mm('vsl').astype(BF16)
    for b, blk in enumerate(rope(mm('kwn'))):
        kwn_ref[:, b * LANES:(b + 1) * LANES] = blk.astype(BF16)
    vwn_ref[...] = mm('vwn').astype(BF16)

    kc_ref[...] = rope(mm('kc'))[0]
    vc_ref[...] = mm('vc')

    t = mm('misc') + bmisc_ref[...]
    misc_ref[...] = -(jnp.maximum(-t, 0.0) + jnp.log(1.0 + jnp.exp(-jnp.abs(t))))
    gates_ref[...] = _sigmoid(mm('gates'))

    ga_ref[...] = _sigmoid(jnp.dot(h, wga_ref[...], preferred_element_type=F32)).astype(BF16)
    gb_ref[...] = _sigmoid(jnp.dot(h, wgb_ref[...], preferred_element_type=F32)).astype(BF16)


def _proj(x2, g, w_small, w_ga, w_gb, tables, bmisc, seq):
    n_tok, d = x2.shape
    tm = PROJ_TM
    layout, n_cols = _proj_layout()
    assert n_tok % tm == 0 and seq % tm == 0 and w_small.shape == (d, n_cols)
    tiles_per_seq = seq // tm

    def row(w):
        return pl.BlockSpec((tm, w), lambda i: (i, 0))

    def tab():
        return pl.BlockSpec((tm, LANES), lambda i: (i % tiles_per_seq, 0))

    outs = [('fq', BF16), ('fk', BF16), ('fv', BF16), ('nq', BF16), ('ksl', BF16), ('vsl', BF16),
            ('kwn', BF16), ('vwn', BF16), ('kc', F32), ('vc', F32), ('misc', F32), ('gates', F32)]
    widths = {n: layout[n][1] for n, _ in outs}
    widths.update(fq=FOX_HEADS * LANES, fk=FOX_HEADS * LANES, fv=FOX_HEADS * LANES)
    out_shape = [jax.ShapeDtypeStruct((n_tok, widths[n]), dt) for n, dt in outs]
    out_specs = [row(widths[n]) for n, _ in outs]
    out_shape += [jax.ShapeDtypeStruct((n_tok, d), BF16)] * 2
    out_specs += [row(d), row(d)]
    return pl.pallas_call(
        functools.partial(_proj_kernel, tm=tm, seq=seq, layout=layout),
        grid=(n_tok // tm,),
        in_specs=[row(d), _resident((1, d)), _resident(w_small.shape), _resident(w_ga.shape),
                  _resident(w_gb.shape), tab(), tab(), tab(), _resident((1, LANES))],
        out_specs=out_specs,
        out_shape=out_shape,
        compiler_params=pltpu.CompilerParams(dimension_semantics=("arbitrary",),
                                             vmem_limit_bytes=VMEM_LIMIT_BYTES),
        name="proj",
    )(x2, g, w_small, w_ga, w_gb, *tables, bmisc)


def _cumsum_kernel(m_ref, ccol_ref, *, seq):
    r = lax.broadcasted_iota(jnp.int32, (LANES, LANES), 0)
    c = lax.broadcasted_iota(jnp.int32, (LANES, LANES), 1)
    tri = (r >= c).astype(F32)
    carry = jnp.zeros((1, LANES), F32)
    for b in range(seq // LANES):
        blk = m_ref[0, b * LANES:(b + 1) * LANES, :]
        cs = jnp.dot(tri, blk, precision=HIGHEST, preferred_element_type=F32) + carry
        ccol_ref[0, b * LANES:(b + 1) * LANES, :] = cs
        carry = cs[LANES - 1:LANES, :]


def _cumsum(misc3):
    b, seq, _ = misc3.shape
    assert seq % LANES == 0
    return pl.pallas_call(
        functools.partial(_cumsum_kernel, seq=seq),
        grid=(b,),
        in_specs=[pl.BlockSpec((1, seq, LANES), lambda i: (i, 0, 0))],
        out_specs=pl.BlockSpec((1, seq, LANES), lambda i: (i, 0, 0)),
        out_shape=jax.ShapeDtypeStruct((b, seq, LANES), F32),
        compiler_params=pltpu.CompilerParams(dimension_semantics=("arbitrary",),
                                             vmem_limit_bytes=VMEM_LIMIT_BYTES),
        name="cumsum",
    )(misc3)


def _softmax_step(s, pv, m_ref, acc_ref, rows):
    m_prev = m_ref[rows]
    m_new = jnp.maximum(m_prev, jnp.max(s, axis=-1, keepdims=True))
    p = jnp.exp2(s - m_new).astype(BF16)
    acc_ref[rows] = jnp.exp2(m_prev - m_new) * acc_ref[rows] + pv(p)
    m_ref[rows] = m_new


def _split3(c):
    hi = c.astype(BF16).astype(F32)
    r = c - hi
    mid = r.astype(BF16).astype(F32)
    lo = (r - mid).astype(BF16).astype(F32)
    return hi, mid, lo


def _softmax_step_t(s, pv, m_ref, acc_ref, idx):
    m_prev = m_ref[idx]
    m_new = jnp.maximum(m_prev, jnp.max(s, axis=0, keepdims=True))
    p = jnp.exp2(s - m_new).astype(BF16)
    acc_ref[idx] = jnp.exp2(m_prev - m_new) * acc_ref[idx] + pv(p)
    m_ref[idx] = m_new


def _fox_kernel(q_ref, k_ref, v_ref, ccol_ref, o_ref, kx_s, vt_s, qx_s, m_s, acc_s, *, tq, tk, seq):
    i = pl.program_id(1)
    lane = lax.broadcasted_iota(jnp.int32, (tq, LANES), 1)
    left = lane < HEAD_DIM
    b0 = BIAS_LANE0

    @pl.when(i == 0)
    def _build_keys():
        def chunk(r, carry):
            rows = pl.ds(pl.multiple_of(r * tq, tq), tq)
            for h in range(FOX_HEADS):
                cols = slice(h * LANES, (h + 1) * LANES)
                hi, mid, lo = _split3(ccol_ref[0, rows, h:h + 1] * LOG2E)
                e = jnp.where(lane < b0 + 3, 1.0,
                              jnp.where(lane == b0 + 3, -hi,
                                        jnp.where(lane == b0 + 4, -mid, jnp.where(lane == b0 + 5, -lo, 0.0))))
                kx_s[h, rows, :] = jnp.where(left, k_ref[0, rows, cols].astype(F32), e).astype(BF16)
            return carry

        lax.fori_loop(0, seq // tq, chunk, 0)

        def vchunk(c, carry):
            rows = pl.ds(pl.multiple_of(c * tk, tk), tk)
            for h in range(FOX_HEADS):
                cols = slice(h * LANES, (h + 1) * LANES)
                vt_s[c, h] = v_ref[0, rows, cols].astype(F32).T.astype(BF16)
            return carry

        lax.fori_loop(0, seq // tk, vchunk, 0)

    qrows = pl.ds(pl.multiple_of(i * tq, tq), tq)
    n_full = (i * tq) // tk
    q_pos = i * tq + lax.broadcasted_iota(jnp.int32, (tk, tq), 1)
    k_off = lax.broadcasted_iota(jnp.int32, (tk, tq), 0)
    row = lax.broadcasted_iota(jnp.int32, (LANES, tq), 0)
    n_pairs = FOX_HEADS // 2
    for h in range(FOX_HEADS):
        hi, mid, lo = _split3(ccol_ref[0, qrows, h:h + 1] * LOG2E)
        e = jnp.where(lane == b0, hi,
                      jnp.where(lane == b0 + 1, mid,
                                jnp.where(lane == b0 + 2, lo, jnp.where(lane < b0 + 6, 1.0, 0.0))))
        qx_s[h] = jnp.where(left, q_ref[0, :, h * LANES:(h + 1) * LANES].astype(F32), e).astype(BF16)
    m_s[...] = jnp.full(m_s.shape, NEG_INF, F32)
    acc_s[...] = jnp.zeros(acc_s.shape, F32)

    def scores(j, p):
        rows = pl.ds(pl.multiple_of(j * tk, tk), tk)
        return tuple(_nt_dot(kx_s[2 * p + slot, rows, :], qx_s[2 * p + slot]) for slot in range(2))

    def consume(j, p, ss, masked):
        for slot in range(2):
            h = 2 * p + slot
            s = ss[slot]
            if masked:
                s = jnp.where(j * tk + k_off <= q_pos, s, NEG_INF)
            _softmax_step_t(s, lambda pr, h=h: jnp.dot(vt_s[j, h], pr, preferred_element_type=F32), m_s, acc_s, h)

    def body(n, ss):
        nxt = scores((n + 1) // n_pairs, (n + 1) % n_pairs)
        consume(n // n_pairs, n % n_pairs, ss, False)
        return nxt

    ss = lax.fori_loop(0, n_pairs * n_full, body, scores(0, 0))
    for p in range(n_pairs):
        nxt = scores(n_full, p + 1) if p + 1 < n_pairs else None
        consume(n_full, p, ss, True)
        ss = nxt

    for hp in range(n_pairs):
        a0, a1 = acc_s[2 * hp], acc_s[2 * hp + 1]
        ot = jnp.where(row < HEAD_DIM, a0 / a0[HEAD_DIM:HEAD_DIM + 1, :], a1 / a1[0:1, :])
        o_ref[0, :, hp * LANES:(hp + 1) * LANES] = ot.T.astype(BF16)


def _fox(fq, fk, fv, ccol):
    b, seq, w = fq.shape
    tq, tk = FOX_TQ, KV_STEP
    assert seq % tq == 0 and seq % tk == 0 and tk % tq == 0 and w == FOX_HEADS * LANES
    whole = pl.BlockSpec((1, seq, w), lambda bi, i: (bi, 0, 0))
    return pl.pallas_call(
        functools.partial(_fox_kernel, tq=tq, tk=tk, seq=seq),
        grid=(b, seq // tq),
        in_specs=[pl.BlockSpec((1, tq, w), lambda bi, i: (bi, i, 0)), whole, whole,
                  pl.BlockSpec((1, seq, LANES), lambda bi, i: (bi, 0, 0))],
        out_specs=pl.BlockSpec((1, tq, FOX_WIDTH), lambda bi, i: (bi, i, 0)),
        out_shape=jax.ShapeDtypeStruct((b, seq, FOX_WIDTH), BF16),
        scratch_shapes=[pltpu.VMEM((FOX_HEADS, seq, LANES), BF16),
                        pltpu.VMEM((seq // tk, FOX_HEADS, LANES, tk), BF16),
                        pltpu.VMEM((FOX_HEADS, tq, LANES), BF16),
                        pltpu.VMEM((FOX_HEADS, 1, tq), F32), pltpu.VMEM((FOX_HEADS, LANES, tq), F32)],
        compiler_params=pltpu.CompilerParams(dimension_semantics=("arbitrary", "arbitrary"),
                                             vmem_limit_bytes=VMEM_LIMIT_BYTES),
        name="fox",
    )(fq, fk, fv, ccol)


def _compress_kernel(yk_ref, yv_ref, posk_ref, w1k_ref, w2k_ref, posv_ref, w1v_ref, w2v_ref,
                     kc_ref, vc_ref, *, nb):
    half = CMP_STRIDE * HEAD_DIM

    def comp(y_ref, pos_ref, w1_ref, w2_ref):
        y = y_ref[0, 0]
        pos = pos_ref[...]
        a = jnp.dot(y + pos[:, :half], w1_ref[0:half, :], precision=HIGHEST, preferred_element_type=F32)
        bm = jnp.dot(y + pos[:, half:], w1_ref[half:2 * half, :], precision=HIGHEST,
                     preferred_element_type=F32)
        pre = a + pltpu.roll(bm, nb - 1, axis=0)
        return jnp.dot(pre * _sigmoid(pre), w2_ref[...], precision=HIGHEST, preferred_element_type=F32)

    kc_ref[0, 0] = comp(yk_ref, posk_ref, w1k_ref, w2k_ref).astype(BF16)
    vc_ref[0, 0] = comp(yv_ref, posv_ref, w1v_ref, w2v_ref).astype(BF16)


def _compress(kc, vc, b, seq, pos_k, w1_k, w2_k, pos_v, w1_v, w2_v):
    g = NSA_KV_GROUPS
    nb = seq // CMP_STRIDE
    half = CMP_STRIDE * HEAD_DIM
    assert CMP_BLOCK == 2 * CMP_STRIDE and seq % CMP_STRIDE == 0 and nb % 8 == 0

    def regroup(t):
        t = t.reshape(b, nb, CMP_STRIDE, g, HEAD_DIM)
        return t.transpose(0, 3, 1, 2, 4).reshape(b, g, nb, half)

    zeros1 = jnp.zeros((CMP_BLOCK * HEAD_DIM, HEAD_DIM), F32)
    zeros2 = jnp.zeros((HEAD_DIM, HEAD_DIM), F32)
    w1k = jnp.concatenate([w1_k, zeros1], axis=-1)
    w1v = jnp.concatenate([w1_v, zeros1], axis=-1)
    w2k = jnp.concatenate([jnp.concatenate([w2_k, zeros2], axis=-1), jnp.zeros((HEAD_DIM, LANES), F32)], axis=0)
    w2v = jnp.concatenate([jnp.concatenate([w2_v, w2_v], axis=-1), jnp.zeros((HEAD_DIM, LANES), F32)], axis=0)
    yspec = pl.BlockSpec((1, 1, nb, half), lambda bi, gi: (bi, gi, 0, 0))
    ospec = pl.BlockSpec((1, 1, nb, LANES), lambda bi, gi: (bi, gi, 0, 0))
    return pl.pallas_call(
        functools.partial(_compress_kernel, nb=nb),
        grid=(b, g),
        in_specs=[yspec, yspec,
                  _resident((1, 2 * half)), _resident(w1k.shape), _resident(w2k.shape),
                  _resident((1, 2 * half)), _resident(w1v.shape), _resident(w2v.shape)],
        out_specs=[ospec, ospec],
        out_shape=[jax.ShapeDtypeStruct((b, g, nb, LANES), BF16)] * 2,
        compiler_params=pltpu.CompilerParams(dimension_semantics=("arbitrary", "arbitrary"),
                                             vmem_limit_bytes=VMEM_LIMIT_BYTES),
        name="compress",
    )(regroup(kc), regroup(vc), pos_k.reshape(1, 2 * half), w1k, w2k, pos_v.reshape(1, 2 * half), w1v, w2v)


def _overlap_matrix(nb, n_cmp, n_sel):
    ci = np.arange(nb)[:, None] * CMP_STRIDE
    sj = (np.arange(LANES)[None, :] - SEL_LANE0) * SEL_BLOCK
    jj = np.arange(LANES)[None, :] - SEL_LANE0
    ov = (ci < sj + SEL_BLOCK) & (ci + CMP_BLOCK > sj) & (jj >= 0) & (jj < n_sel) & (np.arange(nb)[:, None] < n_cmp)
    return jnp.asarray(ov.astype(np.float32))


def _cmpsel_kernel(q_ref, kc_ref, vc_ref, ov_ref, qs_ref, ocmp_ref, *, tq, n_cmp, nb):
    i = pl.program_id(2)
    lane = lax.broadcasted_iota(jnp.int32, (tq, LANES), 1)
    t_pos = i * tq + lax.broadcasted_iota(jnp.int32, (tq, LANES), 0)
    nidx = lax.broadcasted_iota(jnp.int32, (tq, nb), 1)
    t_pos_n = i * tq + lax.broadcasted_iota(jnp.int32, (tq, nb), 0)
    cmask = (nidx * CMP_STRIDE + (CMP_BLOCK - 1) <= t_pos_n) & (nidx < n_cmp)
    kc = kc_ref[0, 0]
    vc = vc_ref[0, 0]
    psum = jnp.zeros((tq, nb), F32)
    outs, qf = [], []
    for h in range(NSA_HEADS_PER_GROUP):
        q = q_ref[0, :, h * LANES:(h + 1) * LANES]
        qf.append(q.astype(F32))
        s = jnp.where(cmask, _nt_dot(q, kc), NEG_INF)
        m = jnp.max(s, axis=-1, keepdims=True)
        p = jnp.where(cmask, jnp.exp2(s - m), 0.0)
        p = p / jnp.maximum(jnp.sum(p, axis=-1, keepdims=True), 1e-30)
        psum = psum + p
        outs.append(jnp.dot(p.astype(BF16), vc, preferred_element_type=F32))
    for hp in range(NSA_HEADS_PER_GROUP // 2):
        ocmp_ref[0, :, hp * LANES:(hp + 1) * LANES] = jnp.where(lane < HEAD_DIM, outs[2 * hp], outs[2 * hp + 1])

    p_slc = jnp.dot(psum, ov_ref[...], precision=HIGHEST, preferred_element_type=F32)
    j = lane - SEL_LANE0
    cur = t_pos // SEL_BLOCK
    in_range = (j >= 0) & (j < MAX_SEL_BLOCKS)
    forced = (j == 0) | (j == cur) | (j == cur - 1)
    score = jnp.where(j <= cur, p_slc + FORCED_BONUS * forced.astype(F32), NEG_INF)
    score = jnp.where(in_range, score, -3e38)
    lane_f = lane.astype(F32)
    sel = jnp.zeros((tq, LANES), jnp.bool_)
    for _ in range(SEL_TOPN):
        mx = jnp.max(score, axis=-1, keepdims=True)
        first = jnp.min(jnp.where(score == mx, lane_f, 1e4), axis=-1, keepdims=True)
        chosen = lane_f == first
        sel = sel | chosen
        score = jnp.where(chosen, -jnp.inf, score)
    bias = jnp.where(in_range & jnp.logical_not(sel), NEG_INF, 0.0)
    for h in range(NSA_HEADS_PER_GROUP):
        qs_ref[0, :, h * LANES:(h + 1) * LANES] = (qf[h] + bias).astype(BF16)


def _cmpsel(nq3, kcmp, vcmp):
    b, seq, _ = nq3.shape
    g, hg = NSA_KV_GROUPS, NSA_HEADS_PER_GROUP
    tq = CMP_TQ
    nb = kcmp.shape[2]
    n_cmp = (seq - CMP_BLOCK) // CMP_STRIDE + 1
    n_sel = seq // SEL_BLOCK
    assert seq % tq == 0 and n_sel <= MAX_SEL_BLOCKS and n_sel >= SEL_TOPN
    ov = _overlap_matrix(nb, n_cmp, n_sel)
    qspec = pl.BlockSpec((1, tq, hg * LANES), lambda bi, gi, i: (bi, i, gi))
    cspec = pl.BlockSpec((1, 1, nb, LANES), lambda bi, gi, i: (bi, gi, 0, 0))
    return pl.pallas_call(
        functools.partial(_cmpsel_kernel, tq=tq, n_cmp=n_cmp, nb=nb),
        grid=(b, g, seq // tq),
        in_specs=[qspec, cspec, cspec, _resident(ov.shape)],
        out_specs=[qspec, pl.BlockSpec((1, tq, hg * HEAD_DIM), lambda bi, gi, i: (bi, i, gi))],
        out_shape=[jax.ShapeDtypeStruct(nq3.shape, BF16),
                   jax.ShapeDtypeStruct((b, seq, NSA_WIDTH), F32)],
        compiler_params=pltpu.CompilerParams(dimension_semantics=("arbitrary",) * 3,
                                             vmem_limit_bytes=VMEM_LIMIT_BYTES),
        name="cmpsel",
    )(nq3, kcmp, vcmp, ov)


def _nsa_kernel(qs_ref, ks_ref, vs_ref, kw_ref, vw_ref, ocmp_ref, gates_ref, o_ref,
                vse_s, vso_s, vwe_s, vwo_s, m_s, acc_s, *, tq, tk, seq):
    i = pl.program_id(2)
    hg = NSA_HEADS_PER_GROUP
    rows_all = hg * tq
    half = rows_all // 2
    wt = WINDOW // tq + 1

    @pl.when(i == 0)
    def _build_values():
        lane = lax.broadcasted_iota(jnp.int32, (tk, LANES), 1)
        one_e = (lane == HEAD_DIM).astype(F32)
        one_o = (lane == 0).astype(F32)

        def chunk(c, carry):
            rows = pl.ds(pl.multiple_of(c * tk, tk), tk)
            v = vs_ref[0, rows, :].astype(F32)
            vse_s[c] = jnp.where(lane < HEAD_DIM, v, one_e).T.astype(BF16)
            vso_s[c] = jnp.where(lane < HEAD_DIM, one_o, v).T.astype(BF16)
            w = vw_ref[0, rows, :].astype(F32)
            we = jnp.where(lane < HEAD_DIM, w, one_e).T.astype(BF16)
            wo = jnp.where(lane < HEAD_DIM, one_o, w).T.astype(BF16)
            for t in range(tk // tq):
                vwe_s[c * (tk // tq) + t] = we[:, t * tq:(t + 1) * tq]
                vwo_s[c * (tk // tq) + t] = wo[:, t * tq:(t + 1) * tq]
            return carry

        lax.fori_loop(0, seq // tk, chunk, 0)

    order = tuple(range(0, hg, 2)) + tuple(range(1, hg, 2))
    q = jnp.concatenate([qs_ref[0, :, h * LANES:(h + 1) * LANES] for h in order], axis=0)

    n_full = (i * tq) // tk
    q_pos = i * tq + (lax.broadcasted_iota(jnp.int32, (tk, rows_all), 1) & (tq - 1))
    k_off = lax.broadcasted_iota(jnp.int32, (tk, rows_all), 0)
    m_s[...] = jnp.full(m_s.shape, NEG_INF, F32)
    acc_s[...] = jnp.zeros(acc_s.shape, F32)

    def scores(j):
        return _nt_dot(ks_ref[0, pl.ds(pl.multiple_of(j * tk, tk), tk), :], q)

    def consume(j, s, masked):
        if masked:
            s = jnp.where(j * tk + k_off <= q_pos, s, NEG_INF)
        _softmax_step_t(s, lambda p: jnp.concatenate(
            [jnp.dot(vse_s[j], p[:, :half], preferred_element_type=F32),
             jnp.dot(vso_s[j], p[:, half:], preferred_element_type=F32)], axis=1), m_s, acc_s, 0)

    def body(j, s):
        nxt = scores(j + 1)
        consume(j, s, False)
        return nxt

    s_last = lax.fori_loop(0, n_full, body, scores(0))

    st = jnp.maximum(i - (wt - 1), 0)
    s_win = _nt_dot(kw_ref[0, pl.ds(pl.multiple_of(st * tq, tq), wt * tq), :], q)
    consume(n_full, s_last, True)
    a_sel = acc_s[0]
    qw_pos = i * tq + (lax.broadcasted_iota(jnp.int32, (wt * tq, rows_all), 1) & (tq - 1))
    kw_pos = st * tq + lax.broadcasted_iota(jnp.int32, (wt * tq, rows_all), 0)
    s_win = jnp.where((kw_pos <= qw_pos) & (kw_pos > qw_pos - WINDOW), s_win, NEG_INF)
    p = jnp.exp2(s_win - jnp.max(s_win, axis=0, keepdims=True)).astype(BF16)
    a_win = jnp.zeros((LANES, rows_all), F32)
    for t in range(wt):
        pt = p[t * tq:(t + 1) * tq]
        a_win = a_win + jnp.concatenate(
            [jnp.dot(vwe_s[st + t], pt[:, :half], preferred_element_type=F32),
             jnp.dot(vwo_s[st + t], pt[:, half:], preferred_element_type=F32)], axis=1)

    lane = lax.broadcasted_iota(jnp.int32, (tq, LANES), 1)
    left = lane < HEAD_DIM
    top = lax.broadcasted_iota(jnp.int32, (LANES, tq), 0) < HEAD_DIM
    gates = gates_ref[0]
    for hp in range(hg // 2):
        h0, h1 = 2 * hp, 2 * hp + 1

        def pair(a):
            ae = a[:, hp * tq:(hp + 1) * tq]
            ao = a[:, half + hp * tq:half + (hp + 1) * tq]
            return jnp.where(top, ae / jnp.maximum(ae[HEAD_DIM:HEAD_DIM + 1, :], 1e-30),
                             ao / jnp.maximum(ao[0:1, :], 1e-30)).T

        def gate(branch):
            return jnp.where(left, gates[:, 3 * h0 + branch:3 * h0 + branch + 1],
                             gates[:, 3 * h1 + branch:3 * h1 + branch + 1])

        o = gate(0) * ocmp_ref[0, :, hp * LANES:(hp + 1) * LANES] + gate(1) * pair(a_sel) + gate(2) * pair(a_win)
        o_ref[0, :, hp * LANES:(hp + 1) * LANES] = o.astype(BF16)


def _nsa(qs, ksl, vsl, kwn, vwn, ocmp, gates):
    b, seq, _ = qs.shape
    g, hg = NSA_KV_GROUPS, NSA_HEADS_PER_GROUP
    tq, tk = NSA_TQ, KV_STEP
    assert seq % tk == 0 and tk % tq == 0 and WINDOW % tq == 0 and seq >= WINDOW + tq and tq & (tq - 1) == 0
    assert hg % 2 == 0
    kv = pl.BlockSpec((1, seq, LANES), lambda bi, gi, i: (bi, 0, gi))
    ospec = pl.BlockSpec((1, tq, hg * HEAD_DIM), lambda bi, gi, i: (bi, i, gi))
    return pl.pallas_call(
        functools.partial(_nsa_kernel, tq=tq, tk=tk, seq=seq),
        grid=(b, g, seq // tq),
        in_specs=[pl.BlockSpec((1, tq, hg * LANES), lambda bi, gi, i: (bi, i, gi)), kv, kv, kv, kv,
                  ospec, pl.BlockSpec((1, tq, LANES), lambda bi, gi, i: (bi, i, gi))],
        out_specs=ospec,
        out_shape=jax.ShapeDtypeStruct((b, seq, NSA_WIDTH), BF16),
        scratch_shapes=[pltpu.VMEM((seq // tk, LANES, tk), BF16)] * 2
        + [pltpu.VMEM((seq // tq, LANES, tq), BF16)] * 2
        + [pltpu.VMEM((1, 1, hg * tq), F32), pltpu.VMEM((1, LANES, hg * tq), F32)],
        compiler_params=pltpu.CompilerParams(dimension_semantics=("arbitrary",) * 3,
                                             vmem_limit_bytes=VMEM_LIMIT_BYTES),
        name="nsa",
    )(qs, ksl, vsl, kwn, vwn, ocmp, gates)


def _rms(y, g):
    return y * lax.rsqrt(jnp.mean(y * y, axis=-1, keepdims=True) + RMS_EPS) * g


def _tail_kernel(x_ref, fox_ref, nsa_ref, ga_ref, gb_ref, wf_ref, wn_ref, wo_ref, wup_ref, wdn_ref,
                 gmix_ref, gpre_ref, gpost_ref, o_ref, *, d_ff):
    a = jnp.dot(fox_ref[...], wf_ref[...], preferred_element_type=F32)
    b = jnp.dot(nsa_ref[...], wn_ref[...], preferred_element_type=F32)
    mix = ga_ref[...].astype(F32) * a + gb_ref[...].astype(F32) * b
    y = jnp.dot(mix.astype(BF16), wo_ref[...], preferred_element_type=F32)
    x1 = x_ref[...] + _rms(y, gmix_ref[...])
    h = _rms(x1, gpre_ref[...]).astype(BF16)
    acc = jnp.zeros(x1.shape, F32)
    for c in range(d_ff // FF_CHUNK):
        cols = slice(c * FF_CHUNK, (c + 1) * FF_CHUNK)
        u = jnp.maximum(jnp.dot(h, wup_ref[:, cols], preferred_element_type=F32), 0.0)
        acc = acc + jnp.dot((u * u).astype(BF16), wdn_ref[cols, :], preferred_element_type=F32)
    o_ref[...] = x1 + _rms(acc, gpost_ref[...])


def _tail(x2, fox, nsa, sga, sgb, wf, wn, wo, wup, wdn, g_mix_post, g_mlp_pre, g_mlp_post):
    n_tok, d = x2.shape
    d_ff = wup.shape[1]
    tm = TAIL_TM
    assert n_tok % tm == 0 and d_ff % FF_CHUNK == 0

    def row(w):
        return pl.BlockSpec((tm, w), lambda i: (i, 0))

    return pl.pallas_call(
        functools.partial(_tail_kernel, d_ff=d_ff),
        grid=(n_tok // tm,),
        in_specs=[row(d), row(fox.shape[1]), row(nsa.shape[1]), row(d), row(d),
                  _resident(wf.shape), _resident(wn.shape), _resident(wo.shape),
                  _resident(wup.shape), _resident(wdn.shape),
                  _resident((1, d)), _resident((1, d)), _resident((1, d))],
        out_specs=row(d),
        out_shape=jax.ShapeDtypeStruct((n_tok, d), F32),
        compiler_params=pltpu.CompilerParams(dimension_semantics=("arbitrary",),
                                             vmem_limit_bytes=VMEM_LIMIT_BYTES),
        name="tail",
    )(x2, fox, nsa, sga, sgb, wf, wn, wo, wup, wdn, g_mix_post, g_mlp_pre, g_mlp_post)


def _layer(x, norm_mix_pre, norm_mix_post, norm_mlp_pre, norm_mlp_post, w_in, b_forget,
           cmp_pos_k, cmp_w1_k, cmp_w2_k, cmp_pos_v, cmp_w1_v, cmp_w2_v,
           w_fox_out, w_nsa_out, w_o, w_up, w_down):
    b, seq, d = x.shape
    n_tok = b * seq
    x2 = x.reshape(n_tok, d)
    w_small, w_ga, w_gb = _rearrange_w_in(w_in, d)
    bmisc = jnp.concatenate([b_forget.astype(F32), jnp.zeros((LANES - FOX_HEADS,), F32)]).reshape(1, LANES)
    (fq, fk, fv, nq, ksl, vsl, kwn, vwn, kc, vc, misc, gates, sga, sgb) = _proj(
        x2, norm_mix_pre.reshape(1, d), w_small, w_ga, w_gb, _rope_tables(seq), bmisc, seq)

    def seq3(t):
        return t.reshape(b, seq, t.shape[-1])

    fox = _fox(seq3(fq), seq3(fk), seq3(fv), _cumsum(seq3(misc)))

    kcmp, vcmp = _compress(kc, vc, b, seq, cmp_pos_k, cmp_w1_k, cmp_w2_k, cmp_pos_v, cmp_w1_v, cmp_w2_v)
    qs, ocmp = _cmpsel(seq3(nq), kcmp, vcmp)
    nsa = _nsa(qs, seq3(ksl), seq3(vsl), seq3(kwn), seq3(vwn), ocmp, seq3(gates))

    out = _tail(x2, fox.reshape(n_tok, FOX_WIDTH), nsa.reshape(n_tok, NSA_WIDTH), sga, sgb,
                w_fox_out.astype(BF16), w_nsa_out.astype(BF16), w_o.astype(BF16),
                w_up.astype(BF16), w_down.astype(BF16),
                norm_mix_post.reshape(1, d), norm_mlp_pre.reshape(1, d), norm_mlp_post.reshape(1, d))
    return out.reshape(b, seq, d)


def kernel(x, norm_mix_pre, norm_mix_post, norm_mlp_pre, norm_mlp_post, w_in, b_forget, cmp_pos_k, cmp_w1_k, cmp_w2_k, cmp_pos_v, cmp_w1_v, cmp_w2_v, w_fox_out, w_nsa_out, w_o, w_up, w_down):
    params = (norm_mix_pre, norm_mix_post, norm_mlp_pre, norm_mlp_post, w_in, b_forget,
              cmp_pos_k, cmp_w1_k, cmp_w2_k, cmp_pos_v, cmp_w1_v, cmp_w2_v,
              w_fox_out, w_nsa_out, w_o, w_up, w_down)
    for layer in range(w_in.shape[0]):
        x = _layer(x, *[p[layer] for p in params])
    return x
```

```python
import functools

import numpy as np
import jax
import jax.numpy as jnp
from jax import lax
from jax.experimental import pallas as pl
from jax.experimental.pallas import tpu as pltpu

HEAD_DIM = 64
FOX_HEADS = 8
NSA_HEADS = 8
NSA_KV_GROUPS = 2
NSA_HEADS_PER_GROUP = NSA_HEADS // NSA_KV_GROUPS
FOX_WIDTH = FOX_HEADS * HEAD_DIM
NSA_WIDTH = NSA_HEADS * HEAD_DIM
NSA_KV_WIDTH = NSA_KV_GROUPS * HEAD_DIM
ROPE_THETA = 10000.0
CMP_BLOCK = 32
CMP_STRIDE = 16
SEL_BLOCK = 64
SEL_TOPN = 8
WINDOW = 512
RMS_EPS = 1e-6
NEG_INF = -1e30
FORCED_BONUS = 1e4
ATTN_SCALE = HEAD_DIM ** -0.5
LOG2E = 1.4426950408889634
Q_SCALE = ATTN_SCALE * LOG2E

LANES = 128
VMEM_LIMIT_BYTES = 52 * 1024 * 1024
SEL_LANE0 = HEAD_DIM
MAX_SEL_BLOCKS = 32

F32 = jnp.float32
BF16 = jnp.bfloat16
HIGHEST = lax.Precision.HIGHEST

PROJ_TM = 512
FOX_TQ = 256
CMP_TQ = 256
NSA_TQ = 256
TAIL_TM = 256
FF_CHUNK = 1024
KV_STEP = 512
BIAS_LANE0 = HEAD_DIM


def _nt_dot(a, b):
    return lax.dot_general(a, b, (((1,), (1,)), ((), ())), preferred_element_type=F32)


def _sigmoid(t):
    return 1.0 / (1.0 + jnp.exp(-t))


def _resident(shape):
    nd = len(shape)
    return pl.BlockSpec(shape, lambda *_: (0,) * nd, pipeline_mode=pl.Buffered(1))


def _proj_layout():
    widths = [('fq', FOX_WIDTH), ('fk', FOX_WIDTH), ('fv', FOX_WIDTH),
              ('nq', NSA_HEADS * LANES),
              ('ksl', NSA_KV_GROUPS * LANES), ('vsl', NSA_KV_GROUPS * LANES),
              ('kwn', NSA_KV_GROUPS * LANES), ('vwn', NSA_KV_GROUPS * LANES),
              ('kc', NSA_KV_WIDTH), ('vc', NSA_KV_WIDTH),
              ('misc', LANES), ('gates', NSA_KV_GROUPS * LANES)]
    out, c = {}, 0
    for name, w in widths:
        out[name] = (c, w)
        c += w
    return out, c


def _rearrange_w_in(w_in, d_model):
    splits = (FOX_WIDTH, FOX_WIDTH, FOX_WIDTH, FOX_HEADS,
              NSA_WIDTH, NSA_KV_WIDTH, NSA_KV_WIDTH, NSA_KV_WIDTH, NSA_KV_WIDTH, NSA_KV_WIDTH, NSA_KV_WIDTH,
              3 * NSA_HEADS, d_model, d_model)
    pts = [int(p) for p in np.cumsum(splits)[:-1]]
    fq, fk, fv, ff, nq, kc, vc, ksl, vsl, kwn, vwn, ng, ga, gb = jnp.split(w_in, pts, axis=-1)
    d = w_in.shape[0]

    def pad_heads(w, n):
        w = w.reshape(d, n, HEAD_DIM)
        return jnp.concatenate([w, jnp.zeros_like(w)], axis=-1).reshape(d, n * LANES)

    def dup_heads(w, n):
        w = w.reshape(d, n, HEAD_DIM)
        return jnp.concatenate([w, w], axis=-1).reshape(d, n * LANES)

    misc = jnp.concatenate([ff, jnp.zeros((d, LANES - FOX_HEADS), w_in.dtype)], axis=-1)
    per_group = 3 * NSA_HEADS_PER_GROUP
    ngg = ng.reshape(d, NSA_KV_GROUPS, per_group)
    gates = jnp.concatenate([ngg, jnp.zeros((d, NSA_KV_GROUPS, LANES - per_group), w_in.dtype)],
                            axis=-1).reshape(d, NSA_KV_GROUPS * LANES)
    w_small = jnp.concatenate(
        [fq, fk, fv, pad_heads(nq, NSA_HEADS),
         pad_heads(ksl, NSA_KV_GROUPS), dup_heads(vsl, NSA_KV_GROUPS),
         pad_heads(kwn, NSA_KV_GROUPS), dup_heads(vwn, NSA_KV_GROUPS),
         kc, vc, misc, gates], axis=-1)
    return w_small.astype(BF16), ga.astype(BF16), gb.astype(BF16)


def _rope_tables(seq):
    half = HEAD_DIM // 2
    inv = jnp.power(jnp.float32(ROPE_THETA), -jnp.arange(0, HEAD_DIM, 2, dtype=F32) / HEAD_DIM)
    ang = jnp.arange(seq, dtype=F32)[:, None] * inv[None, :]
    cos, sin = jnp.cos(ang), jnp.sin(ang)
    zero = jnp.zeros_like(sin)
    cos_t = jnp.concatenate([cos, cos, cos, cos], axis=-1)
    s_lo = jnp.concatenate([-sin, zero, -sin, zero], axis=-1)
    s_hi = jnp.concatenate([zero, sin, zero, sin], axis=-1)
    assert cos_t.shape == (seq, 4 * half)
    return cos_t, s_lo, s_hi


def _proj_kernel(x_ref, g_ref, w_ref, wga_ref, wgb_ref, cos_ref, slo_ref, shi_ref, bmisc_ref,
                 fq_ref, fk_ref, fv_ref, nq_ref, ksl_ref, vsl_ref, kwn_ref, vwn_ref,
                 kc_ref, vc_ref, misc_ref, gates_ref, ga_ref, gb_ref, *, tm, seq, layout):
    x = x_ref[...]
    h = x * lax.rsqrt(jnp.mean(x * x, axis=-1, keepdims=True) + RMS_EPS) * g_ref[...]
    h = h.astype(BF16)
    cos, slo, shi = cos_ref[...], slo_ref[...], shi_ref[...]
    half = HEAD_DIM // 2

    def mm(name):
        c0, n = layout[name]
        return jnp.dot(h, w_ref[:, c0:c0 + n], preferred_element_type=F32)

    def rope_block(t):
        return t * cos + pltpu.roll(t, LANES - half, axis=1) * slo + pltpu.roll(t, half, axis=1) * shi

    def rope(t):
        n = t.shape[1] // LANES
        return [rope_block(t[:, b * LANES:(b + 1) * LANES]) for b in range(n)]

    lane = lax.broadcasted_iota(jnp.int32, (tm, LANES), 1)
    left = lane < HEAD_DIM

    def split_heads(t, o_ref, scale):
        for p in range(FOX_HEADS // 2):
            blk = t[:, p * LANES:(p + 1) * LANES] * scale
            o_ref[:, (2 * p) * LANES:(2 * p + 1) * LANES] = jnp.where(left, blk, 0.0).astype(BF16)
            o_ref[:, (2 * p + 1) * LANES:(2 * p + 2) * LANES] = jnp.where(
                left, pltpu.roll(blk, HEAD_DIM, axis=1), 0.0).astype(BF16)

    split_heads(mm('fq'), fq_ref, Q_SCALE)
    split_heads(mm('fk'), fk_ref, 1.0)
    fv = mm('fv')
    for p in range(FOX_HEADS // 2):
        blk = fv[:, p * LANES:(p + 1) * LANES]
        fv_ref[:, (2 * p) * LANES:(2 * p + 1) * LANES] = jnp.where(
            left, blk, (lane == HEAD_DIM).astype(F32)).astype(BF16)
        fv_ref[:, (2 * p + 1) * LANES:(2 * p + 2) * LANES] = jnp.where(
            left, (lane == 0).astype(F32), blk).astype(BF16)

    for b, blk in enumerate(rope(mm('nq'))):
        nq_ref[:, b * LANES:(b + 1) * LANES] = (blk * Q_SCALE).astype(BF16)

    pos = (pl.program_id(0) % (seq // tm)) * tm + lax.broadcasted_iota(jnp.int32, (tm, LANES), 0)
    onehot = (lane - SEL_LANE0) == (pos // SEL_BLOCK)
    for b, blk in enumerate(rope(mm('ksl'))):
        ksl_ref[:, b * LANES:(b + 1) * LANES] = jnp.where(onehot, 1.0, blk).astype(BF16)
    vsl_ref[...] = mm('vsl').astype(BF16)
    for b, blk in enumerate(rope(mm('kwn'))):
        kwn_ref[:, b * LANES:(b + 1) * LANES] = blk.astype(BF16)
    vwn_ref[...] = mm('vwn').astype(BF16)

    kc_ref[...] = rope(mm('kc'))[0]
    vc_ref[...] = mm('vc')

    t = mm('misc') + bmisc_ref[...]
    misc_ref[...] = -(jnp.maximum(-t, 0.0) + jnp.log(1.0 + jnp.exp(-jnp.abs(t))))
    gates_ref[...] = _sigmoid(mm('gates'))

    ga_ref[...] = _sigmoid(jnp.dot(h, wga_ref[...], preferred_element_type=F32)).astype(BF16)
    gb_ref[...] = _sigmoid(jnp.dot(h, wgb_ref[...], preferred_element_type=F32)).astype(BF16)


def _proj(x2, g, w_small, w_ga, w_gb, tables, bmisc, seq):
    n_tok, d = x2.shape
    tm = PROJ_TM
    layout, n_cols = _proj_layout()
    assert n_tok % tm == 0 and seq % tm == 0 and w_small.shape == (d, n_cols)
    tiles_per_seq = seq // tm

    def row(w):
        return pl.BlockSpec((tm, w), lambda i: (i, 0))

    def tab():
        return pl.BlockSpec((tm, LANES), lambda i: (i % tiles_per_seq, 0))

    outs = [('fq', BF16), ('fk', BF16), ('fv', BF16), ('nq', BF16), ('ksl', BF16), ('vsl', BF16),
            ('kwn', BF16), ('vwn', BF16), ('kc', F32), ('vc', F32), ('misc', F32), ('gates', F32)]
    widths = {n: layout[n][1] for n, _ in outs}
    widths.update(fq=FOX_HEADS * LANES, fk=FOX_HEADS * LANES, fv=FOX_HEADS * LANES)
    out_shape = [jax.ShapeDtypeStruct((n_tok, widths[n]), dt) for n, dt in outs]
    out_specs = [row(widths[n]) for n, _ in outs]
    out_shape += [jax.ShapeDtypeStruct((n_tok, d), BF16)] * 2
    out_specs += [row(d), row(d)]
    return pl.pallas_call(
        functools.partial(_proj_kernel, tm=tm, seq=seq, layout=layout),
        grid=(n_tok // tm,),
        in_specs=[row(d), _resident((1, d)), _resident(w_small.shape), _resident(w_ga.shape),
                  _resident(w_gb.shape), tab(), tab(), tab(), _resident((1, LANES))],
        out_specs=out_specs,
        out_shape=out_shape,
        compiler_params=pltpu.CompilerParams(dimension_semantics=("arbitrary",),
                                             vmem_limit_bytes=VMEM_LIMIT_BYTES),
        name="proj",
    )(x2, g, w_small, w_ga, w_gb, *tables, bmisc)


def _cumsum_kernel(m_ref, ccol_ref, *, seq):
    r = lax.broadcasted_iota(jnp.int32, (LANES, LANES), 0)
    c = lax.broadcasted_iota(jnp.int32, (LANES, LANES), 1)
    tri = (r >= c).astype(F32)
    carry = jnp.zeros((1, LANES), F32)
    for b in range(seq // LANES):
        blk = m_ref[0, b * LANES:(b + 1) * LANES, :]
        cs = jnp.dot(tri, blk, precision=HIGHEST, preferred_element_type=F32) + carry
        ccol_ref[0, b * LANES:(b + 1) * LANES, :] = cs
        carry = cs[LANES - 1:LANES, :]


def _cumsum(misc3):
    b, seq, _ = misc3.shape
    assert seq % LANES == 0
    return pl.pallas_call(
        functools.partial(_cumsum_kernel, seq=seq),
        grid=(b,),
        in_specs=[pl.BlockSpec((1, seq, LANES), lambda i: (i, 0, 0))],
        out_specs=pl.BlockSpec((1, seq, LANES), lambda i: (i, 0, 0)),
        out_shape=jax.ShapeDtypeStruct((b, seq, LANES), F32),
        compiler_params=pltpu.CompilerParams(dimension_semantics=("arbitrary",),
                                             vmem_limit_bytes=VMEM_LIMIT_BYTES),
        name="cumsum",
    )(misc3)


def _softmax_step(s, pv, m_ref, acc_ref, rows):
    m_prev = m_ref[rows]
    m_new = jnp.maximum(m_prev, jnp.max(s, axis=-1, keepdims=True))
    p = jnp.exp2(s - m_new).astype(BF16)
    acc_ref[rows] = jnp.exp2(m_prev - m_new) * acc_ref[rows] + pv(p)
    m_ref[rows] = m_new


def _split3(c):
    hi = c.astype(BF16).astype(F32)
    r = c - hi
    mid = r.astype(BF16).astype(F32)
    lo = (r - mid).astype(BF16).astype(F32)
    return hi, mid, lo


def _softmax_step_t(s, pv, m_ref, acc_ref, idx):
    m_prev = m_ref[idx]
    m_new = jnp.maximum(m_prev, jnp.max(s, axis=0, keepdims=True))
    p = jnp.exp2(s - m_new).astype(BF16)
    acc_ref[idx] = jnp.exp2(m_prev - m_new) * acc_ref[idx] + pv(p)
    m_ref[idx] = m_new


def _fox_kernel(q_ref, k_ref, v_ref, ccol_ref, o_ref, kx_s, vt_s, qx_s, m_s, acc_s, *, tq, tk, seq):
    i = pl.program_id(1)
    lane = lax.broadcasted_iota(jnp.int32, (tq, LANES), 1)
    left = lane < HEAD_DIM
    b0 = BIAS_LANE0

    @pl.when(i == 0)
    def _build_keys():
        def chunk(r, carry):
            rows = pl.ds(pl.multiple_of(r * tq, tq), tq)
            for h in range(FOX_HEADS):
                cols = slice(h * LANES, (h + 1) * LANES)
                hi, mid, lo = _split3(ccol_ref[0, rows, h:h + 1] * LOG2E)
                e = jnp.where(lane < b0 + 3, 1.0,
                              jnp.where(lane == b0 + 3, -hi,
                                        jnp.where(lane == b0 + 4, -mid, jnp.where(lane == b0 + 5, -lo, 0.0))))
                kx_s[h, rows, :] = jnp.where(left, k_ref[0, rows, cols].astype(F32), e).astype(BF16)
            return carry

        lax.fori_loop(0, seq // tq, chunk, 0)

        def vchunk(c, carry):
            rows = pl.ds(pl.multiple_of(c * tk, tk), tk)
            for h in range(FOX_HEADS):
                cols = slice(h * LANES, (h + 1) * LANES)
                vt_s[c, h] = v_ref[0, rows, cols].astype(F32).T.astype(BF16)
            return carry

        lax.fori_loop(0, seq // tk, vchunk, 0)

    qrows = pl.ds(pl.multiple_of(i * tq, tq), tq)
    n_full = (i * tq) // tk
    q_pos = i * tq + lax.broadcasted_iota(jnp.int32, (tk, tq), 1)
    k_off = lax.broadcasted_iota(jnp.int32, (tk, tq), 0)
    row = lax.broadcasted_iota(jnp.int32, (LANES, tq), 0)
    n_pairs = FOX_HEADS // 2
    for h in range(FOX_HEADS):
        hi, mid, lo = _split3(ccol_ref[0, qrows, h:h + 1] * LOG2E)
        e = jnp.where(lane == b0, hi,
                      jnp.where(lane == b0 + 1, mid,
                                jnp.where(lane == b0 + 2, lo, jnp.where(lane < b0 + 6, 1.0, 0.0))))
        qx_s[h] = jnp.where(left, q_ref[0, :, h * LANES:(h + 1) * LANES].astype(F32), e).astype(BF16)
    m_s[...] = jnp.full(m_s.shape, NEG_INF, F32)
    acc_s[...] = jnp.zeros(acc_s.shape, F32)

    def scores(j, p):
        rows = pl.ds(pl.multiple_of(j * tk, tk), tk)
        return tuple(_nt_dot(kx_s[2 * p + slot, rows, :], qx_s[2 * p + slot]) for slot in range(2))

    def consume(j, p, ss, masked):
        for slot in range(2):
            h = 2 * p + slot
            s = ss[slot]
            if masked:
                s = jnp.where(j * tk + k_off <= q_pos, s, NEG_INF)
            _softmax_step_t(s, lambda pr, h=h: jnp.dot(vt_s[j, h], pr, preferred_element_type=F32), m_s, acc_s, h)

    def body(n, ss):
        nxt = scores((n + 1) // n_pairs, (n + 1) % n_pairs)
        consume(n // n_pairs, n % n_pairs, ss, False)
        return nxt

    ss = lax.fori_loop(0, n_pairs * n_full, body, scores(0, 0))
    for p in range(n_pairs):
        nxt = scores(n_full, p + 1) if p + 1 < n_pairs else None
        consume(n_full, p, ss, True)
        ss = nxt

    for hp in range(n_pairs):
        a0, a1 = acc_s[2 * hp], acc_s[2 * hp + 1]
        ot = jnp.where(row < HEAD_DIM, a0 / a0[HEAD_DIM:HEAD_DIM + 1, :], a1 / a1[0:1, :])
        o_ref[0, :, hp * LANES:(hp + 1) * LANES] = ot.T.astype(BF16)


def _fox(fq, fk, fv, ccol):
    b, seq, w = fq.shape
    tq, tk = FOX_TQ, KV_STEP
    assert seq % tq == 0 and seq % tk == 0 and tk % tq == 0 and w == FOX_HEADS * LANES
    whole = pl.BlockSpec((1, seq, w), lambda bi, i: (bi, 0, 0))
    return pl.pallas_call(
        functools.partial(_fox_kernel, tq=tq, tk=tk, seq=seq),
        grid=(b, seq // tq),
        in_specs=[pl.BlockSpec((1, tq, w), lambda bi, i: (bi, i, 0)), whole, whole,
                  pl.BlockSpec((1, seq, LANES), lambda bi, i: (bi, 0, 0))],
        out_specs=pl.BlockSpec((1, tq, FOX_WIDTH), lambda bi, i: (bi, i, 0)),
        out_shape=jax.ShapeDtypeStruct((b, seq, FOX_WIDTH), BF16),
        scratch_shapes=[pltpu.VMEM((FOX_HEADS, seq, LANES), BF16),
                        pltpu.VMEM((seq // tk, FOX_HEADS, LANES, tk), BF16),
                        pltpu.VMEM((FOX_HEADS, tq, LANES), BF16),
                        pltpu.VMEM((FOX_HEADS, 1, tq), F32), pltpu.VMEM((FOX_HEADS, LANES, tq), F32)],
        compiler_params=pltpu.CompilerParams(dimension_semantics=("arbitrary", "arbitrary"),
                                             vmem_limit_bytes=VMEM_LIMIT_BYTES),
        name="fox",
    )(fq, fk, fv, ccol)


def _compress_kernel(yk_ref, yv_ref, posk_ref, w1k_ref, w2k_ref, posv_ref, w1v_ref, w2v_ref,
                     kc_ref, vc_ref, *, nb):
    half = CMP_STRIDE * HEAD_DIM

    lane = lax.broadcasted_iota(jnp.int32, (nb, LANES), 1)
    left = lane < HEAD_DIM

    def regroup(x_ref):
        y = [[], []]
        for j in range(CMP_STRIDE // 2):
            xe = x_ref[0, pl.ds(2 * j, nb, stride=CMP_STRIDE), :]
            xo = x_ref[0, pl.ds(2 * j + 1, nb, stride=CMP_STRIDE), :]
            y[0].append(jnp.where(left, xe, pltpu.roll(xo, HEAD_DIM, axis=1)))
            y[1].append(jnp.where(left, pltpu.roll(xe, HEAD_DIM, axis=1), xo))
        return [jnp.concatenate(t, axis=1) for t in y]

    def comp(y, pos_ref, w1_ref, w2_ref):
        pos = pos_ref[...]
        a = jnp.dot(y + pos[:, :half], w1_ref[0:half, :], precision=HIGHEST, preferred_element_type=F32)
        bm = jnp.dot(y + pos[:, half:], w1_ref[half:2 * half, :], precision=HIGHEST,
                     preferred_element_type=F32)
        pre = a + pltpu.roll(bm, nb - 1, axis=0)
        return jnp.dot(pre * _sigmoid(pre), w2_ref[...], precision=HIGHEST, preferred_element_type=F32)

    for g, (yk, yv) in enumerate(zip(regroup(yk_ref), regroup(yv_ref))):
        kc_ref[0, g] = comp(yk, posk_ref, w1k_ref, w2k_ref).astype(BF16)
        vc_ref[0, g] = comp(yv, posv_ref, w1v_ref, w2v_ref).astype(BF16)


def _compress(kc, vc, b, seq, pos_k, w1_k, w2_k, pos_v, w1_v, w2_v):
    g = NSA_KV_GROUPS
    nb = seq // CMP_STRIDE
    half = CMP_STRIDE * HEAD_DIM
    assert CMP_BLOCK == 2 * CMP_STRIDE and seq % CMP_STRIDE == 0 and nb % 8 == 0
    assert g == 2 and kc.shape[-1] == LANES

    zeros1 = jnp.zeros((CMP_BLOCK * HEAD_DIM, HEAD_DIM), F32)
    zeros2 = jnp.zeros((HEAD_DIM, HEAD_DIM), F32)
    w1k = jnp.concatenate([w1_k, zeros1], axis=-1)
    w1v = jnp.concatenate([w1_v, zeros1], axis=-1)
    w2k = jnp.concatenate([jnp.concatenate([w2_k, zeros2], axis=-1), jnp.zeros((HEAD_DIM, LANES), F32)], axis=0)
    w2v = jnp.concatenate([jnp.concatenate([w2_v, w2_v], axis=-1), jnp.zeros((HEAD_DIM, LANES), F32)], axis=0)
    yspec = pl.BlockSpec((1, seq, LANES), lambda bi: (bi, 0, 0))
    ospec = pl.BlockSpec((1, g, nb, LANES), lambda bi: (bi, 0, 0, 0))
    return pl.pallas_call(
        functools.partial(_compress_kernel, nb=nb),
        grid=(b,),
        in_specs=[yspec, yspec,
                  _resident((1, 2 * half)), _resident(w1k.shape), _resident(w2k.shape),
                  _resident((1, 2 * half)), _resident(w1v.shape), _resident(w2v.shape)],
        out_specs=[ospec, ospec],
        out_shape=[jax.ShapeDtypeStruct((b, g, nb, LANES), BF16)] * 2,
        compiler_params=pltpu.CompilerParams(dimension_semantics=("arbitrary",),
                                             vmem_limit_bytes=VMEM_LIMIT_BYTES),
        name="compress",
    )(kc.reshape(b, seq, LANES), vc.reshape(b, seq, LANES), pos_k.reshape(1, 2 * half), w1k, w2k,
      pos_v.reshape(1, 2 * half), w1v, w2v)


def _overlap_matrix(nb, n_cmp, n_sel):
    ci = np.arange(nb)[:, None] * CMP_STRIDE
    sj = (np.arange(LANES)[None, :] - SEL_LANE0) * SEL_BLOCK
    jj = np.arange(LANES)[None, :] - SEL_LANE0
    ov = (ci < sj + SEL_BLOCK) & (ci + CMP_BLOCK > sj) & (jj >= 0) & (jj < n_sel) & (np.arange(nb)[:, None] < n_cmp)
    return jnp.asarray(ov.astype(np.float32))


def _cmpsel_kernel(q_ref, kc_ref, vc_ref, ovt_ref, qs_ref, ocmp_ref, vct_s, *, tq, n_cmp, nb):
    i = pl.program_id(2)

    @pl.when(i == 0)
    def _transpose_values():
        vct_s[...] = vc_ref[0, 0].astype(F32).T.astype(BF16)

    n_idx = lax.broadcasted_iota(jnp.int32, (nb, tq), 0)
    t_pos = i * tq + lax.broadcasted_iota(jnp.int32, (nb, tq), 1)
    cmask = (n_idx * CMP_STRIDE + (CMP_BLOCK - 1) <= t_pos) & (n_idx < n_cmp)
    top = lax.broadcasted_iota(jnp.int32, (LANES, tq), 0) < HEAD_DIM
    kc = kc_ref[0, 0]
    vct = vct_s[...]
    psum = jnp.zeros((nb, tq), F32)
    outs = []
    for h in range(NSA_HEADS_PER_GROUP):
        s = jnp.where(cmask, _nt_dot(kc, q_ref[0, :, h * LANES:(h + 1) * LANES]), NEG_INF)
        m = jnp.max(s, axis=0, keepdims=True)
        p = jnp.where(cmask, jnp.exp2(s - m), 0.0)
        p = p / jnp.maximum(jnp.sum(p, axis=0, keepdims=True), 1e-30)
        psum = psum + p
        outs.append(jnp.dot(vct, p.astype(BF16), preferred_element_type=F32))
    for hp in range(NSA_HEADS_PER_GROUP // 2):
        ocmp_ref[0, :, hp * LANES:(hp + 1) * LANES] = jnp.where(top, outs[2 * hp], outs[2 * hp + 1]).T

    p_slc = jnp.dot(ovt_ref[...], psum, precision=HIGHEST, preferred_element_type=F32)
    sc = p_slc[SEL_LANE0:SEL_LANE0 + MAX_SEL_BLOCKS, :]
    j = lax.broadcasted_iota(jnp.int32, (MAX_SEL_BLOCKS, tq), 0)
    cur = (i * tq + lax.broadcasted_iota(jnp.int32, (MAX_SEL_BLOCKS, tq), 1)) // SEL_BLOCK
    forced = (j == 0) | (j == cur) | (j == cur - 1)
    score = jnp.where(j <= cur, sc + FORCED_BONUS * forced.astype(F32), NEG_INF)
    j_f = j.astype(F32)
    sel = jnp.zeros((MAX_SEL_BLOCKS, tq), jnp.bool_)
    for _ in range(SEL_TOPN):
        mx = jnp.max(score, axis=0, keepdims=True)
        first = jnp.min(jnp.where(score == mx, j_f, 1e4), axis=0, keepdims=True)
        chosen = j_f == first
        sel = sel | chosen
        score = jnp.where(chosen, -jnp.inf, score)
    bias_t = jnp.concatenate([jnp.zeros((SEL_LANE0, tq), F32), jnp.where(sel, 0.0, NEG_INF),
                              jnp.zeros((LANES - SEL_LANE0 - MAX_SEL_BLOCKS, tq), F32)], axis=0)
    bias = bias_t.T
    for h in range(NSA_HEADS_PER_GROUP):
        cols = slice(h * LANES, (h + 1) * LANES)
        qs_ref[0, :, cols] = (q_ref[0, :, cols].astype(F32) + bias).astype(BF16)


def _cmpsel(nq3, kcmp, vcmp):
    b, seq, _ = nq3.shape
    g, hg = NSA_KV_GROUPS, NSA_HEADS_PER_GROUP
    tq = CMP_TQ
    nb = kcmp.shape[2]
    n_cmp = (seq - CMP_BLOCK) // CMP_STRIDE + 1
    n_sel = seq // SEL_BLOCK
    assert seq % tq == 0 and n_sel <= MAX_SEL_BLOCKS and n_sel >= SEL_TOPN
    ov = _overlap_matrix(nb, n_cmp, n_sel).T
    qspec = pl.BlockSpec((1, tq, hg * LANES), lambda bi, gi, i: (bi, i, gi))
    cspec = pl.BlockSpec((1, 1, nb, LANES), lambda bi, gi, i: (bi, gi, 0, 0))
    return pl.pallas_call(
        functools.partial(_cmpsel_kernel, tq=tq, n_cmp=n_cmp, nb=nb),
        grid=(b, g, seq // tq),
        in_specs=[qspec, cspec, cspec, _resident(ov.shape)],
        out_specs=[qspec, pl.BlockSpec((1, tq, hg * HEAD_DIM), lambda bi, gi, i: (bi, i, gi))],
        out_shape=[jax.ShapeDtypeStruct(nq3.shape, BF16),
                   jax.ShapeDtypeStruct((b, seq, NSA_WIDTH), F32)],
        scratch_shapes=[pltpu.VMEM((LANES, nb), BF16)],
        compiler_params=pltpu.CompilerParams(dimension_semantics=("arbitrary",) * 3,
                                             vmem_limit_bytes=VMEM_LIMIT_BYTES),
        name="cmpsel",
    )(nq3, kcmp, vcmp, ov)


def _nsa_kernel(qs_ref, ks_ref, vs_ref, kw_ref, vw_ref, ocmp_ref, gates_ref, o_ref,
                vse_s, vso_s, vwe_s, vwo_s, m_s, acc_s, *, tq, tk, seq):
    i = pl.program_id(2)
    hg = NSA_HEADS_PER_GROUP
    rows_all = hg * tq
    half = rows_all // 2
    wt = WINDOW // tq + 1

    @pl.when(i == 0)
    def _build_values():
        lane = lax.broadcasted_iota(jnp.int32, (tk, LANES), 1)
        one_e = (lane == HEAD_DIM).astype(F32)
        one_o = (lane == 0).astype(F32)

        def chunk(c, carry):
            rows = pl.ds(pl.multiple_of(c * tk, tk), tk)
            v = vs_ref[0, rows, :].astype(F32)
            vse_s[c] = jnp.where(lane < HEAD_DIM, v, one_e).T.astype(BF16)
            vso_s[c] = jnp.where(lane < HEAD_DIM, one_o, v).T.astype(BF16)
            w = vw_ref[0, rows, :].astype(F32)
            we = jnp.where(lane < HEAD_DIM, w, one_e).T.astype(BF16)
            wo = jnp.where(lane < HEAD_DIM, one_o, w).T.astype(BF16)
            for t in range(tk // tq):
                vwe_s[c * (tk // tq) + t] = we[:, t * tq:(t + 1) * tq]
                vwo_s[c * (tk // tq) + t] = wo[:, t * tq:(t + 1) * tq]
            return carry

        lax.fori_loop(0, seq // tk, chunk, 0)

    order = tuple(range(0, hg, 2)) + tuple(range(1, hg, 2))
    q = jnp.concatenate([qs_ref[0, :, h * LANES:(h + 1) * LANES] for h in order], axis=0)

    n_full = (i * tq) // tk
    q_pos = i * tq + (lax.broadcasted_iota(jnp.int32, (tk, rows_all), 1) & (tq - 1))
    k_off = lax.broadcasted_iota(jnp.int32, (tk, rows_all), 0)
    m_s[...] = jnp.full(m_s.shape, NEG_INF, F32)
    acc_s[...] = jnp.zeros(acc_s.shape, F32)

    def scores(j):
        return _nt_dot(ks_ref[0, pl.ds(pl.multiple_of(j * tk, tk), tk), :], q)

    def consume(j, s, masked):
        if masked:
            s = jnp.where(j * tk + k_off <= q_pos, s, NEG_INF)
        _softmax_step_t(s, lambda p: jnp.concatenate(
            [jnp.dot(vse_s[j], p[:, :half], preferred_element_type=F32),
             jnp.dot(vso_s[j], p[:, half:], preferred_element_type=F32)], axis=1), m_s, acc_s, 0)

    def body(j, s):
        nxt = scores(j + 1)
        consume(j, s, False)
        return nxt

    s_last = lax.fori_loop(0, n_full, body, scores(0))

    st = jnp.maximum(i - (wt - 1), 0)
    s_win = _nt_dot(kw_ref[0, pl.ds(pl.multiple_of(st * tq, tq), wt * tq), :], q)
    consume(n_full, s_last, True)
    a_sel = acc_s[0]
    qw_pos = i * tq + (lax.broadcasted_iota(jnp.int32, (wt * tq, rows_all), 1) & (tq - 1))
    kw_pos = st * tq + lax.broadcasted_iota(jnp.int32, (wt * tq, rows_all), 0)
    s_win = jnp.where((kw_pos <= qw_pos) & (kw_pos > qw_pos - WINDOW), s_win, NEG_INF)
    p = jnp.exp2(s_win - jnp.max(s_win, axis=0, keepdims=True)).astype(BF16)
    a_win = jnp.zeros((LANES, rows_all), F32)
    for t in range(wt):
        pt = p[t * tq:(t + 1) * tq]
        a_win = a_win + jnp.concatenate(
            [jnp.dot(vwe_s[st + t], pt[:, :half], preferred_element_type=F32),
             jnp.dot(vwo_s[st + t], pt[:, half:], preferred_element_type=F32)], axis=1)

    lane = lax.broadcasted_iota(jnp.int32, (tq, LANES), 1)
    left = lane < HEAD_DIM
    top = lax.broadcasted_iota(jnp.int32, (LANES, tq), 0) < HEAD_DIM
    gates = gates_ref[0]
    for hp in range(hg // 2):
        h0, h1 = 2 * hp, 2 * hp + 1

        def pair(a):
            ae = a[:, hp * tq:(hp + 1) * tq]
            ao = a[:, half + hp * tq:half + (hp + 1) * tq]
            return jnp.where(top, ae / jnp.maximum(ae[HEAD_DIM:HEAD_DIM + 1, :], 1e-30),
                             ao / jnp.maximum(ao[0:1, :], 1e-30)).T

        def gate(branch):
            return jnp.where(left, gates[:, 3 * h0 + branch:3 * h0 + branch + 1],
                             gates[:, 3 * h1 + branch:3 * h1 + branch + 1])

        o = gate(0) * ocmp_ref[0, :, hp * LANES:(hp + 1) * LANES] + gate(1) * pair(a_sel) + gate(2) * pair(a_win)
        o_ref[0, :, hp * LANES:(hp + 1) * LANES] = o.astype(BF16)


def _nsa(qs, ksl, vsl, kwn, vwn, ocmp, gates):
    b, seq, _ = qs.shape
    g, hg = NSA_KV_GROUPS, NSA_HEADS_PER_GROUP
    tq, tk = NSA_TQ, KV_STEP
    assert seq % tk == 0 and tk % tq == 0 and WINDOW % tq == 0 and seq >= WINDOW + tq and tq & (tq - 1) == 0
    assert hg % 2 == 0
    kv = pl.BlockSpec((1, seq, LANES), lambda bi, gi, i: (bi, 0, gi))
    ospec = pl.BlockSpec((1, tq, hg * HEAD_DIM), lambda bi, gi, i: (bi, i, gi))
    return pl.pallas_call(
        functools.partial(_nsa_kernel, tq=tq, tk=tk, seq=seq),
        grid=(b, g, seq // tq),
        in_specs=[pl.BlockSpec((1, tq, hg * LANES), lambda bi, gi, i: (bi, i, gi)), kv, kv, kv, kv,
                  ospec, pl.BlockSpec((1, tq, LANES), lambda bi, gi, i: (bi, i, gi))],
        out_specs=ospec,
        out_shape=jax.ShapeDtypeStruct((b, seq, NSA_WIDTH), BF16),
        scratch_shapes=[pltpu.VMEM((seq // tk, LANES, tk), BF16)] * 2
        + [pltpu.VMEM((seq // tq, LANES, tq), BF16)] * 2
        + [pltpu.VMEM((1, 1, hg * tq), F32), pltpu.VMEM((1, LANES, hg * tq), F32)],
        compiler_params=pltpu.CompilerParams(dimension_semantics=("arbitrary",) * 3,
                                             vmem_limit_bytes=VMEM_LIMIT_BYTES),
        name="nsa",
    )(qs, ksl, vsl, kwn, vwn, ocmp, gates)


def _rms(y, g):
    return y * lax.rsqrt(jnp.mean(y * y, axis=-1, keepdims=True) + RMS_EPS) * g


def _tail_kernel(x_ref, fox_ref, nsa_ref, ga_ref, gb_ref, wf_ref, wn_ref, wo_ref, wup_ref, wdn_ref,
                 gmix_ref, gpre_ref, gpost_ref, o_ref, *, d_ff):
    a = jnp.dot(fox_ref[...], wf_ref[...], preferred_element_type=F32)
    b = jnp.dot(nsa_ref[...], wn_ref[...], preferred_element_type=F32)
    mix = ga_ref[...].astype(F32) * a + gb_ref[...].astype(F32) * b
    y = jnp.dot(mix.astype(BF16), wo_ref[...], preferred_element_type=F32)
    x1 = x_ref[...] + _rms(y, gmix_ref[...])
    h = _rms(x1, gpre_ref[...]).astype(BF16)
    acc = jnp.zeros(x1.shape, F32)
    for c in range(d_ff // FF_CHUNK):
        cols = slice(c * FF_CHUNK, (c + 1) * FF_CHUNK)
        u = jnp.maximum(jnp.dot(h, wup_ref[:, cols], preferred_element_type=F32), 0.0)
        acc = acc + jnp.dot((u * u).astype(BF16), wdn_ref[cols, :], preferred_element_type=F32)
    o_ref[...] = x1 + _rms(acc, gpost_ref[...])


def _tail(x2, fox, nsa, sga, sgb, wf, wn, wo, wup, wdn, g_mix_post, g_mlp_pre, g_mlp_post):
    n_tok, d = x2.shape
    d_ff = wup.shape[1]
    tm = TAIL_TM
    assert n_tok % tm == 0 and d_ff % FF_CHUNK == 0

    def row(w):
        return pl.BlockSpec((tm, w), lambda i: (i, 0))

    return pl.pallas_call(
        functools.partial(_tail_kernel, d_ff=d_ff),
        grid=(n_tok // tm,),
        in_specs=[row(d), row(fox.shape[1]), row(nsa.shape[1]), row(d), row(d),
                  _resident(wf.shape), _resident(wn.shape), _resident(wo.shape),
                  _resident(wup.shape), _resident(wdn.shape),
                  _resident((1, d)), _resident((1, d)), _resident((1, d))],
        out_specs=row(d),
        out_shape=jax.ShapeDtypeStruct((n_tok, d), F32),
        compiler_params=pltpu.CompilerParams(dimension_semantics=("arbitrary",),
                                             vmem_limit_bytes=VMEM_LIMIT_BYTES),
        name="tail",
    )(x2, fox, nsa, sga, sgb, wf, wn, wo, wup, wdn, g_mix_post, g_mlp_pre, g_mlp_post)


def _layer(x, norm_mix_pre, norm_mix_post, norm_mlp_pre, norm_mlp_post, w_in, b_forget,
           cmp_pos_k, cmp_w1_k, cmp_w2_k, cmp_pos_v, cmp_w1_v, cmp_w2_v,
           w_fox_out, w_nsa_out, w_o, w_up, w_down):
    b, seq, d = x.shape
    n_tok = b * seq
    x2 = x.reshape(n_tok, d)
    w_small, w_ga, w_gb = _rearrange_w_in(w_in, d)
    bmisc = jnp.concatenate([b_forget.astype(F32), jnp.zeros((LANES - FOX_HEADS,), F32)]).reshape(1, LANES)
    (fq, fk, fv, nq, ksl, vsl, kwn, vwn, kc, vc, misc, gates, sga, sgb) = _proj(
        x2, norm_mix_pre.reshape(1, d), w_small, w_ga, w_gb, _rope_tables(seq), bmisc, seq)

    def seq3(t):
        return t.reshape(b, seq, t.shape[-1])

    fox = _fox(seq3(fq), seq3(fk), seq3(fv), _cumsum(seq3(misc)))

    kcmp, vcmp = _compress(kc, vc, b, seq, cmp_pos_k, cmp_w1_k, cmp_w2_k, cmp_pos_v, cmp_w1_v, cmp_w2_v)
    qs, ocmp = _cmpsel(seq3(nq), kcmp, vcmp)
    nsa = _nsa(qs, seq3(ksl), seq3(vsl), seq3(kwn), seq3(vwn), ocmp, seq3(gates))

    out = _tail(x2, fox.reshape(n_tok, FOX_WIDTH), nsa.reshape(n_tok, NSA_WIDTH), sga, sgb,
                w_fox_out.astype(BF16), w_nsa_out.astype(BF16), w_o.astype(BF16),
                w_up.astype(BF16), w_down.astype(BF16),
                norm_mix_post.reshape(1, d), norm_mlp_pre.reshape(1, d), norm_mlp_post.reshape(1, d))
    return out.reshape(b, seq, d)


def kernel(x, norm_mix_pre, norm_mix_post, norm_mlp_pre, norm_mlp_post, w_in, b_forget, cmp_pos_k, cmp_w1_k, cmp_w2_k, cmp_pos_v, cmp_w1_v, cmp_w2_v, w_fox_out, w_nsa_out, w_o, w_up, w_down):
    params = (norm_mix_pre, norm_mix_post, norm_mlp_pre, norm_mlp_post, w_in, b_forget,
              cmp_pos_k, cmp_w1_k, cmp_w2_k, cmp_pos_v, cmp_w1_v, cmp_w2_v,
              w_fox_out, w_nsa_out, w_o, w_up, w_down)
    for layer in range(w_in.shape[0]):
        x = _layer(x, *[p[layer] for p in params])
    return x
```

```python
import functools

import numpy as np
import jax
import jax.numpy as jnp
from jax import lax
from jax.experimental import pallas as pl
from jax.experimental.pallas import tpu as pltpu

HEAD_DIM = 64
FOX_HEADS = 8
NSA_HEADS = 8
NSA_KV_GROUPS = 2
NSA_HEADS_PER_GROUP = NSA_HEADS // NSA_KV_GROUPS
FOX_WIDTH = FOX_HEADS * HEAD_DIM
NSA_WIDTH = NSA_HEADS * HEAD_DIM
NSA_KV_WIDTH = NSA_KV_GROUPS * HEAD_DIM
ROPE_THETA = 10000.0
CMP_BLOCK = 32
CMP_STRIDE = 16
SEL_BLOCK = 64
SEL_TOPN = 8
WINDOW = 512
RMS_EPS = 1e-6
NEG_INF = -1e30
FORCED_BONUS = 1e4
ATTN_SCALE = HEAD_DIM ** -0.5
LOG2E = 1.4426950408889634
Q_SCALE = ATTN_SCALE * LOG2E

LANES = 128
VMEM_LIMIT_BYTES = 52 * 1024 * 1024
SEL_LANE0 = HEAD_DIM
MAX_SEL_BLOCKS = 32

F32 = jnp.float32
BF16 = jnp.bfloat16
HIGHEST = lax.Precision.HIGHEST

PROJ_TM = 512
FOX_TQ = 256
CMP_TQ = 256
NSA_TQ = 256
TAIL_TM = 256
FF_CHUNK = 1024
KV_STEP = 512
BIAS_LANE0 = HEAD_DIM


def _nt_dot(a, b):
    return lax.dot_general(a, b, (((1,), (1,)), ((), ())), preferred_element_type=F32)


def _sigmoid(t):
    return 1.0 / (1.0 + jnp.exp(-t))


def _resident(shape):
    nd = len(shape)
    return pl.BlockSpec(shape, lambda *_: (0,) * nd, pipeline_mode=pl.Buffered(1))


def _proj_layout():
    widths = [('fq', FOX_WIDTH), ('fk', FOX_WIDTH), ('fv', FOX_WIDTH),
              ('nq', NSA_HEADS * LANES),
              ('ksl', NSA_KV_GROUPS * LANES), ('vsl', NSA_KV_GROUPS * LANES),
              ('kwn', NSA_KV_GROUPS * LANES), ('vwn', NSA_KV_GROUPS * LANES),
              ('kc', NSA_KV_WIDTH), ('vc', NSA_KV_WIDTH),
              ('misc', LANES), ('gates', NSA_KV_GROUPS * LANES)]
    out, c = {}, 0
    for name, w in widths:
        out[name] = (c, w)
        c += w
    return out, c


def _rearrange_w_in(w_in, d_model):
    splits = (FOX_WIDTH, FOX_WIDTH, FOX_WIDTH, FOX_HEADS,
              NSA_WIDTH, NSA_KV_WIDTH, NSA_KV_WIDTH, NSA_KV_WIDTH, NSA_KV_WIDTH, NSA_KV_WIDTH, NSA_KV_WIDTH,
              3 * NSA_HEADS, d_model, d_model)
    pts = [int(p) for p in np.cumsum(splits)[:-1]]
    fq, fk, fv, ff, nq, kc, vc, ksl, vsl, kwn, vwn, ng, ga, gb = jnp.split(w_in, pts, axis=-1)
    d = w_in.shape[0]

    def pad_heads(w, n):
        w = w.reshape(d, n, HEAD_DIM)
        return jnp.concatenate([w, jnp.zeros_like(w)], axis=-1).reshape(d, n * LANES)

    def dup_heads(w, n):
        w = w.reshape(d, n, HEAD_DIM)
        return jnp.concatenate([w, w], axis=-1).reshape(d, n * LANES)

    misc = jnp.concatenate([ff, jnp.zeros((d, LANES - FOX_HEADS), w_in.dtype)], axis=-1)
    per_group = 3 * NSA_HEADS_PER_GROUP
    ngg = ng.reshape(d, NSA_KV_GROUPS, per_group)
    gates = jnp.concatenate([ngg, jnp.zeros((d, NSA_KV_GROUPS, LANES - per_group), w_in.dtype)],
                            axis=-1).reshape(d, NSA_KV_GROUPS * LANES)
    w_small = jnp.concatenate(
        [fq, fk, fv, pad_heads(nq, NSA_HEADS),
         pad_heads(ksl, NSA_KV_GROUPS), dup_heads(vsl, NSA_KV_GROUPS),
         pad_heads(kwn, NSA_KV_GROUPS), dup_heads(vwn, NSA_KV_GROUPS),
         kc, vc, misc, gates], axis=-1)
    return w_small.astype(BF16), ga.astype(BF16), gb.astype(BF16)


def _rope_tables(seq):
    half = HEAD_DIM // 2
    inv = jnp.power(jnp.float32(ROPE_THETA), -jnp.arange(0, HEAD_DIM, 2, dtype=F32) / HEAD_DIM)
    ang = jnp.arange(seq, dtype=F32)[:, None] * inv[None, :]
    cos, sin = jnp.cos(ang), jnp.sin(ang)
    zero = jnp.zeros_like(sin)
    cos_t = jnp.concatenate([cos, cos, cos, cos], axis=-1)
    s_lo = jnp.concatenate([-sin, zero, -sin, zero], axis=-1)
    s_hi = jnp.concatenate([zero, sin, zero, sin], axis=-1)
    assert cos_t.shape == (seq, 4 * half)
    return cos_t, s_lo, s_hi


def _proj_kernel(x_ref, g_ref, w_ref, wga_ref, wgb_ref, cos_ref, slo_ref, shi_ref, bmisc_ref,
                 fq_ref, fk_ref, fv_ref, nq_ref, ksl_ref, vsl_ref, kwn_ref, vwn_ref,
                 kc_ref, vc_ref, misc_ref, gates_ref, ga_ref, gb_ref, *, tm, seq, layout):
    x = x_ref[...]
    h = x * lax.rsqrt(jnp.mean(x * x, axis=-1, keepdims=True) + RMS_EPS) * g_ref[...]
    h = h.astype(BF16)
    cos, slo, shi = cos_ref[...], slo_ref[...], shi_ref[...]
    half = HEAD_DIM // 2

    def mm(name):
        c0, n = layout[name]
        return jnp.dot(h, w_ref[:, c0:c0 + n], preferred_element_type=F32)

    def rope_block(t):
        return t * cos + pltpu.roll(t, LANES - half, axis=1) * slo + pltpu.roll(t, half, axis=1) * shi

    def rope(t):
        n = t.shape[1] // LANES
        return [rope_block(t[:, b * LANES:(b + 1) * LANES]) for b in range(n)]

    lane = lax.broadcasted_iota(jnp.int32, (tm, LANES), 1)
    left = lane < HEAD_DIM

    def split_heads(t, o_ref, scale):
        for p in range(FOX_HEADS // 2):
            blk = t[:, p * LANES:(p + 1) * LANES] * scale
            o_ref[:, (2 * p) * LANES:(2 * p + 1) * LANES] = jnp.where(left, blk, 0.0).astype(BF16)
            o_ref[:, (2 * p + 1) * LANES:(2 * p + 2) * LANES] = jnp.where(
                left, pltpu.roll(blk, HEAD_DIM, axis=1), 0.0).astype(BF16)

    split_heads(mm('fq'), fq_ref, Q_SCALE)
    split_heads(mm('fk'), fk_ref, 1.0)
    fv = mm('fv')
    for p in range(FOX_HEADS // 2):
        blk = fv[:, p * LANES:(p + 1) * LANES]
        fv_ref[:, (2 * p) * LANES:(2 * p + 1) * LANES] = jnp.where(
            left, blk, (lane == HEAD_DIM).astype(F32)).astype(BF16)
        fv_ref[:, (2 * p + 1) * LANES:(2 * p + 2) * LANES] = jnp.where(
            left, (lane == 0).astype(F32), blk).astype(BF16)

    for b, blk in enumerate(rope(mm('nq'))):
        nq_ref[:, b * LANES:(b + 1) * LANES] = (blk * Q_SCALE).astype(BF16)

    pos = (pl.program_id(0) % (seq // tm)) * tm + lax.broadcasted_iota(jnp.int32, (tm, LANES), 0)
    onehot = (lane - SEL_LANE0) == (pos // SEL_BLOCK)
    for b, blk in enumerate(rope(mm('ksl'))):
        ksl_ref[:, b * LANES:(b + 1) * LANES] = jnp.where(onehot, 1.0, blk).astype(BF16)
    vsl_ref[...] = mm('vsl').astype(BF16)
    for b, blk in enumerate(rope(mm('kwn'))):
        kwn_ref[:, b * LANES:(b + 1) * LANES] = blk.astype(BF16)
    vwn_ref[...] = mm('vwn').astype(BF16)

    kc_ref[...] = rope(mm('kc'))[0]
    vc_ref[...] = mm('vc')

    t = mm('misc') + bmisc_ref[...]
    misc_ref[...] = -(jnp.maximum(-t, 0.0) + jnp.log(1.0 + jnp.exp(-jnp.abs(t))))
    gates_ref[...] = _sigmoid(mm('gates'))

    ga_ref[...] = _sigmoid(jnp.dot(h, wga_ref[...], preferred_element_type=F32)).astype(BF16)
    gb_ref[...] = _sigmoid(jnp.dot(h, wgb_ref[...], preferred_element_type=F32)).astype(BF16)


def _proj(x2, g, w_small, w_ga, w_gb, tables, bmisc, seq):
    n_tok, d = x2.shape
    tm = PROJ_TM
    layout, n_cols = _proj_layout()
    assert n_tok % tm == 0 and seq % tm == 0 and w_small.shape == (d, n_cols)
    tiles_per_seq = seq // tm

    def row(w):
        return pl.BlockSpec((tm, w), lambda i: (i, 0))

    def tab():
        return pl.BlockSpec((tm, LANES), lambda i: (i % tiles_per_seq, 0))

    outs = [('fq', BF16), ('fk', BF16), ('fv', BF16), ('nq', BF16), ('ksl', BF16), ('vsl', BF16),
            ('kwn', BF16), ('vwn', BF16), ('kc', F32), ('vc', F32), ('misc', F32), ('gates', F32)]
    widths = {n: layout[n][1] for n, _ in outs}
    widths.update(fq=FOX_HEADS * LANES, fk=FOX_HEADS * LANES, fv=FOX_HEADS * LANES)
    out_shape = [jax.ShapeDtypeStruct((n_tok, widths[n]), dt) for n, dt in outs]
    out_specs = [row(widths[n]) for n, _ in outs]
    out_shape += [jax.ShapeDtypeStruct((n_tok, d), BF16)] * 2
    out_specs += [row(d), row(d)]
    return pl.pallas_call(
        functools.partial(_proj_kernel, tm=tm, seq=seq, layout=layout),
        grid=(n_tok // tm,),
        in_specs=[row(d), _resident((1, d)), _resident(w_small.shape), _resident(w_ga.shape),
                  _resident(w_gb.shape), tab(), tab(), tab(), _resident((1, LANES))],
        out_specs=out_specs,
        out_shape=out_shape,
        compiler_params=pltpu.CompilerParams(dimension_semantics=("arbitrary",),
                                             vmem_limit_bytes=VMEM_LIMIT_BYTES),
        name="proj",
    )(x2, g, w_small, w_ga, w_gb, *tables, bmisc)


def _cumsum_kernel(m_ref, ccol_ref, *, seq):
    r = lax.broadcasted_iota(jnp.int32, (LANES, LANES), 0)
    c = lax.broadcasted_iota(jnp.int32, (LANES, LANES), 1)
    tri = (r >= c).astype(F32)
    carry = jnp.zeros((1, LANES), F32)
    for b in range(seq // LANES):
        blk = m_ref[0, b * LANES:(b + 1) * LANES, :]
        cs = jnp.dot(tri, blk, precision=HIGHEST, preferred_element_type=F32) + carry
        ccol_ref[0, b * LANES:(b + 1) * LANES, :] = cs
        carry = cs[LANES - 1:LANES, :]


def _cumsum(misc3):
    b, seq, _ = misc3.shape
    assert seq % LANES == 0
    return pl.pallas_call(
        functools.partial(_cumsum_kernel, seq=seq),
        grid=(b,),
        in_specs=[pl.BlockSpec((1, seq, LANES), lambda i: (i, 0, 0))],
        out_specs=pl.BlockSpec((1, seq, LANES), lambda i: (i, 0, 0)),
        out_shape=jax.ShapeDtypeStruct((b, seq, LANES), F32),
        compiler_params=pltpu.CompilerParams(dimension_semantics=("arbitrary",),
                                             vmem_limit_bytes=VMEM_LIMIT_BYTES),
        name="cumsum",
    )(misc3)


def _softmax_step(s, pv, m_ref, acc_ref, rows):
    m_prev = m_ref[rows]
    m_new = jnp.maximum(m_prev, jnp.max(s, axis=-1, keepdims=True))
    p = jnp.exp2(s - m_new).astype(BF16)
    acc_ref[rows] = jnp.exp2(m_prev - m_new) * acc_ref[rows] + pv(p)
    m_ref[rows] = m_new


def _mask_table(tq):
    d = np.arange(tq)[:, None] - np.arange(tq)[None, :]
    neg = np.full((tq, tq), NEG_INF, np.float32)
    zero = np.zeros((tq, tq), np.float32)
    return jnp.asarray(np.stack([np.where(d > 0, zero, neg), zero, np.where(d <= 0, zero, neg), neg]))


def _mask_bias(tab_ref, key_tile0, n_tiles, q_tile, reps):
    rows = []
    for u in range(n_tiles):
        t = tab_ref[jnp.clip(key_tile0 + u - q_tile + 2, 0, 3)]
        rows.append(jnp.concatenate([t] * reps, axis=1) if reps > 1 else t)
    return jnp.concatenate(rows, axis=0) if n_tiles > 1 else rows[0]


def _split3(c):
    hi = c.astype(BF16).astype(F32)
    r = c - hi
    mid = r.astype(BF16).astype(F32)
    lo = (r - mid).astype(BF16).astype(F32)
    return hi, mid, lo


def _softmax_step_t(s, pv, m_ref, acc_ref, idx):
    m_prev = m_ref[idx]
    m_new = jnp.maximum(m_prev, jnp.max(s, axis=0, keepdims=True))
    p = jnp.exp2(s - m_new).astype(BF16)
    acc_ref[idx] = jnp.exp2(m_prev - m_new) * acc_ref[idx] + pv(p)
    m_ref[idx] = m_new


def _fox_kernel(q_ref, k_ref, v_ref, ccol_ref, o_ref, kx_s, vt_s, qx_s, m_s, acc_s, *, tq, tk, seq):
    i = pl.program_id(1)
    lane = lax.broadcasted_iota(jnp.int32, (tq, LANES), 1)
    left = lane < HEAD_DIM
    b0 = BIAS_LANE0

    @pl.when(i == 0)
    def _build_keys():
        def chunk(r, carry):
            rows = pl.ds(pl.multiple_of(r * tq, tq), tq)
            for h in range(FOX_HEADS):
                cols = slice(h * LANES, (h + 1) * LANES)
                hi, mid, lo = _split3(ccol_ref[0, rows, h:h + 1] * LOG2E)
                e = jnp.where(lane < b0 + 3, 1.0,
                              jnp.where(lane == b0 + 3, -hi,
                                        jnp.where(lane == b0 + 4, -mid, jnp.where(lane == b0 + 5, -lo, 0.0))))
                kx_s[h, rows, :] = jnp.where(left, k_ref[0, rows, cols].astype(F32), e).astype(BF16)
            return carry

        lax.fori_loop(0, seq // tq, chunk, 0)

        def vchunk(c, carry):
            rows = pl.ds(pl.multiple_of(c * tk, tk), tk)
            for h in range(FOX_HEADS):
                cols = slice(h * LANES, (h + 1) * LANES)
                vt_s[c, h] = v_ref[0, rows, cols].astype(F32).T.astype(BF16)
            return carry

        lax.fori_loop(0, seq // tk, vchunk, 0)

    qrows = pl.ds(pl.multiple_of(i * tq, tq), tq)
    n_full = (i * tq) // tk
    q_pos = i * tq + lax.broadcasted_iota(jnp.int32, (tk, tq), 1)
    k_off = lax.broadcasted_iota(jnp.int32, (tk, tq), 0)
    row = lax.broadcasted_iota(jnp.int32, (LANES, tq), 0)
    n_pairs = FOX_HEADS // 2
    for h in range(FOX_HEADS):
        hi, mid, lo = _split3(ccol_ref[0, qrows, h:h + 1] * LOG2E)
        e = jnp.where(lane == b0, hi,
                      jnp.where(lane == b0 + 1, mid,
                                jnp.where(lane == b0 + 2, lo, jnp.where(lane < b0 + 6, 1.0, 0.0))))
        qx_s[h] = jnp.where(left, q_ref[0, :, h * LANES:(h + 1) * LANES].astype(F32), e).astype(BF16)
    m_s[...] = jnp.full(m_s.shape, NEG_INF, F32)
    acc_s[...] = jnp.zeros(acc_s.shape, F32)

    def scores(j, p):
        rows = pl.ds(pl.multiple_of(j * tk, tk), tk)
        return tuple(_nt_dot(kx_s[2 * p + slot, rows, :], qx_s[2 * p + slot]) for slot in range(2))

    def consume(j, p, ss, masked):
        for slot in range(2):
            h = 2 * p + slot
            s = ss[slot]
            if masked:
                s = jnp.where(j * tk + k_off <= q_pos, s, NEG_INF)
            _softmax_step_t(s, lambda pr, h=h: jnp.dot(vt_s[j, h], pr, preferred_element_type=F32), m_s, acc_s, h)

    def key_step(j, ss, masked, last):
        for p in range(n_pairs):
            if p + 1 < n_pairs:
                nxt = scores(j, p + 1)
            else:
                nxt = None if last else scores(j + 1, 0)
            consume(j, p, ss, masked)
            ss = nxt
        return ss

    ss = lax.fori_loop(0, n_full, lambda j, ss: key_step(j, ss, False, False), scores(0, 0))
    key_step(n_full, ss, True, True)

    for hp in range(n_pairs):
        a0, a1 = acc_s[2 * hp], acc_s[2 * hp + 1]
        ot = jnp.where(row < HEAD_DIM, a0 / a0[HEAD_DIM:HEAD_DIM + 1, :], a1 / a1[0:1, :])
        o_ref[0, :, hp * LANES:(hp + 1) * LANES] = ot.T.astype(BF16)


def _fox(fq, fk, fv, ccol):
    b, seq, w = fq.shape
    tq, tk = FOX_TQ, KV_STEP
    assert seq % tq == 0 and seq % tk == 0 and tk % tq == 0 and w == FOX_HEADS * LANES
    whole = pl.BlockSpec((1, seq, w), lambda bi, i: (bi, 0, 0))
    return pl.pallas_call(
        functools.partial(_fox_kernel, tq=tq, tk=tk, seq=seq),
        grid=(b, seq // tq),
        in_specs=[pl.BlockSpec((1, tq, w), lambda bi, i: (bi, i, 0)), whole, whole,
                  pl.BlockSpec((1, seq, LANES), lambda bi, i: (bi, 0, 0))],
        out_specs=pl.BlockSpec((1, tq, FOX_WIDTH), lambda bi, i: (bi, i, 0)),
        out_shape=jax.ShapeDtypeStruct((b, seq, FOX_WIDTH), BF16),
        scratch_shapes=[pltpu.VMEM((FOX_HEADS, seq, LANES), BF16),
                        pltpu.VMEM((seq // tk, FOX_HEADS, LANES, tk), BF16),
                        pltpu.VMEM((FOX_HEADS, tq, LANES), BF16),
                        pltpu.VMEM((FOX_HEADS, 1, tq), F32), pltpu.VMEM((FOX_HEADS, LANES, tq), F32)],
        compiler_params=pltpu.CompilerParams(dimension_semantics=("arbitrary", "arbitrary"),
                                             vmem_limit_bytes=VMEM_LIMIT_BYTES),
        name="fox",
    )(fq, fk, fv, ccol)


def _compress_kernel(yk_ref, yv_ref, posk_ref, w1k_ref, w2k_ref, posv_ref, w1v_ref, w2v_ref,
                     kc_ref, vc_ref, *, nb):
    half = CMP_STRIDE * HEAD_DIM

    lane = lax.broadcasted_iota(jnp.int32, (nb, LANES), 1)
    left = lane < HEAD_DIM

    def regroup(x_ref):
        y = [[], []]
        for j in range(CMP_STRIDE // 2):
            xe = x_ref[0, pl.ds(2 * j, nb, stride=CMP_STRIDE), :]
            xo = x_ref[0, pl.ds(2 * j + 1, nb, stride=CMP_STRIDE), :]
            y[0].append(jnp.where(left, xe, pltpu.roll(xo, HEAD_DIM, axis=1)))
            y[1].append(jnp.where(left, pltpu.roll(xe, HEAD_DIM, axis=1), xo))
        return [jnp.concatenate(t, axis=1) for t in y]

    def comp(y, pos_ref, w1_ref, w2_ref):
        pos = pos_ref[...]
        a = jnp.dot(y + pos[:, :half], w1_ref[0:half, :], precision=HIGHEST, preferred_element_type=F32)
        bm = jnp.dot(y + pos[:, half:], w1_ref[half:2 * half, :], precision=HIGHEST,
                     preferred_element_type=F32)
        pre = a + pltpu.roll(bm, nb - 1, axis=0)
        return jnp.dot(pre * _sigmoid(pre), w2_ref[...], precision=HIGHEST, preferred_element_type=F32)

    for g, (yk, yv) in enumerate(zip(regroup(yk_ref), regroup(yv_ref))):
        kc_ref[0, g] = comp(yk, posk_ref, w1k_ref, w2k_ref).astype(BF16)
        vc_ref[0, g] = comp(yv, posv_ref, w1v_ref, w2v_ref).astype(BF16)


def _compress(kc, vc, b, seq, pos_k, w1_k, w2_k, pos_v, w1_v, w2_v):
    g = NSA_KV_GROUPS
    nb = seq // CMP_STRIDE
    half = CMP_STRIDE * HEAD_DIM
    assert CMP_BLOCK == 2 * CMP_STRIDE and seq % CMP_STRIDE == 0 and nb % 8 == 0
    assert g == 2 and kc.shape[-1] == LANES

    zeros1 = jnp.zeros((CMP_BLOCK * HEAD_DIM, HEAD_DIM), F32)
    zeros2 = jnp.zeros((HEAD_DIM, HEAD_DIM), F32)
    w1k = jnp.concatenate([w1_k, zeros1], axis=-1)
    w1v = jnp.concatenate([w1_v, zeros1], axis=-1)
    w2k = jnp.concatenate([jnp.concatenate([w2_k, zeros2], axis=-1), jnp.zeros((HEAD_DIM, LANES), F32)], axis=0)
    w2v = jnp.concatenate([jnp.concatenate([w2_v, w2_v], axis=-1), jnp.zeros((HEAD_DIM, LANES), F32)], axis=0)
    yspec = pl.BlockSpec((1, seq, LANES), lambda bi: (bi, 0, 0))
    ospec = pl.BlockSpec((1, g, nb, LANES), lambda bi: (bi, 0, 0, 0))
    return pl.pallas_call(
        functools.partial(_compress_kernel, nb=nb),
        grid=(b,),
        in_specs=[yspec, yspec,
                  _resident((1, 2 * half)), _resident(w1k.shape), _resident(w2k.shape),
                  _resident((1, 2 * half)), _resident(w1v.shape), _resident(w2v.shape)],
        out_specs=[ospec, ospec],
        out_shape=[jax.ShapeDtypeStruct((b, g, nb, LANES), BF16)] * 2,
        compiler_params=pltpu.CompilerParams(dimension_semantics=("arbitrary",),
                                             vmem_limit_bytes=VMEM_LIMIT_BYTES),
        name="compress",
    )(kc.reshape(b, seq, LANES), vc.reshape(b, seq, LANES), pos_k.reshape(1, 2 * half), w1k, w2k,
      pos_v.reshape(1, 2 * half), w1v, w2v)


def _overlap_matrix(nb, n_cmp, n_sel):
    ci = np.arange(nb)[:, None] * CMP_STRIDE
    sj = (np.arange(LANES)[None, :] - SEL_LANE0) * SEL_BLOCK
    jj = np.arange(LANES)[None, :] - SEL_LANE0
    ov = (ci < sj + SEL_BLOCK) & (ci + CMP_BLOCK > sj) & (jj >= 0) & (jj < n_sel) & (np.arange(nb)[:, None] < n_cmp)
    return jnp.asarray(ov.astype(np.float32))


def _cmpsel_kernel(q_ref, kc_ref, vc_ref, ovt_ref, qs_ref, ocmp_ref, vct_s, *, tq, n_cmp, nb):
    i = pl.program_id(2)

    @pl.when(i == 0)
    def _transpose_values():
        vct_s[...] = vc_ref[0, 0].astype(F32).T.astype(BF16)

    n_idx = lax.broadcasted_iota(jnp.int32, (nb, tq), 0)
    t_pos = i * tq + lax.broadcasted_iota(jnp.int32, (nb, tq), 1)
    cmask = (n_idx * CMP_STRIDE + (CMP_BLOCK - 1) <= t_pos) & (n_idx < n_cmp)
    top = lax.broadcasted_iota(jnp.int32, (LANES, tq), 0) < HEAD_DIM
    kc = kc_ref[0, 0]
    vct = vct_s[...]
    psum = jnp.zeros((nb, tq), F32)
    outs = []
    for h in range(NSA_HEADS_PER_GROUP):
        s = jnp.where(cmask, _nt_dot(kc, q_ref[0, :, h * LANES:(h + 1) * LANES]), NEG_INF)
        m = jnp.max(s, axis=0, keepdims=True)
        p = jnp.where(cmask, jnp.exp2(s - m), 0.0)
        p = p / jnp.maximum(jnp.sum(p, axis=0, keepdims=True), 1e-30)
        psum = psum + p
        outs.append(jnp.dot(vct, p.astype(BF16), preferred_element_type=F32))
    for hp in range(NSA_HEADS_PER_GROUP // 2):
        ocmp_ref[0, :, hp * LANES:(hp + 1) * LANES] = jnp.where(top, outs[2 * hp], outs[2 * hp + 1]).T

    p_slc = jnp.dot(ovt_ref[...], psum, precision=HIGHEST, preferred_element_type=F32)
    sc = p_slc[SEL_LANE0:SEL_LANE0 + MAX_SEL_BLOCKS, :]
    j = lax.broadcasted_iota(jnp.int32, (MAX_SEL_BLOCKS, tq), 0)
    cur = (i * tq + lax.broadcasted_iota(jnp.int32, (MAX_SEL_BLOCKS, tq), 1)) // SEL_BLOCK
    forced = (j == 0) | (j == cur) | (j == cur - 1)
    score = jnp.where(j <= cur, sc + FORCED_BONUS * forced.astype(F32), NEG_INF)
    j_f = j.astype(F32)
    sel = jnp.zeros((MAX_SEL_BLOCKS, tq), jnp.bool_)
    for _ in range(SEL_TOPN):
        mx = jnp.max(score, axis=0, keepdims=True)
        first = jnp.min(jnp.where(score == mx, j_f, 1e4), axis=0, keepdims=True)
        chosen = j_f == first
        sel = sel | chosen
        score = jnp.where(chosen, -jnp.inf, score)
    bias_t = jnp.concatenate([jnp.zeros((SEL_LANE0, tq), F32), jnp.where(sel, 0.0, NEG_INF),
                              jnp.zeros((LANES - SEL_LANE0 - MAX_SEL_BLOCKS, tq), F32)], axis=0)
    bias = bias_t.T
    for h in range(NSA_HEADS_PER_GROUP):
        cols = slice(h * LANES, (h + 1) * LANES)
        qs_ref[0, :, cols] = (q_ref[0, :, cols].astype(F32) + bias).astype(BF16)


def _cmpsel(nq3, kcmp, vcmp):
    b, seq, _ = nq3.shape
    g, hg = NSA_KV_GROUPS, NSA_HEADS_PER_GROUP
    tq = CMP_TQ
    nb = kcmp.shape[2]
    n_cmp = (seq - CMP_BLOCK) // CMP_STRIDE + 1
    n_sel = seq // SEL_BLOCK
    assert seq % tq == 0 and n_sel <= MAX_SEL_BLOCKS and n_sel >= SEL_TOPN
    ov = _overlap_matrix(nb, n_cmp, n_sel).T
    qspec = pl.BlockSpec((1, tq, hg * LANES), lambda bi, gi, i: (bi, i, gi))
    cspec = pl.BlockSpec((1, 1, nb, LANES), lambda bi, gi, i: (bi, gi, 0, 0))
    return pl.pallas_call(
        functools.partial(_cmpsel_kernel, tq=tq, n_cmp=n_cmp, nb=nb),
        grid=(b, g, seq // tq),
        in_specs=[qspec, cspec, cspec, _resident(ov.shape)],
        out_specs=[qspec, pl.BlockSpec((1, tq, hg * HEAD_DIM), lambda bi, gi, i: (bi, i, gi))],
        out_shape=[jax.ShapeDtypeStruct(nq3.shape, BF16),
                   jax.ShapeDtypeStruct((b, seq, NSA_WIDTH), F32)],
        scratch_shapes=[pltpu.VMEM((LANES, nb), BF16)],
        compiler_params=pltpu.CompilerParams(dimension_semantics=("arbitrary",) * 3,
                                             vmem_limit_bytes=VMEM_LIMIT_BYTES),
        name="cmpsel",
    )(nq3, kcmp, vcmp, ov)


def _nsa_kernel(qs_ref, ks_ref, vs_ref, kw_ref, vw_ref, ocmp_ref, gates_ref, tab_ref, o_ref,
                vse_s, vso_s, vwe_s, vwo_s, m_s, acc_s, *, tq, tk, seq):
    i = pl.program_id(1)
    n_g, hg = NSA_KV_GROUPS, NSA_HEADS_PER_GROUP
    rows_all = hg * tq
    half = rows_all // 2
    wt = WINDOW // tq + 1

    @pl.when(i == 0)
    def _build_values():
        lane = lax.broadcasted_iota(jnp.int32, (tk, LANES), 1)
        one_e = (lane == HEAD_DIM).astype(F32)
        one_o = (lane == 0).astype(F32)

        def chunk(c, carry):
            rows = pl.ds(pl.multiple_of(c * tk, tk), tk)
            for g in range(n_g):
                cols = slice(g * LANES, (g + 1) * LANES)
                v = vs_ref[0, rows, cols].astype(F32)
                vse_s[g, c] = jnp.where(lane < HEAD_DIM, v, one_e).T.astype(BF16)
                vso_s[g, c] = jnp.where(lane < HEAD_DIM, one_o, v).T.astype(BF16)
                w = vw_ref[0, rows, cols].astype(F32)
                we = jnp.where(lane < HEAD_DIM, w, one_e).T.astype(BF16)
                wo = jnp.where(lane < HEAD_DIM, one_o, w).T.astype(BF16)
                for t in range(tk // tq):
                    vwe_s[g, c * (tk // tq) + t] = we[:, t * tq:(t + 1) * tq]
                    vwo_s[g, c * (tk // tq) + t] = wo[:, t * tq:(t + 1) * tq]
            return carry

        lax.fori_loop(0, seq // tk, chunk, 0)

    order = tuple(range(0, hg, 2)) + tuple(range(1, hg, 2))
    q = [jnp.concatenate([qs_ref[0, :, (g * hg + h) * LANES:(g * hg + h + 1) * LANES] for h in order], axis=0)
         for g in range(n_g)]

    n_full = (i * tq) // tk
    m_s[...] = jnp.full(m_s.shape, NEG_INF, F32)
    acc_s[...] = jnp.zeros(acc_s.shape, F32)

    def scores(j, g):
        rows = pl.ds(pl.multiple_of(j * tk, tk), tk)
        return _nt_dot(ks_ref[0, rows, g * LANES:(g + 1) * LANES], q[g])

    def consume(j, g, s, masked):
        if masked:
            s = s + _mask_bias(tab_ref, j * (tk // tq), tk // tq, i, hg)
        _softmax_step_t(s, lambda p: jnp.concatenate(
            [jnp.dot(vse_s[g, j], p[:, :half], preferred_element_type=F32),
             jnp.dot(vso_s[g, j], p[:, half:], preferred_element_type=F32)], axis=1), m_s, acc_s, g)

    st = jnp.maximum(i - (wt - 1), 0)

    def win_scores(g):
        wrows = pl.ds(pl.multiple_of(st * tq, tq), wt * tq)
        return _nt_dot(kw_ref[0, wrows, g * LANES:(g + 1) * LANES], q[g])

    def win_consume(g, s_win):
        s_win = s_win + _mask_bias(tab_ref, st, wt, i, hg)
        p = jnp.exp2(s_win - jnp.max(s_win, axis=0, keepdims=True)).astype(BF16)
        a_win = jnp.zeros((LANES, rows_all), F32)
        for t in range(wt):
            pt = p[t * tq:(t + 1) * tq]
            a_win = a_win + jnp.concatenate(
                [jnp.dot(vwe_s[g, st + t], pt[:, :half], preferred_element_type=F32),
                 jnp.dot(vwo_s[g, st + t], pt[:, half:], preferred_element_type=F32)], axis=1)
        return a_win

    def key_step(j, s):
        for g in range(n_g):
            nxt = scores(j, g + 1) if g + 1 < n_g else scores(j + 1, 0)
            consume(j, g, s, False)
            s = nxt
        return s

    s = lax.fori_loop(0, n_full, key_step, scores(0, 0))
    for g in range(n_g):
        nxt = scores(n_full, g + 1) if g + 1 < n_g else win_scores(0)
        consume(n_full, g, s, True)
        s = nxt
    a_win = []
    for g in range(n_g):
        nxt = win_scores(g + 1) if g + 1 < n_g else None
        a_win.append(win_consume(g, s))
        s = nxt

    lane = lax.broadcasted_iota(jnp.int32, (tq, LANES), 1)
    left = lane < HEAD_DIM
    top = lax.broadcasted_iota(jnp.int32, (LANES, tq), 0) < HEAD_DIM
    for g in range(n_g):
        gates = gates_ref[0, :, g * LANES:(g + 1) * LANES]
        a_sel = acc_s[g]
        for hp in range(hg // 2):
            h0, h1 = 2 * hp, 2 * hp + 1
            cols = slice((g * (hg // 2) + hp) * LANES, (g * (hg // 2) + hp + 1) * LANES)

            def pair(a):
                ae = a[:, hp * tq:(hp + 1) * tq]
                ao = a[:, half + hp * tq:half + (hp + 1) * tq]
                return jnp.where(top, ae / jnp.maximum(ae[HEAD_DIM:HEAD_DIM + 1, :], 1e-30),
                                 ao / jnp.maximum(ao[0:1, :], 1e-30)).T

            def gate(branch):
                return jnp.where(left, gates[:, 3 * h0 + branch:3 * h0 + branch + 1],
                                 gates[:, 3 * h1 + branch:3 * h1 + branch + 1])

            o = gate(0) * ocmp_ref[0, :, cols] + gate(1) * pair(a_sel) + gate(2) * pair(a_win[g])
            o_ref[0, :, cols] = o.astype(BF16)


def _nsa(qs, ksl, vsl, kwn, vwn, ocmp, gates):
    b, seq, _ = qs.shape
    g, hg = NSA_KV_GROUPS, NSA_HEADS_PER_GROUP
    tq, tk = NSA_TQ, KV_STEP
    assert seq % tk == 0 and tk % tq == 0 and WINDOW % tq == 0 and seq >= WINDOW + tq and tq & (tq - 1) == 0
    assert hg % 2 == 0
    assert WINDOW == 2 * tq
    kv = pl.BlockSpec((1, seq, g * LANES), lambda bi, i: (bi, 0, 0))
    ospec = pl.BlockSpec((1, tq, NSA_WIDTH), lambda bi, i: (bi, i, 0))
    return pl.pallas_call(
        functools.partial(_nsa_kernel, tq=tq, tk=tk, seq=seq),
        grid=(b, seq // tq),
        in_specs=[pl.BlockSpec((1, tq, g * hg * LANES), lambda bi, i: (bi, i, 0)), kv, kv, kv, kv,
                  ospec, pl.BlockSpec((1, tq, g * LANES), lambda bi, i: (bi, i, 0)), _resident((4, tq, tq))],
        out_specs=ospec,
        out_shape=jax.ShapeDtypeStruct((b, seq, NSA_WIDTH), BF16),
        scratch_shapes=[pltpu.VMEM((g, seq // tk, LANES, tk), BF16)] * 2
        + [pltpu.VMEM((g, seq // tq, LANES, tq), BF16)] * 2
        + [pltpu.VMEM((g, 1, hg * tq), F32), pltpu.VMEM((g, LANES, hg * tq), F32)],
        compiler_params=pltpu.CompilerParams(dimension_semantics=("arbitrary",) * 2,
                                             vmem_limit_bytes=VMEM_LIMIT_BYTES),
        name="nsa",
    )(qs, ksl, vsl, kwn, vwn, ocmp, gates, _mask_table(tq))


def _rms(y, g):
    return y * lax.rsqrt(jnp.mean(y * y, axis=-1, keepdims=True) + RMS_EPS) * g


def _tail_kernel(x_ref, fox_ref, nsa_ref, ga_ref, gb_ref, wf_ref, wn_ref, wo_ref, wup_ref, wdn_ref,
                 gmix_ref, gpre_ref, gpost_ref, o_ref, *, d_ff):
    a = jnp.dot(fox_ref[...], wf_ref[...], preferred_element_type=F32)
    b = jnp.dot(nsa_ref[...], wn_ref[...], preferred_element_type=F32)
    mix = ga_ref[...].astype(F32) * a + gb_ref[...].astype(F32) * b
    y = jnp.dot(mix.astype(BF16), wo_ref[...], preferred_element_type=F32)
    x1 = x_ref[...] + _rms(y, gmix_ref[...])
    h = _rms(x1, gpre_ref[...]).astype(BF16)
    acc = jnp.zeros(x1.shape, F32)
    for c in range(d_ff // FF_CHUNK):
        cols = slice(c * FF_CHUNK, (c + 1) * FF_CHUNK)
        u = jnp.maximum(jnp.dot(h, wup_ref[:, cols], preferred_element_type=F32), 0.0)
        acc = acc + jnp.dot((u * u).astype(BF16), wdn_ref[cols, :], preferred_element_type=F32)
    o_ref[...] = x1 + _rms(acc, gpost_ref[...])


def _tail(x2, fox, nsa, sga, sgb, wf, wn, wo, wup, wdn, g_mix_post, g_mlp_pre, g_mlp_post):
    n_tok, d = x2.shape
    d_ff = wup.shape[1]
    tm = TAIL_TM
    assert n_tok % tm == 0 and d_ff % FF_CHUNK == 0

    def row(w):
        return pl.BlockSpec((tm, w), lambda i: (i, 0))

    return pl.pallas_call(
        functools.partial(_tail_kernel, d_ff=d_ff),
        grid=(n_tok // tm,),
        in_specs=[row(d), row(fox.shape[1]), row(nsa.shape[1]), row(d), row(d),
                  _resident(wf.shape), _resident(wn.shape), _resident(wo.shape),
                  _resident(wup.shape), _resident(wdn.shape),
                  _resident((1, d)), _resident((1, d)), _resident((1, d))],
        out_specs=row(d),
        out_shape=jax.ShapeDtypeStruct((n_tok, d), F32),
        compiler_params=pltpu.CompilerParams(dimension_semantics=("arbitrary",),
                                             vmem_limit_bytes=VMEM_LIMIT_BYTES),
        name="tail",
    )(x2, fox, nsa, sga, sgb, wf, wn, wo, wup, wdn, g_mix_post, g_mlp_pre, g_mlp_post)


def _layer(x, norm_mix_pre, norm_mix_post, norm_mlp_pre, norm_mlp_post, w_in, b_forget,
           cmp_pos_k, cmp_w1_k, cmp_w2_k, cmp_pos_v, cmp_w1_v, cmp_w2_v,
           w_fox_out, w_nsa_out, w_o, w_up, w_down):
    b, seq, d = x.shape
    n_tok = b * seq
    x2 = x.reshape(n_tok, d)
    w_small, w_ga, w_gb = _rearrange_w_in(w_in, d)
    bmisc = jnp.concatenate([b_forget.astype(F32), jnp.zeros((LANES - FOX_HEADS,), F32)]).reshape(1, LANES)
    (fq, fk, fv, nq, ksl, vsl, kwn, vwn, kc, vc, misc, gates, sga, sgb) = _proj(
        x2, norm_mix_pre.reshape(1, d), w_small, w_ga, w_gb, _rope_tables(seq), bmisc, seq)

    def seq3(t):
        return t.reshape(b, seq, t.shape[-1])

    fox = _fox(seq3(fq), seq3(fk), seq3(fv), _cumsum(seq3(misc)))

    kcmp, vcmp = _compress(kc, vc, b, seq, cmp_pos_k, cmp_w1_k, cmp_w2_k, cmp_pos_v, cmp_w1_v, cmp_w2_v)
    qs, ocmp = _cmpsel(seq3(nq), kcmp, vcmp)
    nsa = _nsa(qs, seq3(ksl), seq3(vsl), seq3(kwn), seq3(vwn), ocmp, seq3(gates))

    out = _tail(x2, fox.reshape(n_tok, FOX_WIDTH), nsa.reshape(n_tok, NSA_WIDTH), sga, sgb,
                w_fox_out.astype(BF16), w_nsa_out.astype(BF16), w_o.astype(BF16),
                w_up.astype(BF16), w_down.astype(BF16),
                norm_mix_post.reshape(1, d), norm_mlp_pre.reshape(1, d), norm_mlp_post.reshape(1, d))
    return out.reshape(b, seq, d)


def kernel(x, norm_mix_pre, norm_mix_post, norm_mlp_pre, norm_mlp_post, w_in, b_forget, cmp_pos_k, cmp_w1_k, cmp_w2_k, cmp_pos_v, cmp_w1_v, cmp_w2_v, w_fox_out, w_nsa_out, w_o, w_up, w_down):
    params = (norm_mix_pre, norm_mix_post, norm_mlp_pre, norm_mlp_post, w_in, b_forget,
              cmp_pos_k, cmp_w1_k, cmp_w2_k, cmp_pos_v, cmp_w1_v, cmp_w2_v,
              w_fox_out, w_nsa_out, w_o, w_up, w_down)
    for layer in range(w_in.shape[0]):
        x = _layer(x, *[p[layer] for p in params])
    return x
```

```python
import functools

import numpy as np
import jax
import jax.numpy as jnp
from jax import lax
from jax.experimental import pallas as pl
from jax.experimental.pallas import tpu as pltpu

HEAD_DIM = 64
FOX_HEADS = 8
NSA_HEADS = 8
NSA_KV_GROUPS = 2
NSA_HEADS_PER_GROUP = NSA_HEADS // NSA_KV_GROUPS
FOX_WIDTH = FOX_HEADS * HEAD_DIM
NSA_WIDTH = NSA_HEADS * HEAD_DIM
NSA_KV_WIDTH = NSA_KV_GROUPS * HEAD_DIM
ROPE_THETA = 10000.0
CMP_BLOCK = 32
CMP_STRIDE = 16
SEL_BLOCK = 64
SEL_TOPN = 8
WINDOW = 512
RMS_EPS = 1e-6
NEG_INF = -1e30
FORCED_BONUS = 1e4
ATTN_SCALE = HEAD_DIM ** -0.5
LOG2E = 1.4426950408889634
Q_SCALE = ATTN_SCALE * LOG2E

LANES = 128
VMEM_LIMIT_BYTES = 52 * 1024 * 1024
SEL_LANE0 = HEAD_DIM
MAX_SEL_BLOCKS = 32

F32 = jnp.float32
BF16 = jnp.bfloat16
HIGHEST = lax.Precision.HIGHEST

PROJ_TM = 512
FOX_TQ = 256
CMP_TQ = 256
NSA_TQ = 256
TAIL_TM = 256
FF_CHUNK = 1024
KV_STEP = 512
BIAS_LANE0 = HEAD_DIM


def _nt_dot(a, b):
    return lax.dot_general(a, b, (((1,), (1,)), ((), ())), preferred_element_type=F32)


def _sigmoid(t):
    return 1.0 / (1.0 + jnp.exp(-t))


def _resident(shape):
    nd = len(shape)
    return pl.BlockSpec(shape, lambda *_: (0,) * nd, pipeline_mode=pl.Buffered(1))


def _proj_layout():
    widths = [('fq', FOX_WIDTH), ('fk', FOX_WIDTH), ('fv', FOX_WIDTH),
              ('nq', NSA_WIDTH),
              ('ksl', NSA_KV_WIDTH), ('vsl', NSA_KV_WIDTH), ('kwn', NSA_KV_WIDTH), ('vwn', NSA_KV_WIDTH),
              ('kc', NSA_KV_WIDTH), ('vc', NSA_KV_WIDTH),
              ('misc', LANES)]
    out, c = {}, 0
    for name, w in widths:
        out[name] = (c, w)
        c += w
    return out, c


def _rearrange_w_in(w_in, d_model):
    splits = (FOX_WIDTH, FOX_WIDTH, FOX_WIDTH, FOX_HEADS,
              NSA_WIDTH, NSA_KV_WIDTH, NSA_KV_WIDTH, NSA_KV_WIDTH, NSA_KV_WIDTH, NSA_KV_WIDTH, NSA_KV_WIDTH,
              3 * NSA_HEADS, d_model, d_model)
    pts = [int(p) for p in np.cumsum(splits)[:-1]]
    fq, fk, fv, ff, nq, kc, vc, ksl, vsl, kwn, vwn, ng, ga, gb = jnp.split(w_in, pts, axis=-1)
    d = w_in.shape[0]
    misc = jnp.concatenate([ff, ng, jnp.zeros((d, LANES - FOX_HEADS - 3 * NSA_HEADS), w_in.dtype)], axis=-1)
    w_small = jnp.concatenate([fq, fk, fv, nq, ksl, vsl, kwn, vwn, kc, vc, misc], axis=-1)
    return w_small.astype(BF16), ga.astype(BF16), gb.astype(BF16)


def _rope_tables(seq):
    half = HEAD_DIM // 2
    inv = jnp.power(jnp.float32(ROPE_THETA), -jnp.arange(0, HEAD_DIM, 2, dtype=F32) / HEAD_DIM)
    ang = jnp.arange(seq, dtype=F32)[:, None] * inv[None, :]
    cos, sin = jnp.cos(ang), jnp.sin(ang)
    zero = jnp.zeros_like(sin)
    cos_t = jnp.concatenate([cos, cos, cos, cos], axis=-1)
    s_lo = jnp.concatenate([-sin, zero, -sin, zero], axis=-1)
    s_hi = jnp.concatenate([zero, sin, zero, sin], axis=-1)
    assert cos_t.shape == (seq, 4 * half)
    return cos_t, s_lo, s_hi


def _proj_kernel(x_ref, g_ref, w_ref, wga_ref, wgb_ref, cos_ref, slo_ref, shi_ref, bmisc_ref,
                 fq_ref, fk_ref, fv_ref, nq_ref, ksl_ref, vsl_ref, kwn_ref, vwn_ref,
                 kc_ref, vc_ref, misc_ref, gates_ref, ga_ref, gb_ref, *, tm, seq, layout):
    x = x_ref[...]
    h = x * lax.rsqrt(jnp.mean(x * x, axis=-1, keepdims=True) + RMS_EPS) * g_ref[...]
    h = h.astype(BF16)
    cos, slo, shi = cos_ref[...], slo_ref[...], shi_ref[...]
    half = HEAD_DIM // 2

    def mm(name):
        c0, n = layout[name]
        return jnp.dot(h, w_ref[:, c0:c0 + n], preferred_element_type=F32)

    def rope_block(t):
        return t * cos + pltpu.roll(t, LANES - half, axis=1) * slo + pltpu.roll(t, half, axis=1) * shi

    def rope(t):
        n = t.shape[1] // LANES
        return [rope_block(t[:, b * LANES:(b + 1) * LANES]) for b in range(n)]

    lane = lax.broadcasted_iota(jnp.int32, (tm, LANES), 1)
    left = lane < HEAD_DIM

    def blocks(t):
        return [t[:, b * LANES:(b + 1) * LANES] for b in range(t.shape[1] // LANES)]

    def split_heads(pairs, o_ref, scale, fill=0.0):
        for p, blk in enumerate(pairs):
            blk = blk * scale
            o_ref[:, (2 * p) * LANES:(2 * p + 1) * LANES] = jnp.where(left, blk, fill).astype(BF16)
            o_ref[:, (2 * p + 1) * LANES:(2 * p + 2) * LANES] = jnp.where(
                left, pltpu.roll(blk, HEAD_DIM, axis=1), fill).astype(BF16)

    def dup_heads(pairs, o_ref):
        for p, blk in enumerate(pairs):
            swapped = pltpu.roll(blk, HEAD_DIM, axis=1)
            o_ref[:, (2 * p) * LANES:(2 * p + 1) * LANES] = jnp.where(left, blk, swapped).astype(BF16)
            o_ref[:, (2 * p + 1) * LANES:(2 * p + 2) * LANES] = jnp.where(left, swapped, blk).astype(BF16)

    split_heads(blocks(mm('fq')), fq_ref, Q_SCALE)
    split_heads(blocks(mm('fk')), fk_ref, 1.0)
    fv = mm('fv')
    for p in range(FOX_HEADS // 2):
        blk = fv[:, p * LANES:(p + 1) * LANES]
        fv_ref[:, (2 * p) * LANES:(2 * p + 1) * LANES] = jnp.where(
            left, blk, (lane == HEAD_DIM).astype(F32)).astype(BF16)
        fv_ref[:, (2 * p + 1) * LANES:(2 * p + 2) * LANES] = jnp.where(
            left, (lane == 0).astype(F32), blk).astype(BF16)

    split_heads(rope(mm('nq')), nq_ref, Q_SCALE)

    pos = (pl.program_id(0) % (seq // tm)) * tm + lax.broadcasted_iota(jnp.int32, (tm, LANES), 0)
    onehot = ((lane - SEL_LANE0) == (pos // SEL_BLOCK)).astype(F32)
    split_heads(rope(mm('ksl')), ksl_ref, 1.0, fill=onehot)
    dup_heads(blocks(mm('vsl')), vsl_ref)
    split_heads(rope(mm('kwn')), kwn_ref, 1.0)
    dup_heads(blocks(mm('vwn')), vwn_ref)

    kc_ref[...] = rope(mm('kc'))[0]
    vc_ref[...] = mm('vc')

    t = mm('misc') + bmisc_ref[...]
    misc_ref[...] = -(jnp.maximum(-t, 0.0) + jnp.log(1.0 + jnp.exp(-jnp.abs(t))))
    per_group = 3 * NSA_HEADS_PER_GROUP
    for g in range(NSA_KV_GROUPS):
        gates_ref[:, g * LANES:(g + 1) * LANES] = _sigmoid(
            pltpu.roll(t, LANES - (FOX_HEADS + g * per_group), axis=1))

    ga_ref[...] = _sigmoid(jnp.dot(h, wga_ref[...], preferred_element_type=F32)).astype(BF16)
    gb_ref[...] = _sigmoid(jnp.dot(h, wgb_ref[...], preferred_element_type=F32)).astype(BF16)


def _proj(x2, g, w_small, w_ga, w_gb, tables, bmisc, seq):
    n_tok, d = x2.shape
    tm = PROJ_TM
    layout, n_cols = _proj_layout()
    assert n_tok % tm == 0 and seq % tm == 0 and w_small.shape == (d, n_cols)
    tiles_per_seq = seq // tm

    def row(w):
        return pl.BlockSpec((tm, w), lambda i: (i, 0))

    def tab():
        return pl.BlockSpec((tm, LANES), lambda i: (i % tiles_per_seq, 0))

    outs = [('fq', BF16), ('fk', BF16), ('fv', BF16), ('nq', BF16), ('ksl', BF16), ('vsl', BF16),
            ('kwn', BF16), ('vwn', BF16), ('kc', F32), ('vc', F32), ('misc', F32), ('gates', F32)]
    widths = dict(kc=NSA_KV_WIDTH, vc=NSA_KV_WIDTH, misc=LANES, gates=NSA_KV_GROUPS * LANES)
    widths.update({n: FOX_HEADS * LANES for n in ('fq', 'fk', 'fv')})
    widths.update(nq=NSA_HEADS * LANES, **{n: NSA_KV_GROUPS * LANES for n in ('ksl', 'vsl', 'kwn', 'vwn')})
    out_shape = [jax.ShapeDtypeStruct((n_tok, widths[n]), dt) for n, dt in outs]
    out_specs = [row(widths[n]) for n, _ in outs]
    out_shape += [jax.ShapeDtypeStruct((n_tok, d), BF16)] * 2
    out_specs += [row(d), row(d)]
    return pl.pallas_call(
        functools.partial(_proj_kernel, tm=tm, seq=seq, layout=layout),
        grid=(n_tok // tm,),
        in_specs=[row(d), _resident((1, d)), _resident(w_small.shape), _resident(w_ga.shape),
                  _resident(w_gb.shape), tab(), tab(), tab(), _resident((1, LANES))],
        out_specs=out_specs,
        out_shape=out_shape,
        compiler_params=pltpu.CompilerParams(dimension_semantics=("arbitrary",),
                                             vmem_limit_bytes=VMEM_LIMIT_BYTES),
        name="proj",
    )(x2, g, w_small, w_ga, w_gb, *tables, bmisc)


def _cumsum_kernel(m_ref, ccol_ref, *, seq):
    r = lax.broadcasted_iota(jnp.int32, (LANES, LANES), 0)
    c = lax.broadcasted_iota(jnp.int32, (LANES, LANES), 1)
    tri = (r >= c).astype(F32)
    carry = jnp.zeros((1, LANES), F32)
    for b in range(seq // LANES):
        blk = m_ref[0, b * LANES:(b + 1) * LANES, :]
        cs = jnp.dot(tri, blk, precision=HIGHEST, preferred_element_type=F32) + carry
        ccol_ref[0, b * LANES:(b + 1) * LANES, :] = cs
        carry = cs[LANES - 1:LANES, :]


def _cumsum(misc3):
    b, seq, _ = misc3.shape
    assert seq % LANES == 0
    return pl.pallas_call(
        functools.partial(_cumsum_kernel, seq=seq),
        grid=(b,),
        in_specs=[pl.BlockSpec((1, seq, LANES), lambda i: (i, 0, 0))],
        out_specs=pl.BlockSpec((1, seq, LANES), lambda i: (i, 0, 0)),
        out_shape=jax.ShapeDtypeStruct((b, seq, LANES), F32),
        compiler_params=pltpu.CompilerParams(dimension_semantics=("arbitrary",),
                                             vmem_limit_bytes=VMEM_LIMIT_BYTES),
        name="cumsum",
    )(misc3)


def _softmax_step(s, pv, m_ref, acc_ref, rows):
    m_prev = m_ref[rows]
    m_new = jnp.maximum(m_prev, jnp.max(s, axis=-1, keepdims=True))
    p = jnp.exp2(s - m_new).astype(BF16)
    acc_ref[rows] = jnp.exp2(m_prev - m_new) * acc_ref[rows] + pv(p)
    m_ref[rows] = m_new


def _mask_table(tq):
    d = np.arange(tq)[:, None] - np.arange(tq)[None, :]
    neg = np.full((tq, tq), NEG_INF, np.float32)
    zero = np.zeros((tq, tq), np.float32)
    return jnp.asarray(np.stack([np.where(d > 0, zero, neg), zero, np.where(d <= 0, zero, neg), neg]))


def _mask_bias(tab_ref, key_tile0, n_tiles, q_tile, reps):
    rows = []
    for u in range(n_tiles):
        t = tab_ref[jnp.clip(key_tile0 + u - q_tile + 2, 0, 3)]
        rows.append(jnp.concatenate([t] * reps, axis=1) if reps > 1 else t)
    return jnp.concatenate(rows, axis=0) if n_tiles > 1 else rows[0]


def _split3(c):
    hi = c.astype(BF16).astype(F32)
    r = c - hi
    mid = r.astype(BF16).astype(F32)
    lo = (r - mid).astype(BF16).astype(F32)
    return hi, mid, lo


def _softmax_step_t(s, pv, m_ref, acc_ref, idx):
    m_prev = m_ref[idx]
    m_new = jnp.maximum(m_prev, jnp.max(s, axis=0, keepdims=True))
    p = jnp.exp2(s - m_new).astype(BF16)
    acc_ref[idx] = jnp.exp2(m_prev - m_new) * acc_ref[idx] + pv(p)
    m_ref[idx] = m_new


def _fox_kernel(q_ref, k_ref, v_ref, ccol_ref, o_ref, kx_s, vt_s, qx_s, m_s, acc_s, *, tq, tk, seq):
    i = pl.program_id(1)
    lane = lax.broadcasted_iota(jnp.int32, (tq, LANES), 1)
    left = lane < HEAD_DIM
    b0 = BIAS_LANE0

    @pl.when(i == 0)
    def _build_keys():
        def chunk(r, carry):
            rows = pl.ds(pl.multiple_of(r * tq, tq), tq)
            his, mids, los = _split3(ccol_ref[0, rows, :] * LOG2E)
            for h in range(FOX_HEADS):
                cols = slice(h * LANES, (h + 1) * LANES)
                hi, mid, lo = his[:, h:h + 1], mids[:, h:h + 1], los[:, h:h + 1]
                e = jnp.where(lane < b0 + 3, 1.0,
                              jnp.where(lane == b0 + 3, -hi,
                                        jnp.where(lane == b0 + 4, -mid, jnp.where(lane == b0 + 5, -lo, 0.0))))
                kx_s[h, rows, :] = jnp.where(left, k_ref[0, rows, cols].astype(F32), e).astype(BF16)
            return carry

        lax.fori_loop(0, seq // tq, chunk, 0)

        def vchunk(c, carry):
            rows = pl.ds(pl.multiple_of(c * tk, tk), tk)
            for h in range(FOX_HEADS):
                cols = slice(h * LANES, (h + 1) * LANES)
                vt_s[c, h] = v_ref[0, rows, cols].astype(F32).T.astype(BF16)
            return carry

        lax.fori_loop(0, seq // tk, vchunk, 0)

    qrows = pl.ds(pl.multiple_of(i * tq, tq), tq)
    n_full = (i * tq) // tk
    q_pos = i * tq + lax.broadcasted_iota(jnp.int32, (tk, tq), 1)
    k_off = lax.broadcasted_iota(jnp.int32, (tk, tq), 0)
    row = lax.broadcasted_iota(jnp.int32, (LANES, tq), 0)
    n_pairs = FOX_HEADS // 2
    his, mids, los = _split3(ccol_ref[0, qrows, :] * LOG2E)
    for h in range(FOX_HEADS):
        hi, mid, lo = his[:, h:h + 1], mids[:, h:h + 1], los[:, h:h + 1]
        e = jnp.where(lane == b0, hi,
                      jnp.where(lane == b0 + 1, mid,
                                jnp.where(lane == b0 + 2, lo, jnp.where(lane < b0 + 6, 1.0, 0.0))))
        qx_s[h] = jnp.where(left, q_ref[0, :, h * LANES:(h + 1) * LANES].astype(F32), e).astype(BF16)
    m_s[...] = jnp.full(m_s.shape, NEG_INF, F32)
    acc_s[...] = jnp.zeros(acc_s.shape, F32)

    def scores(j, p):
        rows = pl.ds(pl.multiple_of(j * tk, tk), tk)
        return tuple(_nt_dot(kx_s[2 * p + slot, rows, :], qx_s[2 * p + slot]) for slot in range(2))

    def consume(j, p, ss, masked):
        for slot in range(2):
            h = 2 * p + slot
            s = ss[slot]
            if masked:
                s = jnp.where(j * tk + k_off <= q_pos, s, NEG_INF)
            _softmax_step_t(s, lambda pr, h=h: jnp.dot(vt_s[j, h], pr, preferred_element_type=F32), m_s, acc_s, h)

    def key_step(j, ss, masked, last):
        for p in range(n_pairs):
            if p + 1 < n_pairs:
                nxt = scores(j, p + 1)
            else:
                nxt = None if last else scores(j + 1, 0)
            consume(j, p, ss, masked)
            ss = nxt
        return ss

    ss = lax.fori_loop(0, n_full, lambda j, ss: key_step(j, ss, False, False), scores(0, 0))
    key_step(n_full, ss, True, True)

    for hp in range(n_pairs):
        a0, a1 = acc_s[2 * hp], acc_s[2 * hp + 1]
        ot = jnp.where(row < HEAD_DIM, a0 / a0[HEAD_DIM:HEAD_DIM + 1, :], a1 / a1[0:1, :])
        o_ref[0, :, hp * LANES:(hp + 1) * LANES] = ot.T.astype(BF16)


def _fox(fq, fk, fv, ccol):
    b, seq, w = fq.shape
    tq, tk = FOX_TQ, KV_STEP
    assert seq % tq == 0 and seq % tk == 0 and tk % tq == 0 and w == FOX_HEADS * LANES
    whole = pl.BlockSpec((1, seq, w), lambda bi, i: (bi, 0, 0))
    return pl.pallas_call(
        functools.partial(_fox_kernel, tq=tq, tk=tk, seq=seq),
        grid=(b, seq // tq),
        in_specs=[pl.BlockSpec((1, tq, w), lambda bi, i: (bi, i, 0)), whole, whole,
                  pl.BlockSpec((1, seq, LANES), lambda bi, i: (bi, 0, 0))],
        out_specs=pl.BlockSpec((1, tq, FOX_WIDTH), lambda bi, i: (bi, i, 0)),
        out_shape=jax.ShapeDtypeStruct((b, seq, FOX_WIDTH), BF16),
        scratch_shapes=[pltpu.VMEM((FOX_HEADS, seq, LANES), BF16),
                        pltpu.VMEM((seq // tk, FOX_HEADS, LANES, tk), BF16),
                        pltpu.VMEM((FOX_HEADS, tq, LANES), BF16),
                        pltpu.VMEM((FOX_HEADS, 1, tq), F32), pltpu.VMEM((FOX_HEADS, LANES, tq), F32)],
        compiler_params=pltpu.CompilerParams(dimension_semantics=("arbitrary", "arbitrary"),
                                             vmem_limit_bytes=VMEM_LIMIT_BYTES),
        name="fox",
    )(fq, fk, fv, ccol)


def _compress_kernel(yk_ref, yv_ref, posk_ref, w1k_ref, w2k_ref, posv_ref, w1v_ref, w2v_ref,
                     kc_ref, vc_ref, *, nb):
    half = CMP_STRIDE * HEAD_DIM

    lane = lax.broadcasted_iota(jnp.int32, (nb, LANES), 1)
    left = lane < HEAD_DIM

    def regroup(x_ref):
        y = [[], []]
        for j in range(CMP_STRIDE // 2):
            xe = x_ref[0, pl.ds(2 * j, nb, stride=CMP_STRIDE), :]
            xo = x_ref[0, pl.ds(2 * j + 1, nb, stride=CMP_STRIDE), :]
            y[0].append(jnp.where(left, xe, pltpu.roll(xo, HEAD_DIM, axis=1)))
            y[1].append(jnp.where(left, pltpu.roll(xe, HEAD_DIM, axis=1), xo))
        return [jnp.concatenate(t, axis=1) for t in y]

    def comp(y, pos_ref, w1_ref, w2_ref):
        pos = pos_ref[...]
        a = jnp.dot(y + pos[:, :half], w1_ref[0:half, :], precision=HIGHEST, preferred_element_type=F32)
        bm = jnp.dot(y + pos[:, half:], w1_ref[half:2 * half, :], precision=HIGHEST,
                     preferred_element_type=F32)
        pre = a + pltpu.roll(bm, nb - 1, axis=0)
        return jnp.dot(pre * _sigmoid(pre), w2_ref[...], precision=HIGHEST, preferred_element_type=F32)

    for g, (yk, yv) in enumerate(zip(regroup(yk_ref), regroup(yv_ref))):
        kc_ref[0, g] = comp(yk, posk_ref, w1k_ref, w2k_ref).astype(BF16)
        vc_ref[0, g] = comp(yv, posv_ref, w1v_ref, w2v_ref).astype(BF16)


def _compress(kc, vc, b, seq, pos_k, w1_k, w2_k, pos_v, w1_v, w2_v):
    g = NSA_KV_GROUPS
    nb = seq // CMP_STRIDE
    half = CMP_STRIDE * HEAD_DIM
    assert CMP_BLOCK == 2 * CMP_STRIDE and seq % CMP_STRIDE == 0 and nb % 8 == 0
    assert g == 2 and kc.shape[-1] == LANES

    zeros1 = jnp.zeros((CMP_BLOCK * HEAD_DIM, HEAD_DIM), F32)
    zeros2 = jnp.zeros((HEAD_DIM, HEAD_DIM), F32)
    w1k = jnp.concatenate([w1_k, zeros1], axis=-1)
    w1v = jnp.concatenate([w1_v, zeros1], axis=-1)
    w2k = jnp.concatenate([jnp.concatenate([w2_k, zeros2], axis=-1), jnp.zeros((HEAD_DIM, LANES), F32)], axis=0)
    w2v = jnp.concatenate([jnp.concatenate([w2_v, w2_v], axis=-1), jnp.zeros((HEAD_DIM, LANES), F32)], axis=0)
    yspec = pl.BlockSpec((1, seq, LANES), lambda bi: (bi, 0, 0))
    ospec = pl.BlockSpec((1, g, nb, LANES), lambda bi: (bi, 0, 0, 0))
    return pl.pallas_call(
        functools.partial(_compress_kernel, nb=nb),
        grid=(b,),
        in_specs=[yspec, yspec,
                  _resident((1, 2 * half)), _resident(w1k.shape), _resident(w2k.shape),
                  _resident((1, 2 * half)), _resident(w1v.shape), _resident(w2v.shape)],
        out_specs=[ospec, ospec],
        out_shape=[jax.ShapeDtypeStruct((b, g, nb, LANES), BF16)] * 2,
        compiler_params=pltpu.CompilerParams(dimension_semantics=("arbitrary",),
                                             vmem_limit_bytes=VMEM_LIMIT_BYTES),
        name="compress",
    )(kc.reshape(b, seq, LANES), vc.reshape(b, seq, LANES), pos_k.reshape(1, 2 * half), w1k, w2k,
      pos_v.reshape(1, 2 * half), w1v, w2v)


def _overlap_matrix(nb, n_cmp, n_sel):
    ci = np.arange(nb)[:, None] * CMP_STRIDE
    sj = (np.arange(LANES)[None, :] - SEL_LANE0) * SEL_BLOCK
    jj = np.arange(LANES)[None, :] - SEL_LANE0
    ov = (ci < sj + SEL_BLOCK) & (ci + CMP_BLOCK > sj) & (jj >= 0) & (jj < n_sel) & (np.arange(nb)[:, None] < n_cmp)
    return jnp.asarray(ov.astype(np.float32))


def _cmpsel_kernel(q_ref, kc_ref, vc_ref, ovt_ref, qs_ref, ocmp_ref, vct_s, *, tq, n_cmp, nb):
    i = pl.program_id(2)

    @pl.when(i == 0)
    def _transpose_values():
        vct_s[...] = vc_ref[0, 0].astype(F32).T.astype(BF16)

    n_idx = lax.broadcasted_iota(jnp.int32, (nb, tq), 0)
    t_pos = i * tq + lax.broadcasted_iota(jnp.int32, (nb, tq), 1)
    cmask = (n_idx * CMP_STRIDE + (CMP_BLOCK - 1) <= t_pos) & (n_idx < n_cmp)
    top = lax.broadcasted_iota(jnp.int32, (LANES, tq), 0) < HEAD_DIM
    kc = kc_ref[0, 0]
    vct = vct_s[...]
    psum = jnp.zeros((nb, tq), F32)
    outs = []
    for h in range(NSA_HEADS_PER_GROUP):
        s = jnp.where(cmask, _nt_dot(kc, q_ref[0, :, h * LANES:(h + 1) * LANES]), NEG_INF)
        m = jnp.max(s, axis=0, keepdims=True)
        p = jnp.where(cmask, jnp.exp2(s - m), 0.0)
        p = p / jnp.maximum(jnp.sum(p, axis=0, keepdims=True), 1e-30)
        psum = psum + p
        outs.append(jnp.dot(vct, p.astype(BF16), preferred_element_type=F32))
    for hp in range(NSA_HEADS_PER_GROUP // 2):
        ocmp_ref[0, :, hp * LANES:(hp + 1) * LANES] = jnp.where(top, outs[2 * hp], outs[2 * hp + 1]).T

    p_slc = jnp.dot(ovt_ref[...], psum, precision=HIGHEST, preferred_element_type=F32)
    sc = p_slc[SEL_LANE0:SEL_LANE0 + MAX_SEL_BLOCKS, :]
    j = lax.broadcasted_iota(jnp.int32, (MAX_SEL_BLOCKS, tq), 0)
    cur = (i * tq + lax.broadcasted_iota(jnp.int32, (MAX_SEL_BLOCKS, tq), 1)) // SEL_BLOCK
    forced = (j == 0) | (j == cur) | (j == cur - 1)
    score = jnp.where(j <= cur, sc + FORCED_BONUS * forced.astype(F32), NEG_INF)
    j_f = j.astype(F32)
    sel = jnp.zeros((MAX_SEL_BLOCKS, tq), jnp.bool_)
    for _ in range(SEL_TOPN):
        mx = jnp.max(score, axis=0, keepdims=True)
        first = jnp.min(jnp.where(score == mx, j_f, 1e4), axis=0, keepdims=True)
        chosen = j_f == first
        sel = sel | chosen
        score = jnp.where(chosen, -jnp.inf, score)
    bias_t = jnp.concatenate([jnp.zeros((SEL_LANE0, tq), F32), jnp.where(sel, 0.0, NEG_INF),
                              jnp.zeros((LANES - SEL_LANE0 - MAX_SEL_BLOCKS, tq), F32)], axis=0)
    bias = bias_t.T
    for h in range(NSA_HEADS_PER_GROUP):
        cols = slice(h * LANES, (h + 1) * LANES)
        qs_ref[0, :, cols] = (q_ref[0, :, cols].astype(F32) + bias).astype(BF16)


def _cmpsel(nq3, kcmp, vcmp):
    b, seq, _ = nq3.shape
    g, hg = NSA_KV_GROUPS, NSA_HEADS_PER_GROUP
    tq = CMP_TQ
    nb = kcmp.shape[2]
    n_cmp = (seq - CMP_BLOCK) // CMP_STRIDE + 1
    n_sel = seq // SEL_BLOCK
    assert seq % tq == 0 and n_sel <= MAX_SEL_BLOCKS and n_sel >= SEL_TOPN
    ov = _overlap_matrix(nb, n_cmp, n_sel).T
    qspec = pl.BlockSpec((1, tq, hg * LANES), lambda bi, gi, i: (bi, i, gi))
    cspec = pl.BlockSpec((1, 1, nb, LANES), lambda bi, gi, i: (bi, gi, 0, 0))
    return pl.pallas_call(
        functools.partial(_cmpsel_kernel, tq=tq, n_cmp=n_cmp, nb=nb),
        grid=(b, g, seq // tq),
        in_specs=[qspec, cspec, cspec, _resident(ov.shape)],
        out_specs=[qspec, pl.BlockSpec((1, tq, hg * HEAD_DIM), lambda bi, gi, i: (bi, i, gi))],
        out_shape=[jax.ShapeDtypeStruct(nq3.shape, BF16),
                   jax.ShapeDtypeStruct((b, seq, NSA_WIDTH), F32)],
        scratch_shapes=[pltpu.VMEM((LANES, nb), BF16)],
        compiler_params=pltpu.CompilerParams(dimension_semantics=("arbitrary",) * 3,
                                             vmem_limit_bytes=VMEM_LIMIT_BYTES),
        name="cmpsel",
    )(nq3, kcmp, vcmp, ov)


def _nsa_kernel(qs_ref, ks_ref, vs_ref, kw_ref, vw_ref, ocmp_ref, gates_ref, tab_ref, o_ref,
                vse_s, vso_s, vwe_s, vwo_s, m_s, acc_s, *, tq, tk, seq):
    i = pl.program_id(1)
    n_g, hg = NSA_KV_GROUPS, NSA_HEADS_PER_GROUP
    rows_all = hg * tq
    half = rows_all // 2
    wt = WINDOW // tq + 1

    @pl.when(i == 0)
    def _build_values():
        lane = lax.broadcasted_iota(jnp.int32, (tk, LANES), 1)
        one_e = (lane == HEAD_DIM).astype(F32)
        one_o = (lane == 0).astype(F32)

        def chunk(c, carry):
            rows = pl.ds(pl.multiple_of(c * tk, tk), tk)
            for g in range(n_g):
                cols = slice(g * LANES, (g + 1) * LANES)
                v = vs_ref[0, rows, cols].astype(F32)
                vse_s[g, c] = jnp.where(lane < HEAD_DIM, v, one_e).T.astype(BF16)
                vso_s[g, c] = jnp.where(lane < HEAD_DIM, one_o, v).T.astype(BF16)
                w = vw_ref[0, rows, cols].astype(F32)
                we = jnp.where(lane < HEAD_DIM, w, one_e).T.astype(BF16)
                wo = jnp.where(lane < HEAD_DIM, one_o, w).T.astype(BF16)
                for t in range(tk // tq):
                    vwe_s[g, c * (tk // tq) + t] = we[:, t * tq:(t + 1) * tq]
                    vwo_s[g, c * (tk // tq) + t] = wo[:, t * tq:(t + 1) * tq]
            return carry

        lax.fori_loop(0, seq // tk, chunk, 0)

    order = tuple(range(0, hg, 2)) + tuple(range(1, hg, 2))
    q = [jnp.concatenate([qs_ref[0, :, (g * hg + h) * LANES:(g * hg + h + 1) * LANES] for h in order], axis=0)
         for g in range(n_g)]

    n_full = (i * tq) // tk
    m_s[...] = jnp.full(m_s.shape, NEG_INF, F32)
    acc_s[...] = jnp.zeros(acc_s.shape, F32)

    def scores(j, g):
        rows = pl.ds(pl.multiple_of(j * tk, tk), tk)
        return _nt_dot(ks_ref[0, rows, g * LANES:(g + 1) * LANES], q[g])

    def consume(j, g, s, masked):
        if masked:
            s = s + _mask_bias(tab_ref, j * (tk // tq), tk // tq, i, hg)
        _softmax_step_t(s, lambda p: jnp.concatenate(
            [jnp.dot(vse_s[g, j], p[:, :half], preferred_element_type=F32),
             jnp.dot(vso_s[g, j], p[:, half:], preferred_element_type=F32)], axis=1), m_s, acc_s, g)

    st = jnp.maximum(i - (wt - 1), 0)

    def win_scores(g):
        wrows = pl.ds(pl.multiple_of(st * tq, tq), wt * tq)
        return _nt_dot(kw_ref[0, wrows, g * LANES:(g + 1) * LANES], q[g])

    def win_consume(g, s_win):
        s_win = s_win + _mask_bias(tab_ref, st, wt, i, hg)
        p = jnp.exp2(s_win - jnp.max(s_win, axis=0, keepdims=True)).astype(BF16)
        a_win = jnp.zeros((LANES, rows_all), F32)
        for t in range(wt):
            pt = p[t * tq:(t + 1) * tq]
            a_win = a_win + jnp.concatenate(
                [jnp.dot(vwe_s[g, st + t], pt[:, :half], preferred_element_type=F32),
                 jnp.dot(vwo_s[g, st + t], pt[:, half:], preferred_element_type=F32)], axis=1)
        return a_win

    def key_step(j, s):
        for g in range(n_g):
            nxt = scores(j, g + 1) if g + 1 < n_g else scores(j + 1, 0)
            consume(j, g, s, False)
            s = nxt
        return s

    s = lax.fori_loop(0, n_full, key_step, scores(0, 0))
    for g in range(n_g):
        nxt = scores(n_full, g + 1) if g + 1 < n_g else win_scores(0)
        consume(n_full, g, s, True)
        s = nxt
    a_win = []
    for g in range(n_g):
        nxt = win_scores(g + 1) if g + 1 < n_g else None
        a_win.append(win_consume(g, s))
        s = nxt

    lane = lax.broadcasted_iota(jnp.int32, (tq, LANES), 1)
    left = lane < HEAD_DIM
    top = lax.broadcasted_iota(jnp.int32, (LANES, tq), 0) < HEAD_DIM
    for g in range(n_g):
        gates = gates_ref[0, :, g * LANES:(g + 1) * LANES]
        a_sel = acc_s[g]
        for hp in range(hg // 2):
            h0, h1 = 2 * hp, 2 * hp + 1
            cols = slice((g * (hg // 2) + hp) * LANES, (g * (hg // 2) + hp + 1) * LANES)

            def pair(a):
                ae = a[:, hp * tq:(hp + 1) * tq]
                ao = a[:, half + hp * tq:half + (hp + 1) * tq]
                return jnp.where(top, ae / jnp.maximum(ae[HEAD_DIM:HEAD_DIM + 1, :], 1e-30),
                                 ao / jnp.maximum(ao[0:1, :], 1e-30)).T

            def gate(branch):
                return jnp.where(left, gates[:, 3 * h0 + branch:3 * h0 + branch + 1],
                                 gates[:, 3 * h1 + branch:3 * h1 + branch + 1])

            o = gate(0) * ocmp_ref[0, :, cols] + gate(1) * pair(a_sel) + gate(2) * pair(a_win[g])
            o_ref[0, :, cols] = o.astype(BF16)


def _nsa(qs, ksl, vsl, kwn, vwn, ocmp, gates):
    b, seq, _ = qs.shape
    g, hg = NSA_KV_GROUPS, NSA_HEADS_PER_GROUP
    tq, tk = NSA_TQ, KV_STEP
    assert seq % tk == 0 and tk % tq == 0 and WINDOW % tq == 0 and seq >= WINDOW + tq and tq & (tq - 1) == 0
    assert hg % 2 == 0
    assert WINDOW == 2 * tq
    kv = pl.BlockSpec((1, seq, g * LANES), lambda bi, i: (bi, 0, 0))
    ospec = pl.BlockSpec((1, tq, NSA_WIDTH), lambda bi, i: (bi, i, 0))
    return pl.pallas_call(
        functools.partial(_nsa_kernel, tq=tq, tk=tk, seq=seq),
        grid=(b, seq // tq),
        in_specs=[pl.BlockSpec((1, tq, g * hg * LANES), lambda bi, i: (bi, i, 0)), kv, kv, kv, kv,
                  ospec, pl.BlockSpec((1, tq, g * LANES), lambda bi, i: (bi, i, 0)), _resident((4, tq, tq))],
        out_specs=ospec,
        out_shape=jax.ShapeDtypeStruct((b, seq, NSA_WIDTH), BF16),
        scratch_shapes=[pltpu.VMEM((g, seq // tk, LANES, tk), BF16)] * 2
        + [pltpu.VMEM((g, seq // tq, LANES, tq), BF16)] * 2
        + [pltpu.VMEM((g, 1, hg * tq), F32), pltpu.VMEM((g, LANES, hg * tq), F32)],
        compiler_params=pltpu.CompilerParams(dimension_semantics=("arbitrary",) * 2,
                                             vmem_limit_bytes=VMEM_LIMIT_BYTES),
        name="nsa",
    )(qs, ksl, vsl, kwn, vwn, ocmp, gates, _mask_table(tq))


def _rms(y, g):
    return y * lax.rsqrt(jnp.mean(y * y, axis=-1, keepdims=True) + RMS_EPS) * g


def _tail_kernel(x_ref, fox_ref, nsa_ref, ga_ref, gb_ref, wf_ref, wn_ref, wo_ref, wup_ref, wdn_ref,
                 gmix_ref, gpre_ref, gpost_ref, o_ref, *, d_ff):
    a = jnp.dot(fox_ref[...], wf_ref[...], preferred_element_type=F32)
    b = jnp.dot(nsa_ref[...], wn_ref[...], preferred_element_type=F32)
    mix = ga_ref[...].astype(F32) * a + gb_ref[...].astype(F32) * b
    y = jnp.dot(mix.astype(BF16), wo_ref[...], preferred_element_type=F32)
    x1 = x_ref[...] + _rms(y, gmix_ref[...])
    h = _rms(x1, gpre_ref[...]).astype(BF16)
    acc = jnp.zeros(x1.shape, F32)
    for c in range(d_ff // FF_CHUNK):
        cols = slice(c * FF_CHUNK, (c + 1) * FF_CHUNK)
        u = jnp.maximum(jnp.dot(h, wup_ref[:, cols], preferred_element_type=F32), 0.0)
        acc = acc + jnp.dot((u * u).astype(BF16), wdn_ref[cols, :], preferred_element_type=F32)
    o_ref[...] = x1 + _rms(acc, gpost_ref[...])


def _tail(x2, fox, nsa, sga, sgb, wf, wn, wo, wup, wdn, g_mix_post, g_mlp_pre, g_mlp_post):
    n_tok, d = x2.shape
    d_ff = wup.shape[1]
    tm = TAIL_TM
    assert n_tok % tm == 0 and d_ff % FF_CHUNK == 0

    def row(w):
        return pl.BlockSpec((tm, w), lambda i: (i, 0))

    return pl.pallas_call(
        functools.partial(_tail_kernel, d_ff=d_ff),
        grid=(n_tok // tm,),
        in_specs=[row(d), row(fox.shape[1]), row(nsa.shape[1]), row(d), row(d),
                  _resident(wf.shape), _resident(wn.shape), _resident(wo.shape),
                  _resident(wup.shape), _resident(wdn.shape),
                  _resident((1, d)), _resident((1, d)), _resident((1, d))],
        out_specs=row(d),
        out_shape=jax.ShapeDtypeStruct((n_tok, d), F32),
        compiler_params=pltpu.CompilerParams(dimension_semantics=("arbitrary",),
                                             vmem_limit_bytes=VMEM_LIMIT_BYTES),
        name="tail",
    )(x2, fox, nsa, sga, sgb, wf, wn, wo, wup, wdn, g_mix_post, g_mlp_pre, g_mlp_post)


def _layer(x, norm_mix_pre, norm_mix_post, norm_mlp_pre, norm_mlp_post, w_in, b_forget,
           cmp_pos_k, cmp_w1_k, cmp_w2_k, cmp_pos_v, cmp_w1_v, cmp_w2_v,
           w_fox_out, w_nsa_out, w_o, w_up, w_down):
    b, seq, d = x.shape
    n_tok = b * seq
    x2 = x.reshape(n_tok, d)
    w_small, w_ga, w_gb = _rearrange_w_in(w_in, d)
    bmisc = jnp.concatenate([b_forget.astype(F32), jnp.zeros((LANES - FOX_HEADS,), F32)]).reshape(1, LANES)
    (fq, fk, fv, nq, ksl, vsl, kwn, vwn, kc, vc, misc, gates, sga, sgb) = _proj(
        x2, norm_mix_pre.reshape(1, d), w_small, w_ga, w_gb, _rope_tables(seq), bmisc, seq)

    def seq3(t):
        return t.reshape(b, seq, t.shape[-1])

    fox = _fox(seq3(fq), seq3(fk), seq3(fv), _cumsum(seq3(misc)))

    kcmp, vcmp = _compress(kc, vc, b, seq, cmp_pos_k, cmp_w1_k, cmp_w2_k, cmp_pos_v, cmp_w1_v, cmp_w2_v)
    qs, ocmp = _cmpsel(seq3(nq), kcmp, vcmp)
    nsa = _nsa(qs, seq3(ksl), seq3(vsl), seq3(kwn), seq3(vwn), ocmp, seq3(gates))

    out = _tail(x2, fox.reshape(n_tok, FOX_WIDTH), nsa.reshape(n_tok, NSA_WIDTH), sga, sgb,
                w_fox_out.astype(BF16), w_nsa_out.astype(BF16), w_o.astype(BF16),
                w_up.astype(BF16), w_down.astype(BF16),
                norm_mix_post.reshape(1, d), norm_mlp_pre.reshape(1, d), norm_mlp_post.reshape(1, d))
    return out.reshape(b, seq, d)


def kernel(x, norm_mix_pre, norm_mix_post, norm_mlp_pre, norm_mlp_post, w_in, b_forget, cmp_pos_k, cmp_w1_k, cmp_w2_k, cmp_pos_v, cmp_w1_v, cmp_w2_v, w_fox_out, w_nsa_out, w_o, w_up, w_down):
    params = (norm_mix_pre, norm_mix_post, norm_mlp_pre, norm_mlp_post, w_in, b_forget,
              cmp_pos_k, cmp_w1_k, cmp_w2_k, cmp_pos_v, cmp_w1_v, cmp_w2_v,
              w_fox_out, w_nsa_out, w_o, w_up, w_down)
    for layer in range(w_in.shape[0]):
        x = _layer(x, *[p[layer] for p in params])
    return x
```

```python
import functools

import numpy as np
import jax
import jax.numpy as jnp
from jax import lax
from jax.experimental import pallas as pl
from jax.experimental.pallas import tpu as pltpu

HEAD_DIM = 64
FOX_HEADS = 8
NSA_HEADS = 8
NSA_KV_GROUPS = 2
NSA_HEADS_PER_GROUP = NSA_HEADS // NSA_KV_GROUPS
FOX_WIDTH = FOX_HEADS * HEAD_DIM
NSA_WIDTH = NSA_HEADS * HEAD_DIM
NSA_KV_WIDTH = NSA_KV_GROUPS * HEAD_DIM
ROPE_THETA = 10000.0
CMP_BLOCK = 32
CMP_STRIDE = 16
SEL_BLOCK = 64
SEL_TOPN = 8
WINDOW = 512
RMS_EPS = 1e-6
NEG_INF = -1e30
FORCED_BONUS = 1e4
ATTN_SCALE = HEAD_DIM ** -0.5
LOG2E = 1.4426950408889634
Q_SCALE = ATTN_SCALE * LOG2E

LANES = 128
VMEM_LIMIT_BYTES = 52 * 1024 * 1024
SEL_LANE0 = HEAD_DIM
MAX_SEL_BLOCKS = 32

F32 = jnp.float32
BF16 = jnp.bfloat16
HIGHEST = lax.Precision.HIGHEST

PROJ_TM = 512
FOX_TQ = 256
CMP_TQ = 256
NSA_TQ = 256
TAIL_TM = 256
FF_CHUNK = 1024
KV_STEP = 512
BIAS_LANE0 = HEAD_DIM


def _nt_dot(a, b):
    return lax.dot_general(a, b, (((1,), (1,)), ((), ())), preferred_element_type=F32)


def _sigmoid(t):
    return 1.0 / (1.0 + jnp.exp(-t))


def _resident(shape):
    nd = len(shape)
    return pl.BlockSpec(shape, lambda *_: (0,) * nd, pipeline_mode=pl.Buffered(1))


def _proj_layout():
    widths = [('fq', FOX_WIDTH), ('fk', FOX_WIDTH), ('fv', FOX_WIDTH),
              ('nq', NSA_WIDTH),
              ('ksl', NSA_KV_WIDTH), ('vsl', NSA_KV_WIDTH), ('kwn', NSA_KV_WIDTH), ('vwn', NSA_KV_WIDTH),
              ('kc', NSA_KV_WIDTH), ('vc', NSA_KV_WIDTH),
              ('misc', LANES)]
    out, c = {}, 0
    for name, w in widths:
        out[name] = (c, w)
        c += w
    return out, c


def _rearrange_w_in(w_in, d_model):
    splits = (FOX_WIDTH, FOX_WIDTH, FOX_WIDTH, FOX_HEADS,
              NSA_WIDTH, NSA_KV_WIDTH, NSA_KV_WIDTH, NSA_KV_WIDTH, NSA_KV_WIDTH, NSA_KV_WIDTH, NSA_KV_WIDTH,
              3 * NSA_HEADS, d_model, d_model)
    pts = [int(p) for p in np.cumsum(splits)[:-1]]
    fq, fk, fv, ff, nq, kc, vc, ksl, vsl, kwn, vwn, ng, ga, gb = jnp.split(w_in, pts, axis=-1)
    d = w_in.shape[0]
    misc = jnp.concatenate([ff, ng, jnp.zeros((d, LANES - FOX_HEADS - 3 * NSA_HEADS), w_in.dtype)], axis=-1)
    w_small = jnp.concatenate([fq, fk, fv, nq, ksl, vsl, kwn, vwn, kc, vc, misc], axis=-1)
    return w_small.astype(BF16), ga.astype(BF16), gb.astype(BF16)


def _rope_tables(seq):
    half = HEAD_DIM // 2
    inv = jnp.power(jnp.float32(ROPE_THETA), -jnp.arange(0, HEAD_DIM, 2, dtype=F32) / HEAD_DIM)
    ang = jnp.arange(seq, dtype=F32)[:, None] * inv[None, :]
    cos, sin = jnp.cos(ang), jnp.sin(ang)
    zero = jnp.zeros_like(sin)
    cos_t = jnp.concatenate([cos, cos, cos, cos], axis=-1)
    s_lo = jnp.concatenate([-sin, zero, -sin, zero], axis=-1)
    s_hi = jnp.concatenate([zero, sin, zero, sin], axis=-1)
    assert cos_t.shape == (seq, 4 * half)
    return cos_t, s_lo, s_hi


def _proj_kernel(x_ref, g_ref, w_ref, wga_ref, wgb_ref, cos_ref, slo_ref, shi_ref, bmisc_ref,
                 fq_ref, fk_ref, fv_ref, nq_ref, ksl_ref, vsl_ref, kwn_ref, vwn_ref,
                 kc_ref, vc_ref, misc_ref, gates_ref, ga_ref, gb_ref, *, tm, seq, layout):
    x = x_ref[...]
    h = x * lax.rsqrt(jnp.mean(x * x, axis=-1, keepdims=True) + RMS_EPS) * g_ref[...]
    h = h.astype(BF16)
    cos, slo, shi = cos_ref[...], slo_ref[...], shi_ref[...]
    half = HEAD_DIM // 2

    def mm(name):
        c0, n = layout[name]
        return jnp.dot(h, w_ref[:, c0:c0 + n], preferred_element_type=F32)

    def rope_block(t):
        return t * cos + pltpu.roll(t, LANES - half, axis=1) * slo + pltpu.roll(t, half, axis=1) * shi

    def rope(t):
        n = t.shape[1] // LANES
        return [rope_block(t[:, b * LANES:(b + 1) * LANES]) for b in range(n)]

    lane = lax.broadcasted_iota(jnp.int32, (tm, LANES), 1)
    left = lane < HEAD_DIM

    def blocks(t):
        return [t[:, b * LANES:(b + 1) * LANES] for b in range(t.shape[1] // LANES)]

    def split_heads(pairs, o_ref, scale, fill=0.0):
        for p, blk in enumerate(pairs):
            blk = blk * scale
            o_ref[:, (2 * p) * LANES:(2 * p + 1) * LANES] = jnp.where(left, blk, fill).astype(BF16)
            o_ref[:, (2 * p + 1) * LANES:(2 * p + 2) * LANES] = jnp.where(
                left, pltpu.roll(blk, HEAD_DIM, axis=1), fill).astype(BF16)

    def dup_heads(pairs, o_ref):
        for p, blk in enumerate(pairs):
            swapped = pltpu.roll(blk, HEAD_DIM, axis=1)
            o_ref[:, (2 * p) * LANES:(2 * p + 1) * LANES] = jnp.where(left, blk, swapped).astype(BF16)
            o_ref[:, (2 * p + 1) * LANES:(2 * p + 2) * LANES] = jnp.where(left, swapped, blk).astype(BF16)

    split_heads(blocks(mm('fq')), fq_ref, Q_SCALE)
    split_heads(blocks(mm('fk')), fk_ref, 1.0)
    fv = mm('fv')
    for p in range(FOX_HEADS // 2):
        blk = fv[:, p * LANES:(p + 1) * LANES]
        fv_ref[:, (2 * p) * LANES:(2 * p + 1) * LANES] = jnp.where(
            left, blk, (lane == HEAD_DIM).astype(F32)).astype(BF16)
        fv_ref[:, (2 * p + 1) * LANES:(2 * p + 2) * LANES] = jnp.where(
            left, (lane == 0).astype(F32), blk).astype(BF16)

    split_heads(rope(mm('nq')), nq_ref, Q_SCALE)

    pos = (pl.program_id(0) % (seq // tm)) * tm + lax.broadcasted_iota(jnp.int32, (tm, LANES), 0)
    onehot = ((lane - SEL_LANE0) == (pos // SEL_BLOCK)).astype(F32)
    split_heads(rope(mm('ksl')), ksl_ref, 1.0, fill=onehot)
    dup_heads(blocks(mm('vsl')), vsl_ref)
    split_heads(rope(mm('kwn')), kwn_ref, 1.0)
    dup_heads(blocks(mm('vwn')), vwn_ref)

    kc_ref[...] = rope(mm('kc'))[0]
    vc_ref[...] = mm('vc')

    t = mm('misc') + bmisc_ref[...]
    misc_ref[...] = -(jnp.maximum(-t, 0.0) + jnp.log(1.0 + jnp.exp(-jnp.abs(t))))
    per_group = 3 * NSA_HEADS_PER_GROUP
    for g in range(NSA_KV_GROUPS):
        gates_ref[:, g * LANES:(g + 1) * LANES] = _sigmoid(
            pltpu.roll(t, LANES - (FOX_HEADS + g * per_group), axis=1))

    ga_ref[...] = _sigmoid(jnp.dot(h, wga_ref[...], preferred_element_type=F32)).astype(BF16)
    gb_ref[...] = _sigmoid(jnp.dot(h, wgb_ref[...], preferred_element_type=F32)).astype(BF16)


def _proj(x2, g, w_small, w_ga, w_gb, tables, bmisc, seq):
    n_tok, d = x2.shape
    tm = PROJ_TM
    layout, n_cols = _proj_layout()
    assert n_tok % tm == 0 and seq % tm == 0 and w_small.shape == (d, n_cols)
    tiles_per_seq = seq // tm

    def row(w):
        return pl.BlockSpec((tm, w), lambda i: (i, 0))

    def tab():
        return pl.BlockSpec((tm, LANES), lambda i: (i % tiles_per_seq, 0))

    outs = [('fq', BF16), ('fk', BF16), ('fv', BF16), ('nq', BF16), ('ksl', BF16), ('vsl', BF16),
            ('kwn', BF16), ('vwn', BF16), ('kc', F32), ('vc', F32), ('misc', F32), ('gates', F32)]
    widths = dict(kc=NSA_KV_WIDTH, vc=NSA_KV_WIDTH, misc=LANES, gates=NSA_KV_GROUPS * LANES)
    widths.update({n: FOX_HEADS * LANES for n in ('fq', 'fk', 'fv')})
    widths.update(nq=NSA_HEADS * LANES, **{n: NSA_KV_GROUPS * LANES for n in ('ksl', 'vsl', 'kwn', 'vwn')})
    out_shape = [jax.ShapeDtypeStruct((n_tok, widths[n]), dt) for n, dt in outs]
    out_specs = [row(widths[n]) for n, _ in outs]
    out_shape += [jax.ShapeDtypeStruct((n_tok, d), BF16)] * 2
    out_specs += [row(d), row(d)]
    return pl.pallas_call(
        functools.partial(_proj_kernel, tm=tm, seq=seq, layout=layout),
        grid=(n_tok // tm,),
        in_specs=[row(d), _resident((1, d)), _resident(w_small.shape), _resident(w_ga.shape),
                  _resident(w_gb.shape), tab(), tab(), tab(), _resident((1, LANES))],
        out_specs=out_specs,
        out_shape=out_shape,
        compiler_params=pltpu.CompilerParams(dimension_semantics=("arbitrary",),
                                             vmem_limit_bytes=VMEM_LIMIT_BYTES),
        name="proj",
    )(x2, g, w_small, w_ga, w_gb, *tables, bmisc)


def _cumsum_kernel(m_ref, ccol_ref, *, seq):
    r = lax.broadcasted_iota(jnp.int32, (LANES, LANES), 0)
    c = lax.broadcasted_iota(jnp.int32, (LANES, LANES), 1)
    tri = (r >= c).astype(F32)
    carry = jnp.zeros((1, LANES), F32)
    for b in range(seq // LANES):
        blk = m_ref[0, b * LANES:(b + 1) * LANES, :]
        cs = jnp.dot(tri, blk, precision=HIGHEST, preferred_element_type=F32) + carry
        ccol_ref[0, b * LANES:(b + 1) * LANES, :] = cs
        carry = cs[LANES - 1:LANES, :]


def _cumsum(misc3):
    b, seq, _ = misc3.shape
    assert seq % LANES == 0
    return pl.pallas_call(
        functools.partial(_cumsum_kernel, seq=seq),
        grid=(b,),
        in_specs=[pl.BlockSpec((1, seq, LANES), lambda i: (i, 0, 0))],
        out_specs=pl.BlockSpec((1, seq, LANES), lambda i: (i, 0, 0)),
        out_shape=jax.ShapeDtypeStruct((b, seq, LANES), F32),
        compiler_params=pltpu.CompilerParams(dimension_semantics=("arbitrary",),
                                             vmem_limit_bytes=VMEM_LIMIT_BYTES),
        name="cumsum",
    )(misc3)


def _softmax_step(s, pv, m_ref, acc_ref, rows):
    m_prev = m_ref[rows]
    m_new = jnp.maximum(m_prev, jnp.max(s, axis=-1, keepdims=True))
    p = jnp.exp2(s - m_new).astype(BF16)
    acc_ref[rows] = jnp.exp2(m_prev - m_new) * acc_ref[rows] + pv(p)
    m_ref[rows] = m_new


def _mask_table(tq):
    d = np.arange(tq)[:, None] - np.arange(tq)[None, :]
    neg = np.full((tq, tq), NEG_INF, np.float32)
    zero = np.zeros((tq, tq), np.float32)
    return jnp.asarray(np.stack([np.where(d > 0, zero, neg), zero, np.where(d <= 0, zero, neg), neg]))


def _mask_bias(tab_ref, key_tile0, n_tiles, q_tile, reps):
    rows = []
    for u in range(n_tiles):
        t = tab_ref[jnp.clip(key_tile0 + u - q_tile + 2, 0, 3)]
        rows.append(jnp.concatenate([t] * reps, axis=1) if reps > 1 else t)
    return jnp.concatenate(rows, axis=0) if n_tiles > 1 else rows[0]


def _split3(c):
    hi = c.astype(BF16).astype(F32)
    r = c - hi
    mid = r.astype(BF16).astype(F32)
    lo = (r - mid).astype(BF16).astype(F32)
    return hi, mid, lo


def _softmax_step_t(s, pv, m_ref, acc_ref, idx):
    m_prev = m_ref[idx]
    m_new = jnp.maximum(m_prev, jnp.max(s, axis=0, keepdims=True))
    p = jnp.exp2(s - m_new).astype(BF16)
    acc_ref[idx] = jnp.exp2(m_prev - m_new) * acc_ref[idx] + pv(p)
    m_ref[idx] = m_new


def _fox_kernel(q_ref, k_ref, v_ref, ccol_ref, o_ref, kx_s, vt_s, qx_s, m_s, acc_s, *, tq, tk, seq):
    i = pl.program_id(1)
    lane = lax.broadcasted_iota(jnp.int32, (tq, LANES), 1)
    left = lane < HEAD_DIM
    b0 = BIAS_LANE0

    @pl.when(i == 0)
    def _build_keys():
        def chunk(r, carry):
            rows = pl.ds(pl.multiple_of(r * tq, tq), tq)
            his, mids, los = _split3(ccol_ref[0, rows, :] * LOG2E)
            for h in range(FOX_HEADS):
                cols = slice(h * LANES, (h + 1) * LANES)
                hi, mid, lo = his[:, h:h + 1], mids[:, h:h + 1], los[:, h:h + 1]
                e = jnp.where(lane < b0 + 3, 1.0,
                              jnp.where(lane == b0 + 3, -hi,
                                        jnp.where(lane == b0 + 4, -mid, jnp.where(lane == b0 + 5, -lo, 0.0))))
                kx_s[h, rows, :] = jnp.where(left, k_ref[0, rows, cols].astype(F32), e).astype(BF16)
            return carry

        lax.fori_loop(0, seq // tq, chunk, 0)

        def vchunk(c, carry):
            rows = pl.ds(pl.multiple_of(c * tk, tk), tk)
            for h in range(FOX_HEADS):
                cols = slice(h * LANES, (h + 1) * LANES)
                vt_s[c, h] = v_ref[0, rows, cols].astype(F32).T.astype(BF16)
            return carry

        lax.fori_loop(0, seq // tk, vchunk, 0)

    qrows = pl.ds(pl.multiple_of(i * tq, tq), tq)
    n_full = (i * tq) // tk
    q_pos = i * tq + lax.broadcasted_iota(jnp.int32, (tk, tq), 1)
    k_off = lax.broadcasted_iota(jnp.int32, (tk, tq), 0)
    row = lax.broadcasted_iota(jnp.int32, (LANES, tq), 0)
    n_pairs = FOX_HEADS // 2
    his, mids, los = _split3(ccol_ref[0, qrows, :] * LOG2E)
    for h in range(FOX_HEADS):
        hi, mid, lo = his[:, h:h + 1], mids[:, h:h + 1], los[:, h:h + 1]
        e = jnp.where(lane == b0, hi,
                      jnp.where(lane == b0 + 1, mid,
                                jnp.where(lane == b0 + 2, lo, jnp.where(lane < b0 + 6, 1.0, 0.0))))
        qx_s[h] = jnp.where(left, q_ref[0, :, h * LANES:(h + 1) * LANES].astype(F32), e).astype(BF16)
    m_s[...] = jnp.full(m_s.shape, NEG_INF, F32)
    acc_s[...] = jnp.zeros(acc_s.shape, F32)

    def scores(j, p, nk=tk):
        rows = pl.ds(pl.multiple_of(j * tk, tk), nk)
        return tuple(_nt_dot(kx_s[2 * p + slot, rows, :], qx_s[2 * p + slot]) for slot in range(2))

    def consume(j, p, ss, masked, nk=tk):
        for slot in range(2):
            h = 2 * p + slot
            s = ss[slot]
            if masked:
                visible = (j * tk + lax.broadcasted_iota(jnp.int32, (nk, tq), 0)
                           <= i * tq + lax.broadcasted_iota(jnp.int32, (nk, tq), 1))
                s = jnp.where(visible, s, NEG_INF)
            _softmax_step_t(s, lambda pr, h=h: jnp.dot(vt_s[j, h, :, 0:nk], pr, preferred_element_type=F32),
                            m_s, acc_s, h)

    def key_step(j, ss, masked, last, nk=tk):
        for p in range(n_pairs):
            if p + 1 < n_pairs:
                nxt = scores(j, p + 1, nk)
            else:
                nxt = None if last else scores(j + 1, 0)
            consume(j, p, ss, masked, nk)
            ss = nxt
        return ss

    ss = lax.fori_loop(0, n_full, lambda j, ss: key_step(j, ss, False, False), scores(0, 0))

    def tail_short(ss):
        key_step(n_full, scores(n_full, 0, tq), True, True, tq)
        return 0

    def tail_full(ss):
        key_step(n_full, ss, True, True)
        return 0

    lax.cond((i * tq) % tk == 0, tail_short, tail_full, ss)

    for hp in range(n_pairs):
        a0, a1 = acc_s[2 * hp], acc_s[2 * hp + 1]
        ot = jnp.where(row < HEAD_DIM, a0 / a0[HEAD_DIM:HEAD_DIM + 1, :], a1 / a1[0:1, :])
        o_ref[0, :, hp * LANES:(hp + 1) * LANES] = ot.T.astype(BF16)


def _fox(fq, fk, fv, ccol):
    b, seq, w = fq.shape
    tq, tk = FOX_TQ, KV_STEP
    assert seq % tq == 0 and seq % tk == 0 and tk % tq == 0 and w == FOX_HEADS * LANES
    whole = pl.BlockSpec((1, seq, w), lambda bi, i: (bi, 0, 0))
    return pl.pallas_call(
        functools.partial(_fox_kernel, tq=tq, tk=tk, seq=seq),
        grid=(b, seq // tq),
        in_specs=[pl.BlockSpec((1, tq, w), lambda bi, i: (bi, i, 0)), whole, whole,
                  pl.BlockSpec((1, seq, LANES), lambda bi, i: (bi, 0, 0))],
        out_specs=pl.BlockSpec((1, tq, FOX_WIDTH), lambda bi, i: (bi, i, 0)),
        out_shape=jax.ShapeDtypeStruct((b, seq, FOX_WIDTH), BF16),
        scratch_shapes=[pltpu.VMEM((FOX_HEADS, seq, LANES), BF16),
                        pltpu.VMEM((seq // tk, FOX_HEADS, LANES, tk), BF16),
                        pltpu.VMEM((FOX_HEADS, tq, LANES), BF16),
                        pltpu.VMEM((FOX_HEADS, 1, tq), F32), pltpu.VMEM((FOX_HEADS, LANES, tq), F32)],
        compiler_params=pltpu.CompilerParams(dimension_semantics=("arbitrary", "arbitrary"),
                                             vmem_limit_bytes=VMEM_LIMIT_BYTES),
        name="fox",
    )(fq, fk, fv, ccol)


def _compress_kernel(yk_ref, yv_ref, posk_ref, w1k_ref, w2k_ref, posv_ref, w1v_ref, w2v_ref,
                     kc_ref, vc_ref, *, nb):
    half = CMP_STRIDE * HEAD_DIM

    lane = lax.broadcasted_iota(jnp.int32, (nb, LANES), 1)
    left = lane < HEAD_DIM

    def regroup(x_ref):
        y = [[], []]
        for j in range(CMP_STRIDE // 2):
            xe = x_ref[0, pl.ds(2 * j, nb, stride=CMP_STRIDE), :]
            xo = x_ref[0, pl.ds(2 * j + 1, nb, stride=CMP_STRIDE), :]
            y[0].append(jnp.where(left, xe, pltpu.roll(xo, HEAD_DIM, axis=1)))
            y[1].append(jnp.where(left, pltpu.roll(xe, HEAD_DIM, axis=1), xo))
        return [jnp.concatenate(t, axis=1) for t in y]

    def comp(y, pos_ref, w1_ref, w2_ref):
        pos = pos_ref[...]
        a = jnp.dot(y + pos[:, :half], w1_ref[0:half, :], precision=HIGHEST, preferred_element_type=F32)
        bm = jnp.dot(y + pos[:, half:], w1_ref[half:2 * half, :], precision=HIGHEST,
                     preferred_element_type=F32)
        pre = a + pltpu.roll(bm, nb - 1, axis=0)
        return jnp.dot(pre * _sigmoid(pre), w2_ref[...], precision=HIGHEST, preferred_element_type=F32)

    for g, (yk, yv) in enumerate(zip(regroup(yk_ref), regroup(yv_ref))):
        kc_ref[0, g] = comp(yk, posk_ref, w1k_ref, w2k_ref).astype(BF16)
        vc_ref[0, g] = comp(yv, posv_ref, w1v_ref, w2v_ref).astype(BF16)


def _compress(kc, vc, b, seq, pos_k, w1_k, w2_k, pos_v, w1_v, w2_v):
    g = NSA_KV_GROUPS
    nb = seq // CMP_STRIDE
    half = CMP_STRIDE * HEAD_DIM
    assert CMP_BLOCK == 2 * CMP_STRIDE and seq % CMP_STRIDE == 0 and nb % 8 == 0
    assert g == 2 and kc.shape[-1] == LANES

    zeros1 = jnp.zeros((CMP_BLOCK * HEAD_DIM, HEAD_DIM), F32)
    zeros2 = jnp.zeros((HEAD_DIM, HEAD_DIM), F32)
    w1k = jnp.concatenate([w1_k, zeros1], axis=-1)
    w1v = jnp.concatenate([w1_v, zeros1], axis=-1)
    w2k = jnp.concatenate([jnp.concatenate([w2_k, zeros2], axis=-1), jnp.zeros((HEAD_DIM, LANES), F32)], axis=0)
    w2v = jnp.concatenate([jnp.concatenate([w2_v, w2_v], axis=-1), jnp.zeros((HEAD_DIM, LANES), F32)], axis=0)
    yspec = pl.BlockSpec((1, seq, LANES), lambda bi: (bi, 0, 0))
    ospec = pl.BlockSpec((1, g, nb, LANES), lambda bi: (bi, 0, 0, 0))
    return pl.pallas_call(
        functools.partial(_compress_kernel, nb=nb),
        grid=(b,),
        in_specs=[yspec, yspec,
                  _resident((1, 2 * half)), _resident(w1k.shape), _resident(w2k.shape),
                  _resident((1, 2 * half)), _resident(w1v.shape), _resident(w2v.shape)],
        out_specs=[ospec, ospec],
        out_shape=[jax.ShapeDtypeStruct((b, g, nb, LANES), BF16)] * 2,
        compiler_params=pltpu.CompilerParams(dimension_semantics=("arbitrary",),
                                             vmem_limit_bytes=VMEM_LIMIT_BYTES),
        name="compress",
    )(kc.reshape(b, seq, LANES), vc.reshape(b, seq, LANES), pos_k.reshape(1, 2 * half), w1k, w2k,
      pos_v.reshape(1, 2 * half), w1v, w2v)


def _overlap_matrix(nb, n_cmp, n_sel):
    ci = np.arange(nb)[:, None] * CMP_STRIDE
    sj = (np.arange(LANES)[None, :] - SEL_LANE0) * SEL_BLOCK
    jj = np.arange(LANES)[None, :] - SEL_LANE0
    ov = (ci < sj + SEL_BLOCK) & (ci + CMP_BLOCK > sj) & (jj >= 0) & (jj < n_sel) & (np.arange(nb)[:, None] < n_cmp)
    return jnp.asarray(ov.astype(np.float32))


def _cmpsel_kernel(q_ref, kc_ref, vc_ref, ovt_ref, qs_ref, ocmp_ref, vct_s, *, tq, n_cmp, nb):
    i = pl.program_id(1)
    n_g, hg = NSA_KV_GROUPS, NSA_HEADS_PER_GROUP

    @pl.when(i == 0)
    def _transpose_values():
        for g in range(n_g):
            vct_s[g] = vc_ref[0, g].astype(F32).T.astype(BF16)

    n_idx = lax.broadcasted_iota(jnp.int32, (nb, tq), 0)
    t_pos = i * tq + lax.broadcasted_iota(jnp.int32, (nb, tq), 1)
    cmask = (n_idx * CMP_STRIDE + (CMP_BLOCK - 1) <= t_pos) & (n_idx < n_cmp)
    top = lax.broadcasted_iota(jnp.int32, (LANES, tq), 0) < HEAD_DIM
    psums = []
    for g in range(n_g):
        kc = kc_ref[0, g]
        vct = vct_s[g]
        psum = jnp.zeros((nb, tq), F32)
        outs = []
        for h in range(hg):
            cols = slice((g * hg + h) * LANES, (g * hg + h + 1) * LANES)
            s = jnp.where(cmask, _nt_dot(kc, q_ref[0, :, cols]), NEG_INF)
            m = jnp.max(s, axis=0, keepdims=True)
            p = jnp.where(cmask, jnp.exp2(s - m), 0.0)
            p = p / jnp.maximum(jnp.sum(p, axis=0, keepdims=True), 1e-30)
            psum = psum + p
            outs.append(jnp.dot(vct, p.astype(BF16), preferred_element_type=F32))
        psums.append(psum)
        for hp in range(hg // 2):
            ocols = slice((g * (hg // 2) + hp) * LANES, (g * (hg // 2) + hp + 1) * LANES)
            ocmp_ref[0, :, ocols] = jnp.where(top, outs[2 * hp], outs[2 * hp + 1]).T

    wide = n_g * tq
    p_slc = jnp.dot(ovt_ref[...], jnp.concatenate(psums, axis=1), precision=HIGHEST, preferred_element_type=F32)
    sc = p_slc[SEL_LANE0:SEL_LANE0 + MAX_SEL_BLOCKS, :]
    j = lax.broadcasted_iota(jnp.int32, (MAX_SEL_BLOCKS, wide), 0)
    cur = (i * tq + (lax.broadcasted_iota(jnp.int32, (MAX_SEL_BLOCKS, wide), 1) & (tq - 1))) // SEL_BLOCK
    forced = (j == 0) | (j == cur) | (j == cur - 1)
    score = jnp.where(j <= cur, sc + FORCED_BONUS * forced.astype(F32), NEG_INF)
    j_f = j.astype(F32)
    sel = jnp.zeros((MAX_SEL_BLOCKS, wide), jnp.bool_)
    for _ in range(SEL_TOPN):
        mx = jnp.max(score, axis=0, keepdims=True)
        first = jnp.min(jnp.where(score == mx, j_f, 1e4), axis=0, keepdims=True)
        chosen = j_f == first
        sel = sel | chosen
        score = jnp.where(chosen, -jnp.inf, score)
    bias_t = jnp.concatenate([jnp.zeros((SEL_LANE0, wide), F32), jnp.where(sel, 0.0, NEG_INF),
                              jnp.zeros((LANES - SEL_LANE0 - MAX_SEL_BLOCKS, wide), F32)], axis=0)
    for g in range(n_g):
        bias = bias_t[:, g * tq:(g + 1) * tq].T
        for h in range(hg):
            cols = slice((g * hg + h) * LANES, (g * hg + h + 1) * LANES)
            qs_ref[0, :, cols] = (q_ref[0, :, cols].astype(F32) + bias).astype(BF16)


def _cmpsel(nq3, kcmp, vcmp):
    b, seq, _ = nq3.shape
    g, hg = NSA_KV_GROUPS, NSA_HEADS_PER_GROUP
    tq = CMP_TQ
    nb = kcmp.shape[2]
    n_cmp = (seq - CMP_BLOCK) // CMP_STRIDE + 1
    n_sel = seq // SEL_BLOCK
    assert seq % tq == 0 and n_sel <= MAX_SEL_BLOCKS and n_sel >= SEL_TOPN
    ov = _overlap_matrix(nb, n_cmp, n_sel).T
    assert tq & (tq - 1) == 0
    qspec = pl.BlockSpec((1, tq, g * hg * LANES), lambda bi, i: (bi, i, 0))
    cspec = pl.BlockSpec((1, g, nb, LANES), lambda bi, i: (bi, 0, 0, 0))
    return pl.pallas_call(
        functools.partial(_cmpsel_kernel, tq=tq, n_cmp=n_cmp, nb=nb),
        grid=(b, seq // tq),
        in_specs=[qspec, cspec, cspec, _resident(ov.shape)],
        out_specs=[qspec, pl.BlockSpec((1, tq, NSA_WIDTH), lambda bi, i: (bi, i, 0))],
        out_shape=[jax.ShapeDtypeStruct(nq3.shape, BF16),
                   jax.ShapeDtypeStruct((b, seq, NSA_WIDTH), F32)],
        scratch_shapes=[pltpu.VMEM((g, LANES, nb), BF16)],
        compiler_params=pltpu.CompilerParams(dimension_semantics=("arbitrary",) * 2,
                                             vmem_limit_bytes=VMEM_LIMIT_BYTES),
        name="cmpsel",
    )(nq3, kcmp, vcmp, ov)


def _nsa_kernel(qs_ref, ks_ref, vs_ref, kw_ref, vw_ref, ocmp_ref, gates_ref, tab_ref, o_ref,
                vse_s, vso_s, vwe_s, vwo_s, m_s, acc_s, *, tq, tk, seq):
    i = pl.program_id(1)
    n_g, hg = NSA_KV_GROUPS, NSA_HEADS_PER_GROUP
    rows_all = hg * tq
    half = rows_all // 2
    wt = WINDOW // tq + 1

    @pl.when(i == 0)
    def _build_values():
        lane = lax.broadcasted_iota(jnp.int32, (tk, LANES), 1)
        one_e = (lane == HEAD_DIM).astype(F32)
        one_o = (lane == 0).astype(F32)

        def chunk(c, carry):
            rows = pl.ds(pl.multiple_of(c * tk, tk), tk)
            for g in range(n_g):
                cols = slice(g * LANES, (g + 1) * LANES)
                v = vs_ref[0, rows, cols].astype(F32)
                vse_s[g, c] = jnp.where(lane < HEAD_DIM, v, one_e).T.astype(BF16)
                vso_s[g, c] = jnp.where(lane < HEAD_DIM, one_o, v).T.astype(BF16)
                w = vw_ref[0, rows, cols].astype(F32)
                we = jnp.where(lane < HEAD_DIM, w, one_e).T.astype(BF16)
                wo = jnp.where(lane < HEAD_DIM, one_o, w).T.astype(BF16)
                for t in range(tk // tq):
                    vwe_s[g, c * (tk // tq) + t] = we[:, t * tq:(t + 1) * tq]
                    vwo_s[g, c * (tk // tq) + t] = wo[:, t * tq:(t + 1) * tq]
            return carry

        lax.fori_loop(0, seq // tk, chunk, 0)

    order = tuple(range(0, hg, 2)) + tuple(range(1, hg, 2))
    q = [jnp.concatenate([qs_ref[0, :, (g * hg + h) * LANES:(g * hg + h + 1) * LANES] for h in order], axis=0)
         for g in range(n_g)]

    n_full = (i * tq) // tk
    m_s[...] = jnp.full(m_s.shape, NEG_INF, F32)
    acc_s[...] = jnp.zeros(acc_s.shape, F32)

    def scores(j, g):
        rows = pl.ds(pl.multiple_of(j * tk, tk), tk)
        return _nt_dot(ks_ref[0, rows, g * LANES:(g + 1) * LANES], q[g])

    def consume(j, g, s, masked):
        if masked:
            s = s + _mask_bias(tab_ref, j * (tk // tq), tk // tq, i, hg)
        _softmax_step_t(s, lambda p: jnp.concatenate(
            [jnp.dot(vse_s[g, j], p[:, :half], preferred_element_type=F32),
             jnp.dot(vso_s[g, j], p[:, half:], preferred_element_type=F32)], axis=1), m_s, acc_s, g)

    st = jnp.maximum(i - (wt - 1), 0)

    def win_scores(g):
        wrows = pl.ds(pl.multiple_of(st * tq, tq), wt * tq)
        return _nt_dot(kw_ref[0, wrows, g * LANES:(g + 1) * LANES], q[g])

    def win_consume(g, s_win):
        s_win = s_win + _mask_bias(tab_ref, st, wt, i, hg)
        p = jnp.exp2(s_win - jnp.max(s_win, axis=0, keepdims=True)).astype(BF16)
        a_win = jnp.zeros((LANES, rows_all), F32)
        for t in range(wt):
            pt = p[t * tq:(t + 1) * tq]
            a_win = a_win + jnp.concatenate(
                [jnp.dot(vwe_s[g, st + t], pt[:, :half], preferred_element_type=F32),
                 jnp.dot(vwo_s[g, st + t], pt[:, half:], preferred_element_type=F32)], axis=1)
        return a_win

    def key_step(j, s):
        for g in range(n_g):
            nxt = scores(j, g + 1) if g + 1 < n_g else scores(j + 1, 0)
            consume(j, g, s, False)
            s = nxt
        return s

    s = lax.fori_loop(0, n_full, key_step, scores(0, 0))
    for g in range(n_g):
        nxt = scores(n_full, g + 1) if g + 1 < n_g else win_scores(0)
        consume(n_full, g, s, True)
        s = nxt
    a_win = []
    for g in range(n_g):
        nxt = win_scores(g + 1) if g + 1 < n_g else None
        a_win.append(win_consume(g, s))
        s = nxt

    lane = lax.broadcasted_iota(jnp.int32, (tq, LANES), 1)
    left = lane < HEAD_DIM
    top = lax.broadcasted_iota(jnp.int32, (LANES, tq), 0) < HEAD_DIM
    for g in range(n_g):
        gates = gates_ref[0, :, g * LANES:(g + 1) * LANES]
        a_sel = acc_s[g]
        for hp in range(hg // 2):
            h0, h1 = 2 * hp, 2 * hp + 1
            cols = slice((g * (hg // 2) + hp) * LANES, (g * (hg // 2) + hp + 1) * LANES)

            def pair(a):
                ae = a[:, hp * tq:(hp + 1) * tq]
                ao = a[:, half + hp * tq:half + (hp + 1) * tq]
                return jnp.where(top, ae / jnp.maximum(ae[HEAD_DIM:HEAD_DIM + 1, :], 1e-30),
                                 ao / jnp.maximum(ao[0:1, :], 1e-30)).T

            def gate(branch):
                return jnp.where(left, gates[:, 3 * h0 + branch:3 * h0 + branch + 1],
                                 gates[:, 3 * h1 + branch:3 * h1 + branch + 1])

            o = gate(0) * ocmp_ref[0, :, cols] + gate(1) * pair(a_sel) + gate(2) * pair(a_win[g])
            o_ref[0, :, cols] = o.astype(BF16)


def _nsa(qs, ksl, vsl, kwn, vwn, ocmp, gates):
    b, seq, _ = qs.shape
    g, hg = NSA_KV_GROUPS, NSA_HEADS_PER_GROUP
    tq, tk = NSA_TQ, KV_STEP
    assert seq % tk == 0 and tk % tq == 0 and WINDOW % tq == 0 and seq >= WINDOW + tq and tq & (tq - 1) == 0
    assert hg % 2 == 0
    assert WINDOW == 2 * tq
    kv = pl.BlockSpec((1, seq, g * LANES), lambda bi, i: (bi, 0, 0))
    ospec = pl.BlockSpec((1, tq, NSA_WIDTH), lambda bi, i: (bi, i, 0))
    return pl.pallas_call(
        functools.partial(_nsa_kernel, tq=tq, tk=tk, seq=seq),
        grid=(b, seq // tq),
        in_specs=[pl.BlockSpec((1, tq, g * hg * LANES), lambda bi, i: (bi, i, 0)), kv, kv, kv, kv,
                  ospec, pl.BlockSpec((1, tq, g * LANES), lambda bi, i: (bi, i, 0)), _resident((4, tq, tq))],
        out_specs=ospec,
        out_shape=jax.ShapeDtypeStruct((b, seq, NSA_WIDTH), BF16),
        scratch_shapes=[pltpu.VMEM((g, seq // tk, LANES, tk), BF16)] * 2
        + [pltpu.VMEM((g, seq // tq, LANES, tq), BF16)] * 2
        + [pltpu.VMEM((g, 1, hg * tq), F32), pltpu.VMEM((g, LANES, hg * tq), F32)],
        compiler_params=pltpu.CompilerParams(dimension_semantics=("arbitrary",) * 2,
                                             vmem_limit_bytes=VMEM_LIMIT_BYTES),
        name="nsa",
    )(qs, ksl, vsl, kwn, vwn, ocmp, gates, _mask_table(tq))


def _rms(y, g):
    return y * lax.rsqrt(jnp.mean(y * y, axis=-1, keepdims=True) + RMS_EPS) * g


def _tail_kernel(x_ref, fox_ref, nsa_ref, ga_ref, gb_ref, wf_ref, wn_ref, wo_ref, wup_ref, wdn_ref,
                 gmix_ref, gpre_ref, gpost_ref, o_ref, *, d_ff):
    a = jnp.dot(fox_ref[...], wf_ref[...], preferred_element_type=F32)
    b = jnp.dot(nsa_ref[...], wn_ref[...], preferred_element_type=F32)
    mix = ga_ref[...].astype(F32) * a + gb_ref[...].astype(F32) * b
    y = jnp.dot(mix.astype(BF16), wo_ref[...], preferred_element_type=F32)
    x1 = x_ref[...] + _rms(y, gmix_ref[...])
    h = _rms(x1, gpre_ref[...]).astype(BF16)
    acc = jnp.zeros(x1.shape, F32)
    for c in range(d_ff // FF_CHUNK):
        cols = slice(c * FF_CHUNK, (c + 1) * FF_CHUNK)
        u = jnp.maximum(jnp.dot(h, wup_ref[:, cols], preferred_element_type=F32), 0.0)
        acc = acc + jnp.dot((u * u).astype(BF16), wdn_ref[cols, :], preferred_element_type=F32)
    o_ref[...] = x1 + _rms(acc, gpost_ref[...])


def _tail(x2, fox, nsa, sga, sgb, wf, wn, wo, wup, wdn, g_mix_post, g_mlp_pre, g_mlp_post):
    n_tok, d = x2.shape
    d_ff = wup.shape[1]
    tm = TAIL_TM
    assert n_tok % tm == 0 and d_ff % FF_CHUNK == 0

    def row(w):
        return pl.BlockSpec((tm, w), lambda i: (i, 0))

    return pl.pallas_call(
        functools.partial(_tail_kernel, d_ff=d_ff),
        grid=(n_tok // tm,),
        in_specs=[row(d), row(fox.shape[1]), row(nsa.shape[1]), row(d), row(d),
                  _resident(wf.shape), _resident(wn.shape), _resident(wo.shape),
                  _resident(wup.shape), _resident(wdn.shape),
                  _resident((1, d)), _resident((1, d)), _resident((1, d))],
        out_specs=row(d),
        out_shape=jax.ShapeDtypeStruct((n_tok, d), F32),
        compiler_params=pltpu.CompilerParams(dimension_semantics=("arbitrary",),
                                             vmem_limit_bytes=VMEM_LIMIT_BYTES),
        name="tail",
    )(x2, fox, nsa, sga, sgb, wf, wn, wo, wup, wdn, g_mix_post, g_mlp_pre, g_mlp_post)


def _layer(x, norm_mix_pre, norm_mix_post, norm_mlp_pre, norm_mlp_post, w_in, b_forget,
           cmp_pos_k, cmp_w1_k, cmp_w2_k, cmp_pos_v, cmp_w1_v, cmp_w2_v,
           w_fox_out, w_nsa_out, w_o, w_up, w_down):
    b, seq, d = x.shape
    n_tok = b * seq
    x2 = x.reshape(n_tok, d)
    w_small, w_ga, w_gb = _rearrange_w_in(w_in, d)
    bmisc = jnp.concatenate([b_forget.astype(F32), jnp.zeros((LANES - FOX_HEADS,), F32)]).reshape(1, LANES)
    (fq, fk, fv, nq, ksl, vsl, kwn, vwn, kc, vc, misc, gates, sga, sgb) = _proj(
        x2, norm_mix_pre.reshape(1, d), w_small, w_ga, w_gb, _rope_tables(seq), bmisc, seq)

    def seq3(t):
        return t.reshape(b, seq, t.shape[-1])

    fox = _fox(seq3(fq), seq3(fk), seq3(fv), _cumsum(seq3(misc)))

    kcmp, vcmp = _compress(kc, vc, b, seq, cmp_pos_k, cmp_w1_k, cmp_w2_k, cmp_pos_v, cmp_w1_v, cmp_w2_v)
    qs, ocmp = _cmpsel(seq3(nq), kcmp, vcmp)
    nsa = _nsa(qs, seq3(ksl), seq3(vsl), seq3(kwn), seq3(vwn), ocmp, seq3(gates))

    out = _tail(x2, fox.reshape(n_tok, FOX_WIDTH), nsa.reshape(n_tok, NSA_WIDTH), sga, sgb,
                w_fox_out.astype(BF16), w_nsa_out.astype(BF16), w_o.astype(BF16),
                w_up.astype(BF16), w_down.astype(BF16),
                norm_mix_post.reshape(1, d), norm_mlp_pre.reshape(1, d), norm_mlp_post.reshape(1, d))
    return out.reshape(b, seq, d)


def kernel(x, norm_mix_pre, norm_mix_post, norm_mlp_pre, norm_mlp_post, w_in, b_forget, cmp_pos_k, cmp_w1_k, cmp_w2_k, cmp_pos_v, cmp_w1_v, cmp_w2_v, w_fox_out, w_nsa_out, w_o, w_up, w_down):
    params = (norm_mix_pre, norm_mix_post, norm_mlp_pre, norm_mlp_post, w_in, b_forget,
              cmp_pos_k, cmp_w1_k, cmp_w2_k, cmp_pos_v, cmp_w1_v, cmp_w2_v,
              w_fox_out, w_nsa_out, w_o, w_up, w_down)
    for layer in range(w_in.shape[0]):
        x = _layer(x, *[p[layer] for p in params])
    return x
```

```python
import functools

import numpy as np
import jax
import jax.numpy as jnp
from jax import lax
from jax.experimental import pallas as pl
from jax.experimental.pallas import tpu as pltpu

HEAD_DIM = 64
FOX_HEADS = 8
NSA_HEADS = 8
NSA_KV_GROUPS = 2
NSA_HEADS_PER_GROUP = NSA_HEADS // NSA_KV_GROUPS
FOX_WIDTH = FOX_HEADS * HEAD_DIM
NSA_WIDTH = NSA_HEADS * HEAD_DIM
NSA_KV_WIDTH = NSA_KV_GROUPS * HEAD_DIM
ROPE_THETA = 10000.0
CMP_BLOCK = 32
CMP_STRIDE = 16
SEL_BLOCK = 64
SEL_TOPN = 8
WINDOW = 512
RMS_EPS = 1e-6
NEG_INF = -1e30
FORCED_BONUS = 1e4
ATTN_SCALE = HEAD_DIM ** -0.5
LOG2E = 1.4426950408889634
Q_SCALE = ATTN_SCALE * LOG2E

LANES = 128
VMEM_LIMIT_BYTES = 52 * 1024 * 1024
SEL_LANE0 = HEAD_DIM
MAX_SEL_BLOCKS = 32

F32 = jnp.float32
BF16 = jnp.bfloat16
HIGHEST = lax.Precision.HIGHEST

PROJ_TM = 512
FOX_TQ = 256
CMP_TQ = 256
NSA_TQ = 256
TAIL_TM = 512
FF_CHUNK = 1024
KV_STEP = 512
BIAS_LANE0 = HEAD_DIM


def _nt_dot(a, b):
    return lax.dot_general(a, b, (((1,), (1,)), ((), ())), preferred_element_type=F32)


def _sigmoid(t):
    return 1.0 / (1.0 + jnp.exp(-t))


def _resident(shape):
    nd = len(shape)
    return pl.BlockSpec(shape, lambda *_: (0,) * nd, pipeline_mode=pl.Buffered(1))


def _proj_layout():
    widths = [('fq', FOX_WIDTH), ('fk', FOX_WIDTH), ('fv', FOX_WIDTH),
              ('nq', NSA_WIDTH),
              ('ksl', NSA_KV_WIDTH), ('vsl', NSA_KV_WIDTH), ('kwn', NSA_KV_WIDTH), ('vwn', NSA_KV_WIDTH),
              ('kc', NSA_KV_WIDTH), ('vc', NSA_KV_WIDTH),
              ('misc', LANES)]
    out, c = {}, 0
    for name, w in widths:
        out[name] = (c, w)
        c += w
    return out, c


def _rearrange_w_in(w_in, d_model):
    splits = (FOX_WIDTH, FOX_WIDTH, FOX_WIDTH, FOX_HEADS,
              NSA_WIDTH, NSA_KV_WIDTH, NSA_KV_WIDTH, NSA_KV_WIDTH, NSA_KV_WIDTH, NSA_KV_WIDTH, NSA_KV_WIDTH,
              3 * NSA_HEADS, d_model, d_model)
    pts = [int(p) for p in np.cumsum(splits)[:-1]]
    fq, fk, fv, ff, nq, kc, vc, ksl, vsl, kwn, vwn, ng, ga, gb = jnp.split(w_in, pts, axis=-1)
    d = w_in.shape[0]
    misc = jnp.concatenate([ff, ng, jnp.zeros((d, LANES - FOX_HEADS - 3 * NSA_HEADS), w_in.dtype)], axis=-1)
    w_small = jnp.concatenate([fq, fk, fv, nq, ksl, vsl, kwn, vwn, kc, vc, misc], axis=-1)
    return w_small.astype(BF16), ga.astype(BF16), gb.astype(BF16)


def _rope_tables(seq):
    half = HEAD_DIM // 2
    inv = jnp.power(jnp.float32(ROPE_THETA), -jnp.arange(0, HEAD_DIM, 2, dtype=F32) / HEAD_DIM)
    ang = jnp.arange(seq, dtype=F32)[:, None] * inv[None, :]
    cos, sin = jnp.cos(ang), jnp.sin(ang)
    zero = jnp.zeros_like(sin)
    cos_t = jnp.concatenate([cos, cos, cos, cos], axis=-1)
    s_lo = jnp.concatenate([-sin, zero, -sin, zero], axis=-1)
    s_hi = jnp.concatenate([zero, sin, zero, sin], axis=-1)
    assert cos_t.shape == (seq, 4 * half)
    return cos_t, s_lo, s_hi


def _proj_kernel(x_ref, g_ref, w_ref, wga_ref, wgb_ref, cos_ref, slo_ref, shi_ref, bmisc_ref,
                 fq_ref, fk_ref, fv_ref, nq_ref, ksl_ref, vsl_ref, kwn_ref, vwn_ref,
                 kc_ref, vc_ref, misc_ref, gates_ref, ga_ref, gb_ref, *, tm, seq, layout):
    x = x_ref[...]
    h = x * lax.rsqrt(jnp.mean(x * x, axis=-1, keepdims=True) + RMS_EPS) * g_ref[...]
    h = h.astype(BF16)
    cos, slo, shi = cos_ref[...], slo_ref[...], shi_ref[...]
    half = HEAD_DIM // 2

    def mm(name):
        c0, n = layout[name]
        return jnp.dot(h, w_ref[:, c0:c0 + n], preferred_element_type=F32)

    def rope_block(t):
        return t * cos + pltpu.roll(t, LANES - half, axis=1) * slo + pltpu.roll(t, half, axis=1) * shi

    def rope(t):
        n = t.shape[1] // LANES
        return [rope_block(t[:, b * LANES:(b + 1) * LANES]) for b in range(n)]

    lane = lax.broadcasted_iota(jnp.int32, (tm, LANES), 1)
    left = lane < HEAD_DIM

    def blocks(t):
        return [t[:, b * LANES:(b + 1) * LANES] for b in range(t.shape[1] // LANES)]

    def split_heads(pairs, o_ref, scale, fill=0.0):
        for p, blk in enumerate(pairs):
            blk = blk * scale
            o_ref[:, (2 * p) * LANES:(2 * p + 1) * LANES] = jnp.where(left, blk, fill).astype(BF16)
            o_ref[:, (2 * p + 1) * LANES:(2 * p + 2) * LANES] = jnp.where(
                left, pltpu.roll(blk, HEAD_DIM, axis=1), fill).astype(BF16)

    def dup_heads(pairs, o_ref):
        for p, blk in enumerate(pairs):
            swapped = pltpu.roll(blk, HEAD_DIM, axis=1)
            o_ref[:, (2 * p) * LANES:(2 * p + 1) * LANES] = jnp.where(left, blk, swapped).astype(BF16)
            o_ref[:, (2 * p + 1) * LANES:(2 * p + 2) * LANES] = jnp.where(left, swapped, blk).astype(BF16)

    split_heads(blocks(mm('fq')), fq_ref, Q_SCALE)
    split_heads(blocks(mm('fk')), fk_ref, 1.0)
    fv = mm('fv')
    for p in range(FOX_HEADS // 2):
        blk = fv[:, p * LANES:(p + 1) * LANES]
        fv_ref[:, (2 * p) * LANES:(2 * p + 1) * LANES] = jnp.where(
            left, blk, (lane == HEAD_DIM).astype(F32)).astype(BF16)
        fv_ref[:, (2 * p + 1) * LANES:(2 * p + 2) * LANES] = jnp.where(
            left, (lane == 0).astype(F32), blk).astype(BF16)

    split_heads(rope(mm('nq')), nq_ref, Q_SCALE)

    pos = (pl.program_id(0) % (seq // tm)) * tm + lax.broadcasted_iota(jnp.int32, (tm, LANES), 0)
    onehot = ((lane - SEL_LANE0) == (pos // SEL_BLOCK)).astype(F32)
    split_heads(rope(mm('ksl')), ksl_ref, 1.0, fill=onehot)
    dup_heads(blocks(mm('vsl')), vsl_ref)
    split_heads(rope(mm('kwn')), kwn_ref, 1.0)
    dup_heads(blocks(mm('vwn')), vwn_ref)

    kc_ref[...] = rope(mm('kc'))[0]
    vc_ref[...] = mm('vc')

    t = mm('misc') + bmisc_ref[...]
    misc_ref[...] = -(jnp.maximum(-t, 0.0) + jnp.log(1.0 + jnp.exp(-jnp.abs(t))))
    per_group = 3 * NSA_HEADS_PER_GROUP
    for g in range(NSA_KV_GROUPS):
        gates_ref[:, g * LANES:(g + 1) * LANES] = _sigmoid(
            pltpu.roll(t, LANES - (FOX_HEADS + g * per_group), axis=1))

    ga_ref[...] = _sigmoid(jnp.dot(h, wga_ref[...], preferred_element_type=F32)).astype(BF16)
    gb_ref[...] = _sigmoid(jnp.dot(h, wgb_ref[...], preferred_element_type=F32)).astype(BF16)


def _proj(x2, g, w_small, w_ga, w_gb, tables, bmisc, seq):
    n_tok, d = x2.shape
    tm = PROJ_TM
    layout, n_cols = _proj_layout()
    assert n_tok % tm == 0 and seq % tm == 0 and w_small.shape == (d, n_cols)
    tiles_per_seq = seq // tm

    def row(w):
        return pl.BlockSpec((tm, w), lambda i: (i, 0))

    def tab():
        return pl.BlockSpec((tm, LANES), lambda i: (i % tiles_per_seq, 0))

    outs = [('fq', BF16), ('fk', BF16), ('fv', BF16), ('nq', BF16), ('ksl', BF16), ('vsl', BF16),
            ('kwn', BF16), ('vwn', BF16), ('kc', F32), ('vc', F32), ('misc', F32), ('gates', F32)]
    widths = dict(kc=NSA_KV_WIDTH, vc=NSA_KV_WIDTH, misc=LANES, gates=NSA_KV_GROUPS * LANES)
    widths.update({n: FOX_HEADS * LANES for n in ('fq', 'fk', 'fv')})
    widths.update(nq=NSA_HEADS * LANES, **{n: NSA_KV_GROUPS * LANES for n in ('ksl', 'vsl', 'kwn', 'vwn')})
    out_shape = [jax.ShapeDtypeStruct((n_tok, widths[n]), dt) for n, dt in outs]
    out_specs = [row(widths[n]) for n, _ in outs]
    out_shape += [jax.ShapeDtypeStruct((n_tok, d), BF16)] * 2
    out_specs += [row(d), row(d)]
    return pl.pallas_call(
        functools.partial(_proj_kernel, tm=tm, seq=seq, layout=layout),
        grid=(n_tok // tm,),
        in_specs=[row(d), _resident((1, d)), _resident(w_small.shape), _resident(w_ga.shape),
                  _resident(w_gb.shape), tab(), tab(), tab(), _resident((1, LANES))],
        out_specs=out_specs,
        out_shape=out_shape,
        compiler_params=pltpu.CompilerParams(dimension_semantics=("arbitrary",),
                                             vmem_limit_bytes=VMEM_LIMIT_BYTES),
        name="proj",
    )(x2, g, w_small, w_ga, w_gb, *tables, bmisc)


def _cumsum_kernel(m_ref, ccol_ref, *, seq):
    r = lax.broadcasted_iota(jnp.int32, (LANES, LANES), 0)
    c = lax.broadcasted_iota(jnp.int32, (LANES, LANES), 1)
    tri = (r >= c).astype(F32)
    carry = jnp.zeros((1, LANES), F32)
    for b in range(seq // LANES):
        blk = m_ref[0, b * LANES:(b + 1) * LANES, :]
        cs = jnp.dot(tri, blk, precision=HIGHEST, preferred_element_type=F32) + carry
        ccol_ref[0, b * LANES:(b + 1) * LANES, :] = cs
        carry = cs[LANES - 1:LANES, :]


def _cumsum(misc3):
    b, seq, _ = misc3.shape
    assert seq % LANES == 0
    return pl.pallas_call(
        functools.partial(_cumsum_kernel, seq=seq),
        grid=(b,),
        in_specs=[pl.BlockSpec((1, seq, LANES), lambda i: (i, 0, 0))],
        out_specs=pl.BlockSpec((1, seq, LANES), lambda i: (i, 0, 0)),
        out_shape=jax.ShapeDtypeStruct((b, seq, LANES), F32),
        compiler_params=pltpu.CompilerParams(dimension_semantics=("arbitrary",),
                                             vmem_limit_bytes=VMEM_LIMIT_BYTES),
        name="cumsum",
    )(misc3)


def _softmax_step(s, pv, m_ref, acc_ref, rows):
    m_prev = m_ref[rows]
    m_new = jnp.maximum(m_prev, jnp.max(s, axis=-1, keepdims=True))
    p = jnp.exp2(s - m_new).astype(BF16)
    acc_ref[rows] = jnp.exp2(m_prev - m_new) * acc_ref[rows] + pv(p)
    m_ref[rows] = m_new


def _mask_table(tq):
    d = np.arange(tq)[:, None] - np.arange(tq)[None, :]
    neg = np.full((tq, tq), NEG_INF, np.float32)
    zero = np.zeros((tq, tq), np.float32)
    return jnp.asarray(np.stack([np.where(d > 0, zero, neg), zero, np.where(d <= 0, zero, neg), neg]))


def _mask_bias(tab_ref, key_tile0, n_tiles, q_tile, reps):
    rows = []
    for u in range(n_tiles):
        t = tab_ref[jnp.clip(key_tile0 + u - q_tile + 2, 0, 3)]
        rows.append(jnp.concatenate([t] * reps, axis=1) if reps > 1 else t)
    return jnp.concatenate(rows, axis=0) if n_tiles > 1 else rows[0]


def _split3(c):
    hi = c.astype(BF16).astype(F32)
    r = c - hi
    mid = r.astype(BF16).astype(F32)
    lo = (r - mid).astype(BF16).astype(F32)
    return hi, mid, lo


def _softmax_step_t(s, pv, m_ref, acc_ref, idx):
    m_prev = m_ref[idx]
    m_new = jnp.maximum(m_prev, jnp.max(s, axis=0, keepdims=True))
    p = jnp.exp2(s - m_new).astype(BF16)
    acc_ref[idx] = jnp.exp2(m_prev - m_new) * acc_ref[idx] + pv(p)
    m_ref[idx] = m_new


def _fox_kernel(q_ref, k_ref, v_ref, ccol_ref, o_ref, kx_s, vt_s, qx_s, m_s, acc_s, *, tq, tk, seq):
    i = pl.program_id(1)
    lane = lax.broadcasted_iota(jnp.int32, (tq, LANES), 1)
    left = lane < HEAD_DIM
    b0 = BIAS_LANE0

    @pl.when(i == 0)
    def _build_keys():
        def chunk(r, carry):
            rows = pl.ds(pl.multiple_of(r * tq, tq), tq)
            his, mids, los = _split3(ccol_ref[0, rows, :] * LOG2E)
            for h in range(FOX_HEADS):
                cols = slice(h * LANES, (h + 1) * LANES)
                hi, mid, lo = his[:, h:h + 1], mids[:, h:h + 1], los[:, h:h + 1]
                e = jnp.where(lane < b0 + 3, 1.0,
                              jnp.where(lane == b0 + 3, -hi,
                                        jnp.where(lane == b0 + 4, -mid, jnp.where(lane == b0 + 5, -lo, 0.0))))
                kx_s[h, rows, :] = jnp.where(left, k_ref[0, rows, cols].astype(F32), e).astype(BF16)
            return carry

        lax.fori_loop(0, seq // tq, chunk, 0)

        def vchunk(c, carry):
            rows = pl.ds(pl.multiple_of(c * tk, tk), tk)
            for h in range(FOX_HEADS):
                cols = slice(h * LANES, (h + 1) * LANES)
                vt_s[c, h] = v_ref[0, rows, cols].astype(F32).T.astype(BF16)
            return carry

        lax.fori_loop(0, seq // tk, vchunk, 0)

    qrows = pl.ds(pl.multiple_of(i * tq, tq), tq)
    n_full = (i * tq) // tk
    q_pos = i * tq + lax.broadcasted_iota(jnp.int32, (tk, tq), 1)
    k_off = lax.broadcasted_iota(jnp.int32, (tk, tq), 0)
    row = lax.broadcasted_iota(jnp.int32, (LANES, tq), 0)
    n_pairs = FOX_HEADS // 2
    his, mids, los = _split3(ccol_ref[0, qrows, :] * LOG2E)
    for h in range(FOX_HEADS):
        hi, mid, lo = his[:, h:h + 1], mids[:, h:h + 1], los[:, h:h + 1]
        e = jnp.where(lane == b0, hi,
                      jnp.where(lane == b0 + 1, mid,
                                jnp.where(lane == b0 + 2, lo, jnp.where(lane < b0 + 6, 1.0, 0.0))))
        qx_s[h] = jnp.where(left, q_ref[0, :, h * LANES:(h + 1) * LANES].astype(F32), e).astype(BF16)
    m_s[...] = jnp.full(m_s.shape, NEG_INF, F32)
    acc_s[...] = jnp.zeros(acc_s.shape, F32)

    def scores(j, p, nk=tk):
        rows = pl.ds(pl.multiple_of(j * tk, tk), nk)
        return tuple(_nt_dot(kx_s[2 * p + slot, rows, :], qx_s[2 * p + slot]) for slot in range(2))

    def consume(j, p, ss, masked, nk=tk):
        for slot in range(2):
            h = 2 * p + slot
            s = ss[slot]
            if masked:
                visible = (j * tk + lax.broadcasted_iota(jnp.int32, (nk, tq), 0)
                           <= i * tq + lax.broadcasted_iota(jnp.int32, (nk, tq), 1))
                s = jnp.where(visible, s, NEG_INF)
            _softmax_step_t(s, lambda pr, h=h: jnp.dot(vt_s[j, h, :, 0:nk], pr, preferred_element_type=F32),
                            m_s, acc_s, h)

    def key_step(j, ss, masked, last, nk=tk):
        for p in range(n_pairs):
            if p + 1 < n_pairs:
                nxt = scores(j, p + 1, nk)
            else:
                nxt = None if last else scores(j + 1, 0)
            consume(j, p, ss, masked, nk)
            ss = nxt
        return ss

    ss = lax.fori_loop(0, n_full, lambda j, ss: key_step(j, ss, False, False), scores(0, 0))

    def tail_short(ss):
        key_step(n_full, scores(n_full, 0, tq), True, True, tq)
        return 0

    def tail_full(ss):
        key_step(n_full, ss, True, True)
        return 0

    lax.cond((i * tq) % tk == 0, tail_short, tail_full, ss)

    for hp in range(n_pairs):
        a0, a1 = acc_s[2 * hp], acc_s[2 * hp + 1]
        ot = jnp.where(row < HEAD_DIM, a0 / a0[HEAD_DIM:HEAD_DIM + 1, :], a1 / a1[0:1, :])
        o_ref[0, :, hp * LANES:(hp + 1) * LANES] = ot.T.astype(BF16)


def _fox(fq, fk, fv, ccol):
    b, seq, w = fq.shape
    tq, tk = FOX_TQ, KV_STEP
    assert seq % tq == 0 and seq % tk == 0 and tk % tq == 0 and w == FOX_HEADS * LANES
    whole = pl.BlockSpec((1, seq, w), lambda bi, i: (bi, 0, 0))
    return pl.pallas_call(
        functools.partial(_fox_kernel, tq=tq, tk=tk, seq=seq),
        grid=(b, seq // tq),
        in_specs=[pl.BlockSpec((1, tq, w), lambda bi, i: (bi, i, 0)), whole, whole,
                  pl.BlockSpec((1, seq, LANES), lambda bi, i: (bi, 0, 0))],
        out_specs=pl.BlockSpec((1, tq, FOX_WIDTH), lambda bi, i: (bi, i, 0)),
        out_shape=jax.ShapeDtypeStruct((b, seq, FOX_WIDTH), BF16),
        scratch_shapes=[pltpu.VMEM((FOX_HEADS, seq, LANES), BF16),
                        pltpu.VMEM((seq // tk, FOX_HEADS, LANES, tk), BF16),
                        pltpu.VMEM((FOX_HEADS, tq, LANES), BF16),
                        pltpu.VMEM((FOX_HEADS, 1, tq), F32), pltpu.VMEM((FOX_HEADS, LANES, tq), F32)],
        compiler_params=pltpu.CompilerParams(dimension_semantics=("arbitrary", "arbitrary"),
                                             vmem_limit_bytes=VMEM_LIMIT_BYTES),
        name="fox",
    )(fq, fk, fv, ccol)


def _compress_kernel(yk_ref, yv_ref, posk_ref, w1k_ref, w2k_ref, posv_ref, w1v_ref, w2v_ref,
                     kc_ref, vc_ref, *, nb):
    half = CMP_STRIDE * HEAD_DIM

    lane = lax.broadcasted_iota(jnp.int32, (nb, LANES), 1)
    left = lane < HEAD_DIM

    def regroup(x_ref):
        y = [[], []]
        for j in range(CMP_STRIDE // 2):
            xe = x_ref[0, pl.ds(2 * j, nb, stride=CMP_STRIDE), :]
            xo = x_ref[0, pl.ds(2 * j + 1, nb, stride=CMP_STRIDE), :]
            y[0].append(jnp.where(left, xe, pltpu.roll(xo, HEAD_DIM, axis=1)))
            y[1].append(jnp.where(left, pltpu.roll(xe, HEAD_DIM, axis=1), xo))
        return [jnp.concatenate(t, axis=1) for t in y]

    def comp(y, pos_ref, w1_ref, w2_ref):
        pos = pos_ref[...]
        a = jnp.dot(y + pos[:, :half], w1_ref[0:half, :], precision=HIGHEST, preferred_element_type=F32)
        bm = jnp.dot(y + pos[:, half:], w1_ref[half:2 * half, :], precision=HIGHEST,
                     preferred_element_type=F32)
        pre = a + pltpu.roll(bm, nb - 1, axis=0)
        return jnp.dot(pre * _sigmoid(pre), w2_ref[...], precision=HIGHEST, preferred_element_type=F32)

    for g, (yk, yv) in enumerate(zip(regroup(yk_ref), regroup(yv_ref))):
        kc_ref[0, g] = comp(yk, posk_ref, w1k_ref, w2k_ref).astype(BF16)
        vc_ref[0, g] = comp(yv, posv_ref, w1v_ref, w2v_ref).astype(BF16)


def _compress(kc, vc, b, seq, pos_k, w1_k, w2_k, pos_v, w1_v, w2_v):
    g = NSA_KV_GROUPS
    nb = seq // CMP_STRIDE
    half = CMP_STRIDE * HEAD_DIM
    assert CMP_BLOCK == 2 * CMP_STRIDE and seq % CMP_STRIDE == 0 and nb % 8 == 0
    assert g == 2 and kc.shape[-1] == LANES

    zeros1 = jnp.zeros((CMP_BLOCK * HEAD_DIM, HEAD_DIM), F32)
    zeros2 = jnp.zeros((HEAD_DIM, HEAD_DIM), F32)
    w1k = jnp.concatenate([w1_k, zeros1], axis=-1)
    w1v = jnp.concatenate([w1_v, zeros1], axis=-1)
    w2k = jnp.concatenate([jnp.concatenate([w2_k, zeros2], axis=-1), jnp.zeros((HEAD_DIM, LANES), F32)], axis=0)
    w2v = jnp.concatenate([jnp.concatenate([w2_v, w2_v], axis=-1), jnp.zeros((HEAD_DIM, LANES), F32)], axis=0)
    yspec = pl.BlockSpec((1, seq, LANES), lambda bi: (bi, 0, 0))
    ospec = pl.BlockSpec((1, g, nb, LANES), lambda bi: (bi, 0, 0, 0))
    return pl.pallas_call(
        functools.partial(_compress_kernel, nb=nb),
        grid=(b,),
        in_specs=[yspec, yspec,
                  _resident((1, 2 * half)), _resident(w1k.shape), _resident(w2k.shape),
                  _resident((1, 2 * half)), _resident(w1v.shape), _resident(w2v.shape)],
        out_specs=[ospec, ospec],
        out_shape=[jax.ShapeDtypeStruct((b, g, nb, LANES), BF16)] * 2,
        compiler_params=pltpu.CompilerParams(dimension_semantics=("arbitrary",),
                                             vmem_limit_bytes=VMEM_LIMIT_BYTES),
        name="compress",
    )(kc.reshape(b, seq, LANES), vc.reshape(b, seq, LANES), pos_k.reshape(1, 2 * half), w1k, w2k,
      pos_v.reshape(1, 2 * half), w1v, w2v)


def _overlap_matrix(nb, n_cmp, n_sel):
    ci = np.arange(nb)[:, None] * CMP_STRIDE
    sj = (np.arange(LANES)[None, :] - SEL_LANE0) * SEL_BLOCK
    jj = np.arange(LANES)[None, :] - SEL_LANE0
    ov = (ci < sj + SEL_BLOCK) & (ci + CMP_BLOCK > sj) & (jj >= 0) & (jj < n_sel) & (np.arange(nb)[:, None] < n_cmp)
    return jnp.asarray(ov.astype(np.float32))


def _cmpsel_kernel(q_ref, kc_ref, vc_ref, ovt_ref, qs_ref, ocmp_ref, vct_s, *, tq, n_cmp, nb):
    i = pl.program_id(1)
    n_g, hg = NSA_KV_GROUPS, NSA_HEADS_PER_GROUP

    @pl.when(i == 0)
    def _transpose_values():
        for g in range(n_g):
            vct_s[g] = vc_ref[0, g].astype(F32).T.astype(BF16)

    n_idx = lax.broadcasted_iota(jnp.int32, (nb, tq), 0)
    t_pos = i * tq + lax.broadcasted_iota(jnp.int32, (nb, tq), 1)
    cmask = (n_idx * CMP_STRIDE + (CMP_BLOCK - 1) <= t_pos) & (n_idx < n_cmp)
    top = lax.broadcasted_iota(jnp.int32, (LANES, tq), 0) < HEAD_DIM
    psums = []
    for g in range(n_g):
        kc = kc_ref[0, g]
        vct = vct_s[g]
        psum = jnp.zeros((nb, tq), F32)
        outs = []
        for h in range(hg):
            cols = slice((g * hg + h) * LANES, (g * hg + h + 1) * LANES)
            s = jnp.where(cmask, _nt_dot(kc, q_ref[0, :, cols]), NEG_INF)
            m = jnp.max(s, axis=0, keepdims=True)
            p = jnp.where(cmask, jnp.exp2(s - m), 0.0)
            p = p / jnp.maximum(jnp.sum(p, axis=0, keepdims=True), 1e-30)
            psum = psum + p
            outs.append(jnp.dot(vct, p.astype(BF16), preferred_element_type=F32))
        psums.append(psum)
        for hp in range(hg // 2):
            ocols = slice((g * (hg // 2) + hp) * LANES, (g * (hg // 2) + hp + 1) * LANES)
            ocmp_ref[0, :, ocols] = jnp.where(top, outs[2 * hp], outs[2 * hp + 1]).T

    wide = n_g * tq
    p_slc = jnp.dot(ovt_ref[...], jnp.concatenate(psums, axis=1), precision=HIGHEST, preferred_element_type=F32)
    sc = p_slc[SEL_LANE0:SEL_LANE0 + MAX_SEL_BLOCKS, :]
    j = lax.broadcasted_iota(jnp.int32, (MAX_SEL_BLOCKS, wide), 0)
    cur = (i * tq + (lax.broadcasted_iota(jnp.int32, (MAX_SEL_BLOCKS, wide), 1) & (tq - 1))) // SEL_BLOCK
    forced = (j == 0) | (j == cur) | (j == cur - 1)
    score = jnp.where(j <= cur, sc + FORCED_BONUS * forced.astype(F32), NEG_INF)
    j_f = j.astype(F32)
    sel = jnp.zeros((MAX_SEL_BLOCKS, wide), jnp.bool_)
    for _ in range(SEL_TOPN):
        mx = jnp.max(score, axis=0, keepdims=True)
        first = jnp.min(jnp.where(score == mx, j_f, 1e4), axis=0, keepdims=True)
        chosen = j_f == first
        sel = sel | chosen
        score = jnp.where(chosen, -jnp.inf, score)
    bias_t = jnp.concatenate([jnp.zeros((SEL_LANE0, wide), F32), jnp.where(sel, 0.0, NEG_INF),
                              jnp.zeros((LANES - SEL_LANE0 - MAX_SEL_BLOCKS, wide), F32)], axis=0)
    for g in range(n_g):
        bias = bias_t[:, g * tq:(g + 1) * tq].T
        for h in range(hg):
            cols = slice((g * hg + h) * LANES, (g * hg + h + 1) * LANES)
            qs_ref[0, :, cols] = (q_ref[0, :, cols].astype(F32) + bias).astype(BF16)


def _cmpsel(nq3, kcmp, vcmp):
    b, seq, _ = nq3.shape
    g, hg = NSA_KV_GROUPS, NSA_HEADS_PER_GROUP
    tq = CMP_TQ
    nb = kcmp.shape[2]
    n_cmp = (seq - CMP_BLOCK) // CMP_STRIDE + 1
    n_sel = seq // SEL_BLOCK
    assert seq % tq == 0 and n_sel <= MAX_SEL_BLOCKS and n_sel >= SEL_TOPN
    ov = _overlap_matrix(nb, n_cmp, n_sel).T
    assert tq & (tq - 1) == 0
    qspec = pl.BlockSpec((1, tq, g * hg * LANES), lambda bi, i: (bi, i, 0))
    cspec = pl.BlockSpec((1, g, nb, LANES), lambda bi, i: (bi, 0, 0, 0))
    return pl.pallas_call(
        functools.partial(_cmpsel_kernel, tq=tq, n_cmp=n_cmp, nb=nb),
        grid=(b, seq // tq),
        in_specs=[qspec, cspec, cspec, _resident(ov.shape)],
        out_specs=[qspec, pl.BlockSpec((1, tq, NSA_WIDTH), lambda bi, i: (bi, i, 0))],
        out_shape=[jax.ShapeDtypeStruct(nq3.shape, BF16),
                   jax.ShapeDtypeStruct((b, seq, NSA_WIDTH), F32)],
        scratch_shapes=[pltpu.VMEM((g, LANES, nb), BF16)],
        compiler_params=pltpu.CompilerParams(dimension_semantics=("arbitrary",) * 2,
                                             vmem_limit_bytes=VMEM_LIMIT_BYTES),
        name="cmpsel",
    )(nq3, kcmp, vcmp, ov)


def _nsa_kernel(qs_ref, ks_ref, vs_ref, kw_ref, vw_ref, ocmp_ref, gates_ref, tab_ref, o_ref,
                vse_s, vso_s, vwe_s, vwo_s, m_s, acc_s, *, tq, tk, seq):
    i = pl.program_id(1)
    n_g, hg = NSA_KV_GROUPS, NSA_HEADS_PER_GROUP
    rows_all = hg * tq
    half = rows_all // 2
    wt = WINDOW // tq + 1

    @pl.when(i == 0)
    def _build_values():
        lane = lax.broadcasted_iota(jnp.int32, (tk, LANES), 1)
        one_e = (lane == HEAD_DIM).astype(F32)
        one_o = (lane == 0).astype(F32)

        def chunk(c, carry):
            rows = pl.ds(pl.multiple_of(c * tk, tk), tk)
            for g in range(n_g):
                cols = slice(g * LANES, (g + 1) * LANES)
                v = vs_ref[0, rows, cols].astype(F32)
                vse_s[g, c] = jnp.where(lane < HEAD_DIM, v, one_e).T.astype(BF16)
                vso_s[g, c] = jnp.where(lane < HEAD_DIM, one_o, v).T.astype(BF16)
                w = vw_ref[0, rows, cols].astype(F32)
                we = jnp.where(lane < HEAD_DIM, w, one_e).T.astype(BF16)
                wo = jnp.where(lane < HEAD_DIM, one_o, w).T.astype(BF16)
                for t in range(tk // tq):
                    vwe_s[g, c * (tk // tq) + t] = we[:, t * tq:(t + 1) * tq]
                    vwo_s[g, c * (tk // tq) + t] = wo[:, t * tq:(t + 1) * tq]
            return carry

        lax.fori_loop(0, seq // tk, chunk, 0)

    order = tuple(range(0, hg, 2)) + tuple(range(1, hg, 2))
    q = [jnp.concatenate([qs_ref[0, :, (g * hg + h) * LANES:(g * hg + h + 1) * LANES] for h in order], axis=0)
         for g in range(n_g)]

    n_full = (i * tq) // tk
    m_s[...] = jnp.full(m_s.shape, NEG_INF, F32)
    acc_s[...] = jnp.zeros(acc_s.shape, F32)

    def scores(j, g):
        rows = pl.ds(pl.multiple_of(j * tk, tk), tk)
        return _nt_dot(ks_ref[0, rows, g * LANES:(g + 1) * LANES], q[g])

    def consume(j, g, s, masked):
        if masked:
            s = s + _mask_bias(tab_ref, j * (tk // tq), tk // tq, i, hg)
        _softmax_step_t(s, lambda p: jnp.concatenate(
            [jnp.dot(vse_s[g, j], p[:, :half], preferred_element_type=F32),
             jnp.dot(vso_s[g, j], p[:, half:], preferred_element_type=F32)], axis=1), m_s, acc_s, g)

    st = jnp.maximum(i - (wt - 1), 0)

    def win_scores(g):
        wrows = pl.ds(pl.multiple_of(st * tq, tq), wt * tq)
        return _nt_dot(kw_ref[0, wrows, g * LANES:(g + 1) * LANES], q[g])

    def win_consume(g, s_win):
        s_win = s_win + _mask_bias(tab_ref, st, wt, i, hg)
        p = jnp.exp2(s_win - jnp.max(s_win, axis=0, keepdims=True)).astype(BF16)
        a_win = jnp.zeros((LANES, rows_all), F32)
        for t in range(wt):
            pt = p[t * tq:(t + 1) * tq]
            a_win = a_win + jnp.concatenate(
                [jnp.dot(vwe_s[g, st + t], pt[:, :half], preferred_element_type=F32),
                 jnp.dot(vwo_s[g, st + t], pt[:, half:], preferred_element_type=F32)], axis=1)
        return a_win

    def key_step(j, s):
        for g in range(n_g):
            nxt = scores(j, g + 1) if g + 1 < n_g else scores(j + 1, 0)
            consume(j, g, s, False)
            s = nxt
        return s

    s = lax.fori_loop(0, n_full, key_step, scores(0, 0))

    for g in range(n_g):
        nxt = scores(n_full, g + 1) if g + 1 < n_g else win_scores(0)
        consume(n_full, g, s, True)
        s = nxt
    a_win = []
    for g in range(n_g):
        nxt = win_scores(g + 1) if g + 1 < n_g else None
        a_win.append(win_consume(g, s))
        s = nxt

    lane = lax.broadcasted_iota(jnp.int32, (tq, LANES), 1)
    left = lane < HEAD_DIM
    top = lax.broadcasted_iota(jnp.int32, (LANES, tq), 0) < HEAD_DIM
    for g in range(n_g):
        gates = gates_ref[0, :, g * LANES:(g + 1) * LANES]
        a_sel = acc_s[g]
        for hp in range(hg // 2):
            h0, h1 = 2 * hp, 2 * hp + 1
            cols = slice((g * (hg // 2) + hp) * LANES, (g * (hg // 2) + hp + 1) * LANES)

            def pair(a):
                ae = a[:, hp * tq:(hp + 1) * tq]
                ao = a[:, half + hp * tq:half + (hp + 1) * tq]
                return jnp.where(top, ae / jnp.maximum(ae[HEAD_DIM:HEAD_DIM + 1, :], 1e-30),
                                 ao / jnp.maximum(ao[0:1, :], 1e-30)).T

            def gate(branch):
                return jnp.where(left, gates[:, 3 * h0 + branch:3 * h0 + branch + 1],
                                 gates[:, 3 * h1 + branch:3 * h1 + branch + 1])

            o = gate(0) * ocmp_ref[0, :, cols] + gate(1) * pair(a_sel) + gate(2) * pair(a_win[g])
            o_ref[0, :, cols] = o.astype(BF16)


def _nsa(qs, ksl, vsl, kwn, vwn, ocmp, gates):
    b, seq, _ = qs.shape
    g, hg = NSA_KV_GROUPS, NSA_HEADS_PER_GROUP
    tq, tk = NSA_TQ, KV_STEP
    assert seq % tk == 0 and tk % tq == 0 and WINDOW % tq == 0 and seq >= WINDOW + tq and tq & (tq - 1) == 0
    assert hg % 2 == 0
    assert WINDOW == 2 * tq
    kv = pl.BlockSpec((1, seq, g * LANES), lambda bi, i: (bi, 0, 0))
    ospec = pl.BlockSpec((1, tq, NSA_WIDTH), lambda bi, i: (bi, i, 0))
    return pl.pallas_call(
        functools.partial(_nsa_kernel, tq=tq, tk=tk, seq=seq),
        grid=(b, seq // tq),
        in_specs=[pl.BlockSpec((1, tq, g * hg * LANES), lambda bi, i: (bi, i, 0)), kv, kv, kv, kv,
                  ospec, pl.BlockSpec((1, tq, g * LANES), lambda bi, i: (bi, i, 0)), _resident((4, tq, tq))],
        out_specs=ospec,
        out_shape=jax.ShapeDtypeStruct((b, seq, NSA_WIDTH), BF16),
        scratch_shapes=[pltpu.VMEM((g, seq // tk, LANES, tk), BF16)] * 2
        + [pltpu.VMEM((g, seq // tq, LANES, tq), BF16)] * 2
        + [pltpu.VMEM((g, 1, hg * tq), F32), pltpu.VMEM((g, LANES, hg * tq), F32)],
        compiler_params=pltpu.CompilerParams(dimension_semantics=("arbitrary",) * 2,
                                             vmem_limit_bytes=VMEM_LIMIT_BYTES),
        name="nsa",
    )(qs, ksl, vsl, kwn, vwn, ocmp, gates, _mask_table(tq))


def _rms(y, g):
    return y * lax.rsqrt(jnp.mean(y * y, axis=-1, keepdims=True) + RMS_EPS) * g


def _tail_kernel(x_ref, fox_ref, nsa_ref, ga_ref, gb_ref, wf_ref, wn_ref, wo_ref, wup_ref, wdn_ref,
                 gmix_ref, gpre_ref, gpost_ref, o_ref, *, d_ff):
    a = jnp.dot(fox_ref[...], wf_ref[...], preferred_element_type=F32)
    b = jnp.dot(nsa_ref[...], wn_ref[...], preferred_element_type=F32)
    mix = ga_ref[...].astype(F32) * a + gb_ref[...].astype(F32) * b
    y = jnp.dot(mix.astype(BF16), wo_ref[...], preferred_element_type=F32)
    x1 = x_ref[...] + _rms(y, gmix_ref[...])
    h = _rms(x1, gpre_ref[...]).astype(BF16)
    acc = jnp.zeros(x1.shape, F32)
    for c in range(d_ff // FF_CHUNK):
        cols = slice(c * FF_CHUNK, (c + 1) * FF_CHUNK)
        u = jnp.maximum(jnp.dot(h, wup_ref[:, cols], preferred_element_type=F32), 0.0)
        acc = acc + jnp.dot((u * u).astype(BF16), wdn_ref[cols, :], preferred_element_type=F32)
    o_ref[...] = x1 + _rms(acc, gpost_ref[...])


def _tail(x2, fox, nsa, sga, sgb, wf, wn, wo, wup, wdn, g_mix_post, g_mlp_pre, g_mlp_post):
    n_tok, d = x2.shape
    d_ff = wup.shape[1]
    tm = TAIL_TM
    assert n_tok % tm == 0 and d_ff % FF_CHUNK == 0

    def row(w):
        return pl.BlockSpec((tm, w), lambda i: (i, 0))

    return pl.pallas_call(
        functools.partial(_tail_kernel, d_ff=d_ff),
        grid=(n_tok // tm,),
        in_specs=[row(d), row(fox.shape[1]), row(nsa.shape[1]), row(d), row(d),
                  _resident(wf.shape), _resident(wn.shape), _resident(wo.shape),
                  _resident(wup.shape), _resident(wdn.shape),
                  _resident((1, d)), _resident((1, d)), _resident((1, d))],
        out_specs=row(d),
        out_shape=jax.ShapeDtypeStruct((n_tok, d), F32),
        compiler_params=pltpu.CompilerParams(dimension_semantics=("arbitrary",),
                                             vmem_limit_bytes=VMEM_LIMIT_BYTES),
        name="tail",
    )(x2, fox, nsa, sga, sgb, wf, wn, wo, wup, wdn, g_mix_post, g_mlp_pre, g_mlp_post)


def _layer(x, norm_mix_pre, norm_mix_post, norm_mlp_pre, norm_mlp_post, w_in, b_forget,
           cmp_pos_k, cmp_w1_k, cmp_w2_k, cmp_pos_v, cmp_w1_v, cmp_w2_v,
           w_fox_out, w_nsa_out, w_o, w_up, w_down):
    b, seq, d = x.shape
    n_tok = b * seq
    x2 = x.reshape(n_tok, d)
    w_small, w_ga, w_gb = _rearrange_w_in(w_in, d)
    bmisc = jnp.concatenate([b_forget.astype(F32), jnp.zeros((LANES - FOX_HEADS,), F32)]).reshape(1, LANES)
    (fq, fk, fv, nq, ksl, vsl, kwn, vwn, kc, vc, misc, gates, sga, sgb) = _proj(
        x2, norm_mix_pre.reshape(1, d), w_small, w_ga, w_gb, _rope_tables(seq), bmisc, seq)

    def seq3(t):
        return t.reshape(b, seq, t.shape[-1])

    fox = _fox(seq3(fq), seq3(fk), seq3(fv), _cumsum(seq3(misc)))

    kcmp, vcmp = _compress(kc, vc, b, seq, cmp_pos_k, cmp_w1_k, cmp_w2_k, cmp_pos_v, cmp_w1_v, cmp_w2_v)
    qs, ocmp = _cmpsel(seq3(nq), kcmp, vcmp)
    nsa = _nsa(qs, seq3(ksl), seq3(vsl), seq3(kwn), seq3(vwn), ocmp, seq3(gates))

    out = _tail(x2, fox.reshape(n_tok, FOX_WIDTH), nsa.reshape(n_tok, NSA_WIDTH), sga, sgb,
                w_fox_out.astype(BF16), w_nsa_out.astype(BF16), w_o.astype(BF16),
                w_up.astype(BF16), w_down.astype(BF16),
                norm_mix_post.reshape(1, d), norm_mlp_pre.reshape(1, d), norm_mlp_post.reshape(1, d))
    return out.reshape(b, seq, d)


def kernel(x, norm_mix_pre, norm_mix_post, norm_mlp_pre, norm_mlp_post, w_in, b_forget, cmp_pos_k, cmp_w1_k, cmp_w2_k, cmp_pos_v, cmp_w1_v, cmp_w2_v, w_fox_out, w_nsa_out, w_o, w_up, w_down):
    params = (norm_mix_pre, norm_mix_post, norm_mlp_pre, norm_mlp_post, w_in, b_forget,
              cmp_pos_k, cmp_w1_k, cmp_w2_k, cmp_pos_v, cmp_w1_v, cmp_w2_v,
              w_fox_out, w_nsa_out, w_o, w_up, w_down)
    for layer in range(w_in.shape[0]):
        x = _layer(x, *[p[layer] for p in params])
    return x
```

```python
import functools

import numpy as np
import jax
import jax.numpy as jnp
from jax import lax
from jax.experimental import pallas as pl
from jax.experimental.pallas import tpu as pltpu

HEAD_DIM = 64
FOX_HEADS = 8
NSA_HEADS = 8
NSA_KV_GROUPS = 2
NSA_HEADS_PER_GROUP = NSA_HEADS // NSA_KV_GROUPS
FOX_WIDTH = FOX_HEADS * HEAD_DIM
NSA_WIDTH = NSA_HEADS * HEAD_DIM
NSA_KV_WIDTH = NSA_KV_GROUPS * HEAD_DIM
ROPE_THETA = 10000.0
CMP_BLOCK = 32
CMP_STRIDE = 16
SEL_BLOCK = 64
SEL_TOPN = 8
WINDOW = 512
RMS_EPS = 1e-6
NEG_INF = -1e30
FORCED_BONUS = 1e4
ATTN_SCALE = HEAD_DIM ** -0.5
LOG2E = 1.4426950408889634
Q_SCALE = ATTN_SCALE * LOG2E

LANES = 128
VMEM_LIMIT_BYTES = 52 * 1024 * 1024
SEL_LANE0 = HEAD_DIM
MAX_SEL_BLOCKS = 32

F32 = jnp.float32
BF16 = jnp.bfloat16
HIGHEST = lax.Precision.HIGHEST

PROJ_TM = 512
FOX_TQ = 256
CMP_TQ = 256
NSA_TQ = 256
TAIL_TM = 512
FF_CHUNK = 1024
KV_STEP = 512
BIAS_LANE0 = HEAD_DIM


def _nt_dot(a, b):
    return lax.dot_general(a, b, (((1,), (1,)), ((), ())), preferred_element_type=F32)


def _sigmoid(t):
    return 1.0 / (1.0 + jnp.exp(-t))


def _resident(shape):
    nd = len(shape)
    return pl.BlockSpec(shape, lambda *_: (0,) * nd, pipeline_mode=pl.Buffered(1))


def _proj_layout():
    widths = [('fq', FOX_WIDTH), ('fk', FOX_WIDTH), ('fv', FOX_WIDTH),
              ('nq', NSA_WIDTH),
              ('ksl', NSA_KV_WIDTH), ('vsl', NSA_KV_WIDTH), ('kwn', NSA_KV_WIDTH), ('vwn', NSA_KV_WIDTH),
              ('kc', NSA_KV_WIDTH), ('vc', NSA_KV_WIDTH),
              ('misc', LANES)]
    out, c = {}, 0
    for name, w in widths:
        out[name] = (c, w)
        c += w
    return out, c


def _rearrange_w_in(w_in, d_model):
    splits = (FOX_WIDTH, FOX_WIDTH, FOX_WIDTH, FOX_HEADS,
              NSA_WIDTH, NSA_KV_WIDTH, NSA_KV_WIDTH, NSA_KV_WIDTH, NSA_KV_WIDTH, NSA_KV_WIDTH, NSA_KV_WIDTH,
              3 * NSA_HEADS, d_model, d_model)
    pts = [int(p) for p in np.cumsum(splits)[:-1]]
    fq, fk, fv, ff, nq, kc, vc, ksl, vsl, kwn, vwn, ng, ga, gb = jnp.split(w_in, pts, axis=-1)
    d = w_in.shape[0]
    misc = jnp.concatenate([ff, ng, jnp.zeros((d, LANES - FOX_HEADS - 3 * NSA_HEADS), w_in.dtype)], axis=-1)
    w_small = jnp.concatenate([fq, fk, fv, nq, ksl, vsl, kwn, vwn, kc, vc, misc], axis=-1)
    return w_small.astype(BF16), ga.astype(BF16), gb.astype(BF16)


def _rope_tables(seq):
    half = HEAD_DIM // 2
    inv = jnp.power(jnp.float32(ROPE_THETA), -jnp.arange(0, HEAD_DIM, 2, dtype=F32) / HEAD_DIM)
    ang = jnp.arange(seq, dtype=F32)[:, None] * inv[None, :]
    cos, sin = jnp.cos(ang), jnp.sin(ang)
    zero = jnp.zeros_like(sin)
    cos_t = jnp.concatenate([cos, cos, cos, cos], axis=-1)
    s_lo = jnp.concatenate([-sin, zero, -sin, zero], axis=-1)
    s_hi = jnp.concatenate([zero, sin, zero, sin], axis=-1)
    assert cos_t.shape == (seq, 4 * half)
    return cos_t, s_lo, s_hi


def _proj_kernel(x_ref, g_ref, w_ref, wga_ref, wgb_ref, cos_ref, slo_ref, shi_ref, bmisc_ref,
                 fq_ref, fk_ref, fv_ref, nq_ref, ksl_ref, vsl_ref, kwn_ref, vwn_ref,
                 kc_ref, vc_ref, misc_ref, gates_ref, ga_ref, gb_ref, *, tm, seq, layout):
    x = x_ref[...]
    h = x * lax.rsqrt(jnp.mean(x * x, axis=-1, keepdims=True) + RMS_EPS) * g_ref[...]
    h = h.astype(BF16)
    cos, slo, shi = cos_ref[...], slo_ref[...], shi_ref[...]
    half = HEAD_DIM // 2

    def mm(name):
        c0, n = layout[name]
        return jnp.dot(h, w_ref[:, c0:c0 + n], preferred_element_type=F32)

    def rope_block(t):
        return t * cos + pltpu.roll(t, LANES - half, axis=1) * slo + pltpu.roll(t, half, axis=1) * shi

    def rope(t):
        n = t.shape[1] // LANES
        return [rope_block(t[:, b * LANES:(b + 1) * LANES]) for b in range(n)]

    lane = lax.broadcasted_iota(jnp.int32, (tm, LANES), 1)
    left = lane < HEAD_DIM

    def blocks(t):
        return [t[:, b * LANES:(b + 1) * LANES] for b in range(t.shape[1] // LANES)]

    def split_heads(pairs, o_ref, scale, fill=0.0):
        for p, blk in enumerate(pairs):
            blk = blk * scale
            o_ref[:, (2 * p) * LANES:(2 * p + 1) * LANES] = jnp.where(left, blk, fill).astype(BF16)
            o_ref[:, (2 * p + 1) * LANES:(2 * p + 2) * LANES] = jnp.where(
                left, pltpu.roll(blk, HEAD_DIM, axis=1), fill).astype(BF16)

    def dup_heads(pairs, o_ref):
        for p, blk in enumerate(pairs):
            swapped = pltpu.roll(blk, HEAD_DIM, axis=1)
            o_ref[:, (2 * p) * LANES:(2 * p + 1) * LANES] = jnp.where(left, blk, swapped).astype(BF16)
            o_ref[:, (2 * p + 1) * LANES:(2 * p + 2) * LANES] = jnp.where(left, swapped, blk).astype(BF16)

    split_heads(blocks(mm('fq')), fq_ref, Q_SCALE)
    split_heads(blocks(mm('fk')), fk_ref, 1.0)
    fv = mm('fv')
    for p in range(FOX_HEADS // 2):
        blk = fv[:, p * LANES:(p + 1) * LANES]
        fv_ref[:, (2 * p) * LANES:(2 * p + 1) * LANES] = jnp.where(
            left, blk, (lane == HEAD_DIM).astype(F32)).astype(BF16)
        fv_ref[:, (2 * p + 1) * LANES:(2 * p + 2) * LANES] = jnp.where(
            left, (lane == 0).astype(F32), blk).astype(BF16)

    split_heads(rope(mm('nq')), nq_ref, Q_SCALE)

    pos = (pl.program_id(0) % (seq // tm)) * tm + lax.broadcasted_iota(jnp.int32, (tm, LANES), 0)
    onehot = ((lane - SEL_LANE0) == (pos // SEL_BLOCK)).astype(F32)
    split_heads(rope(mm('ksl')), ksl_ref, 1.0, fill=onehot)
    dup_heads(blocks(mm('vsl')), vsl_ref)
    split_heads(rope(mm('kwn')), kwn_ref, 1.0)
    dup_heads(blocks(mm('vwn')), vwn_ref)

    kc_ref[...] = rope(mm('kc'))[0]
    vc_ref[...] = mm('vc')

    t = mm('misc') + bmisc_ref[...]
    misc_ref[...] = -(jnp.maximum(-t, 0.0) + jnp.log(1.0 + jnp.exp(-jnp.abs(t))))
    per_group = 3 * NSA_HEADS_PER_GROUP
    for g in range(NSA_KV_GROUPS):
        gates_ref[:, g * LANES:(g + 1) * LANES] = _sigmoid(
            pltpu.roll(t, LANES - (FOX_HEADS + g * per_group), axis=1))

    ga_ref[...] = _sigmoid(jnp.dot(h, wga_ref[...], preferred_element_type=F32)).astype(BF16)
    gb_ref[...] = _sigmoid(jnp.dot(h, wgb_ref[...], preferred_element_type=F32)).astype(BF16)


def _proj(x2, g, w_small, w_ga, w_gb, tables, bmisc, seq):
    n_tok, d = x2.shape
    tm = PROJ_TM
    layout, n_cols = _proj_layout()
    assert n_tok % tm == 0 and seq % tm == 0 and w_small.shape == (d, n_cols)
    tiles_per_seq = seq // tm

    def row(w):
        return pl.BlockSpec((tm, w), lambda i: (i, 0))

    def tab():
        return pl.BlockSpec((tm, LANES), lambda i: (i % tiles_per_seq, 0))

    outs = [('fq', BF16), ('fk', BF16), ('fv', BF16), ('nq', BF16), ('ksl', BF16), ('vsl', BF16),
            ('kwn', BF16), ('vwn', BF16), ('kc', F32), ('vc', F32), ('misc', F32), ('gates', F32)]
    widths = dict(kc=NSA_KV_WIDTH, vc=NSA_KV_WIDTH, misc=LANES, gates=NSA_KV_GROUPS * LANES)
    widths.update({n: FOX_HEADS * LANES for n in ('fq', 'fk', 'fv')})
    widths.update(nq=NSA_HEADS * LANES, **{n: NSA_KV_GROUPS * LANES for n in ('ksl', 'vsl', 'kwn', 'vwn')})
    out_shape = [jax.ShapeDtypeStruct((n_tok, widths[n]), dt) for n, dt in outs]
    out_specs = [row(widths[n]) for n, _ in outs]
    out_shape += [jax.ShapeDtypeStruct((n_tok, d), BF16)] * 2
    out_specs += [row(d), row(d)]
    return pl.pallas_call(
        functools.partial(_proj_kernel, tm=tm, seq=seq, layout=layout),
        grid=(n_tok // tm,),
        in_specs=[row(d), _resident((1, d)), _resident(w_small.shape), _resident(w_ga.shape),
                  _resident(w_gb.shape), tab(), tab(), tab(), _resident((1, LANES))],
        out_specs=out_specs,
        out_shape=out_shape,
        compiler_params=pltpu.CompilerParams(dimension_semantics=("arbitrary",),
                                             vmem_limit_bytes=VMEM_LIMIT_BYTES),
        name="proj",
    )(x2, g, w_small, w_ga, w_gb, *tables, bmisc)


def _cumsum_kernel(m_ref, ccol_ref, *, seq):
    r = lax.broadcasted_iota(jnp.int32, (LANES, LANES), 0)
    c = lax.broadcasted_iota(jnp.int32, (LANES, LANES), 1)
    tri = (r >= c).astype(F32)
    carry = jnp.zeros((1, LANES), F32)
    for b in range(seq // LANES):
        blk = m_ref[0, b * LANES:(b + 1) * LANES, :]
        cs = jnp.dot(tri, blk, precision=HIGHEST, preferred_element_type=F32) + carry
        ccol_ref[0, b * LANES:(b + 1) * LANES, :] = cs
        carry = cs[LANES - 1:LANES, :]


def _cumsum(misc3):
    b, seq, _ = misc3.shape
    assert seq % LANES == 0
    return pl.pallas_call(
        functools.partial(_cumsum_kernel, seq=seq),
        grid=(b,),
        in_specs=[pl.BlockSpec((1, seq, LANES), lambda i: (i, 0, 0))],
        out_specs=pl.BlockSpec((1, seq, LANES), lambda i: (i, 0, 0)),
        out_shape=jax.ShapeDtypeStruct((b, seq, LANES), F32),
        compiler_params=pltpu.CompilerParams(dimension_semantics=("arbitrary",),
                                             vmem_limit_bytes=VMEM_LIMIT_BYTES),
        name="cumsum",
    )(misc3)


def _softmax_step(s, pv, m_ref, acc_ref, rows):
    m_prev = m_ref[rows]
    m_new = jnp.maximum(m_prev, jnp.max(s, axis=-1, keepdims=True))
    p = jnp.exp2(s - m_new).astype(BF16)
    acc_ref[rows] = jnp.exp2(m_prev - m_new) * acc_ref[rows] + pv(p)
    m_ref[rows] = m_new


def _mask_table(tq):
    d = np.arange(tq)[:, None] - np.arange(tq)[None, :]
    neg = np.full((tq, tq), NEG_INF, np.float32)
    zero = np.zeros((tq, tq), np.float32)
    return jnp.asarray(np.stack([np.where(d > 0, zero, neg), zero, np.where(d <= 0, zero, neg), neg]))


def _mask_bias(tab_ref, key_tile0, n_tiles, q_tile, reps):
    rows = []
    for u in range(n_tiles):
        t = tab_ref[jnp.clip(key_tile0 + u - q_tile + 2, 0, 3)]
        rows.append(jnp.concatenate([t] * reps, axis=1) if reps > 1 else t)
    return jnp.concatenate(rows, axis=0) if n_tiles > 1 else rows[0]


def _split3(c):
    hi = c.astype(BF16).astype(F32)
    r = c - hi
    mid = r.astype(BF16).astype(F32)
    lo = (r - mid).astype(BF16).astype(F32)
    return hi, mid, lo


def _softmax_step_t(s, pv, m_ref, acc_ref, idx):
    m_prev = m_ref[idx]
    m_new = jnp.maximum(m_prev, jnp.max(s, axis=0, keepdims=True))
    p = jnp.exp2(s - m_new).astype(BF16)
    acc_ref[idx] = jnp.exp2(m_prev - m_new) * acc_ref[idx] + pv(p)
    m_ref[idx] = m_new


def _fox_kernel(q_ref, k_ref, v_ref, ccol_ref, o_ref, kx_s, vt_s, qx_s, m_s, acc_s, *, tq, tk, seq):
    i = pl.program_id(1)
    lane = lax.broadcasted_iota(jnp.int32, (tq, LANES), 1)
    left = lane < HEAD_DIM
    b0 = BIAS_LANE0

    @pl.when(i == 0)
    def _build_keys():
        def chunk(r, carry):
            rows = pl.ds(pl.multiple_of(r * tq, tq), tq)
            his, mids, los = _split3(ccol_ref[0, rows, :] * LOG2E)
            for h in range(FOX_HEADS):
                cols = slice(h * LANES, (h + 1) * LANES)
                hi, mid, lo = his[:, h:h + 1], mids[:, h:h + 1], los[:, h:h + 1]
                e = jnp.where(lane < b0 + 3, 1.0,
                              jnp.where(lane == b0 + 3, -hi,
                                        jnp.where(lane == b0 + 4, -mid, jnp.where(lane == b0 + 5, -lo, 0.0))))
                kx_s[h, rows, :] = jnp.where(left, k_ref[0, rows, cols].astype(F32), e).astype(BF16)
            return carry

        lax.fori_loop(0, seq // tq, chunk, 0)

        def vchunk(c, carry):
            rows = pl.ds(pl.multiple_of(c * tk, tk), tk)
            for h in range(FOX_HEADS):
                cols = slice(h * LANES, (h + 1) * LANES)
                vt_s[c, h] = v_ref[0, rows, cols].astype(F32).T.astype(BF16)
            return carry

        lax.fori_loop(0, seq // tk, vchunk, 0)

    qrows = pl.ds(pl.multiple_of(i * tq, tq), tq)
    n_full = (i * tq) // tk
    row = lax.broadcasted_iota(jnp.int32, (LANES, tq), 0)
    n_pairs = FOX_HEADS // 2
    his, mids, los = _split3(ccol_ref[0, qrows, :] * LOG2E)
    for h in range(FOX_HEADS):
        hi, mid, lo = his[:, h:h + 1], mids[:, h:h + 1], los[:, h:h + 1]
        e = jnp.where(lane == b0, hi,
                      jnp.where(lane == b0 + 1, mid,
                                jnp.where(lane == b0 + 2, lo, jnp.where(lane < b0 + 6, 1.0, 0.0))))
        qx_s[h] = jnp.where(left, q_ref[0, :, h * LANES:(h + 1) * LANES].astype(F32), e).astype(BF16)
    m_s[...] = jnp.full(m_s.shape, NEG_INF, F32)
    acc_s[...] = jnp.zeros(acc_s.shape, F32)

    def scores(j, p, nk=tk):
        rows = pl.ds(pl.multiple_of(j * tk, tk), nk)
        return tuple(_nt_dot(kx_s[2 * p + slot, rows, :], qx_s[2 * p + slot]) for slot in range(2))

    def causal(j, ss, nk):
        visible = (j * tk + lax.broadcasted_iota(jnp.int32, (nk, tq), 0)
                   <= i * tq + lax.broadcasted_iota(jnp.int32, (nk, tq), 1))
        return tuple(jnp.where(visible, s, NEG_INF) for s in ss)

    def consume(j, p, ss, nk=tk):
        for slot in range(2):
            h = 2 * p + slot
            _softmax_step_t(ss[slot], lambda pr, h=h: jnp.dot(vt_s[j, h, :, 0:nk], pr, preferred_element_type=F32),
                            m_s, acc_s, h)

    def key_step(j, ss, masked, last, nk=tk):
        for p in range(n_pairs):
            if p + 1 < n_pairs:
                nxt = scores(j, p + 1, nk)
                if masked:
                    nxt = causal(j, nxt, nk)
            else:
                nxt = None if last else scores(j + 1, 0)
            consume(j, p, ss, nk)
            ss = nxt
        return ss

    ss = lax.fori_loop(0, n_full, lambda j, ss: key_step(j, ss, False, False), scores(0, 0))

    def tail_short(ss):
        key_step(n_full, causal(n_full, scores(n_full, 0, tq), tq), True, True, tq)
        return 0

    def tail_full(ss):
        key_step(n_full, causal(n_full, ss, tk), True, True)
        return 0

    lax.cond((i * tq) % tk == 0, tail_short, tail_full, ss)

    for hp in range(n_pairs):
        a0, a1 = acc_s[2 * hp], acc_s[2 * hp + 1]
        ot = jnp.where(row < HEAD_DIM, a0 / a0[HEAD_DIM:HEAD_DIM + 1, :], a1 / a1[0:1, :])
        o_ref[0, :, hp * LANES:(hp + 1) * LANES] = ot.T.astype(BF16)


def _fox(fq, fk, fv, ccol):
    b, seq, w = fq.shape
    tq, tk = FOX_TQ, KV_STEP
    assert seq % tq == 0 and seq % tk == 0 and tk % tq == 0 and w == FOX_HEADS * LANES
    whole = pl.BlockSpec((1, seq, w), lambda bi, i: (bi, 0, 0))
    return pl.pallas_call(
        functools.partial(_fox_kernel, tq=tq, tk=tk, seq=seq),
        grid=(b, seq // tq),
        in_specs=[pl.BlockSpec((1, tq, w), lambda bi, i: (bi, i, 0)), whole, whole,
                  pl.BlockSpec((1, seq, LANES), lambda bi, i: (bi, 0, 0))],
        out_specs=pl.BlockSpec((1, tq, FOX_WIDTH), lambda bi, i: (bi, i, 0)),
        out_shape=jax.ShapeDtypeStruct((b, seq, FOX_WIDTH), BF16),
        scratch_shapes=[pltpu.VMEM((FOX_HEADS, seq, LANES), BF16),
                        pltpu.VMEM((seq // tk, FOX_HEADS, LANES, tk), BF16),
                        pltpu.VMEM((FOX_HEADS, tq, LANES), BF16),
                        pltpu.VMEM((FOX_HEADS, 1, tq), F32), pltpu.VMEM((FOX_HEADS, LANES, tq), F32)],
        compiler_params=pltpu.CompilerParams(dimension_semantics=("arbitrary", "arbitrary"),
                                             vmem_limit_bytes=VMEM_LIMIT_BYTES),
        name="fox",
    )(fq, fk, fv, ccol)


def _compress_kernel(yk_ref, yv_ref, posk_ref, w1k_ref, w2k_ref, posv_ref, w1v_ref, w2v_ref,
                     kc_ref, vc_ref, *, nb):
    half = CMP_STRIDE * HEAD_DIM

    lane = lax.broadcasted_iota(jnp.int32, (nb, LANES), 1)
    left = lane < HEAD_DIM

    def regroup(x_ref):
        y = [[], []]
        for j in range(CMP_STRIDE // 2):
            xe = x_ref[0, pl.ds(2 * j, nb, stride=CMP_STRIDE), :]
            xo = x_ref[0, pl.ds(2 * j + 1, nb, stride=CMP_STRIDE), :]
            y[0].append(jnp.where(left, xe, pltpu.roll(xo, HEAD_DIM, axis=1)))
            y[1].append(jnp.where(left, pltpu.roll(xe, HEAD_DIM, axis=1), xo))
        return [jnp.concatenate(t, axis=1) for t in y]

    def comp(y, pos_ref, w1_ref, w2_ref):
        pos = pos_ref[...]
        a = jnp.dot(y + pos[:, :half], w1_ref[0:half, :], precision=HIGHEST, preferred_element_type=F32)
        bm = jnp.dot(y + pos[:, half:], w1_ref[half:2 * half, :], precision=HIGHEST,
                     preferred_element_type=F32)
        pre = a + pltpu.roll(bm, nb - 1, axis=0)
        return jnp.dot(pre * _sigmoid(pre), w2_ref[...], precision=HIGHEST, preferred_element_type=F32)

    for g, (yk, yv) in enumerate(zip(regroup(yk_ref), regroup(yv_ref))):
        kc_ref[0, g] = comp(yk, posk_ref, w1k_ref, w2k_ref).astype(BF16)
        vc_ref[0, g] = comp(yv, posv_ref, w1v_ref, w2v_ref).astype(BF16)


def _compress(kc, vc, b, seq, pos_k, w1_k, w2_k, pos_v, w1_v, w2_v):
    g = NSA_KV_GROUPS
    nb = seq // CMP_STRIDE
    half = CMP_STRIDE * HEAD_DIM
    assert CMP_BLOCK == 2 * CMP_STRIDE and seq % CMP_STRIDE == 0 and nb % 8 == 0
    assert g == 2 and kc.shape[-1] == LANES

    zeros1 = jnp.zeros((CMP_BLOCK * HEAD_DIM, HEAD_DIM), F32)
    zeros2 = jnp.zeros((HEAD_DIM, HEAD_DIM), F32)
    w1k = jnp.concatenate([w1_k, zeros1], axis=-1)
    w1v = jnp.concatenate([w1_v, zeros1], axis=-1)
    w2k = jnp.concatenate([jnp.concatenate([w2_k, zeros2], axis=-1), jnp.zeros((HEAD_DIM, LANES), F32)], axis=0)
    w2v = jnp.concatenate([jnp.concatenate([w2_v, w2_v], axis=-1), jnp.zeros((HEAD_DIM, LANES), F32)], axis=0)
    yspec = pl.BlockSpec((1, seq, LANES), lambda bi: (bi, 0, 0))
    ospec = pl.BlockSpec((1, g, nb, LANES), lambda bi: (bi, 0, 0, 0))
    return pl.pallas_call(
        functools.partial(_compress_kernel, nb=nb),
        grid=(b,),
        in_specs=[yspec, yspec,
                  _resident((1, 2 * half)), _resident(w1k.shape), _resident(w2k.shape),
                  _resident((1, 2 * half)), _resident(w1v.shape), _resident(w2v.shape)],
        out_specs=[ospec, ospec],
        out_shape=[jax.ShapeDtypeStruct((b, g, nb, LANES), BF16)] * 2,
        compiler_params=pltpu.CompilerParams(dimension_semantics=("arbitrary",),
                                             vmem_limit_bytes=VMEM_LIMIT_BYTES),
        name="compress",
    )(kc.reshape(b, seq, LANES), vc.reshape(b, seq, LANES), pos_k.reshape(1, 2 * half), w1k, w2k,
      pos_v.reshape(1, 2 * half), w1v, w2v)


def _overlap_matrix(nb, n_cmp, n_sel):
    ci = np.arange(nb)[:, None] * CMP_STRIDE
    sj = (np.arange(LANES)[None, :] - SEL_LANE0) * SEL_BLOCK
    jj = np.arange(LANES)[None, :] - SEL_LANE0
    ov = (ci < sj + SEL_BLOCK) & (ci + CMP_BLOCK > sj) & (jj >= 0) & (jj < n_sel) & (np.arange(nb)[:, None] < n_cmp)
    return jnp.asarray(ov.astype(np.float32))


def _cmpsel_kernel(q_ref, kc_ref, vc_ref, ovt_ref, qs_ref, ocmp_ref, vct_s, *, tq, n_cmp, nb):
    i = pl.program_id(1)
    n_g, hg = NSA_KV_GROUPS, NSA_HEADS_PER_GROUP

    @pl.when(i == 0)
    def _transpose_values():
        for g in range(n_g):
            vct_s[g] = vc_ref[0, g].astype(F32).T.astype(BF16)

    n_idx = lax.broadcasted_iota(jnp.int32, (nb, tq), 0)
    t_pos = i * tq + lax.broadcasted_iota(jnp.int32, (nb, tq), 1)
    cmask = (n_idx * CMP_STRIDE + (CMP_BLOCK - 1) <= t_pos) & (n_idx < n_cmp)
    top = lax.broadcasted_iota(jnp.int32, (LANES, tq), 0) < HEAD_DIM
    psums = []
    for g in range(n_g):
        kc = kc_ref[0, g]
        vct = vct_s[g]
        psum = jnp.zeros((nb, tq), F32)
        outs = []
        for h in range(hg):
            cols = slice((g * hg + h) * LANES, (g * hg + h + 1) * LANES)
            s = jnp.where(cmask, _nt_dot(kc, q_ref[0, :, cols]), NEG_INF)
            m = jnp.max(s, axis=0, keepdims=True)
            p = jnp.where(cmask, jnp.exp2(s - m), 0.0)
            p = p / jnp.maximum(jnp.sum(p, axis=0, keepdims=True), 1e-30)
            psum = psum + p
            outs.append(jnp.dot(vct, p.astype(BF16), preferred_element_type=F32))
        psums.append(psum)
        for hp in range(hg // 2):
            ocols = slice((g * (hg // 2) + hp) * LANES, (g * (hg // 2) + hp + 1) * LANES)
            ocmp_ref[0, :, ocols] = jnp.where(top, outs[2 * hp], outs[2 * hp + 1]).T

    wide = n_g * tq
    p_slc = jnp.dot(ovt_ref[...], jnp.concatenate(psums, axis=1), precision=HIGHEST, preferred_element_type=F32)
    sc = p_slc[SEL_LANE0:SEL_LANE0 + MAX_SEL_BLOCKS, :]
    j = lax.broadcasted_iota(jnp.int32, (MAX_SEL_BLOCKS, wide), 0)
    cur = (i * tq + (lax.broadcasted_iota(jnp.int32, (MAX_SEL_BLOCKS, wide), 1) & (tq - 1))) // SEL_BLOCK
    forced = (j == 0) | (j == cur) | (j == cur - 1)
    score = jnp.where(j <= cur, sc + FORCED_BONUS * forced.astype(F32), NEG_INF)
    j_f = j.astype(F32)
    sel = jnp.zeros((MAX_SEL_BLOCKS, wide), jnp.bool_)
    for _ in range(SEL_TOPN):
        mx = jnp.max(score, axis=0, keepdims=True)
        first = jnp.min(jnp.where(score == mx, j_f, 1e4), axis=0, keepdims=True)
        chosen = j_f == first
        sel = sel | chosen
        score = jnp.where(chosen, -jnp.inf, score)
    bias_t = jnp.concatenate([jnp.zeros((SEL_LANE0, wide), F32), jnp.where(sel, 0.0, NEG_INF),
                              jnp.zeros((LANES - SEL_LANE0 - MAX_SEL_BLOCKS, wide), F32)], axis=0)
    for g in range(n_g):
        bias = bias_t[:, g * tq:(g + 1) * tq].T
        for h in range(hg):
            cols = slice((g * hg + h) * LANES, (g * hg + h + 1) * LANES)
            qs_ref[0, :, cols] = (q_ref[0, :, cols].astype(F32) + bias).astype(BF16)


def _cmpsel(nq3, kcmp, vcmp):
    b, seq, _ = nq3.shape
    g, hg = NSA_KV_GROUPS, NSA_HEADS_PER_GROUP
    tq = CMP_TQ
    nb = kcmp.shape[2]
    n_cmp = (seq - CMP_BLOCK) // CMP_STRIDE + 1
    n_sel = seq // SEL_BLOCK
    assert seq % tq == 0 and n_sel <= MAX_SEL_BLOCKS and n_sel >= SEL_TOPN
    ov = _overlap_matrix(nb, n_cmp, n_sel).T
    assert tq & (tq - 1) == 0
    qspec = pl.BlockSpec((1, tq, g * hg * LANES), lambda bi, i: (bi, i, 0))
    cspec = pl.BlockSpec((1, g, nb, LANES), lambda bi, i: (bi, 0, 0, 0))
    return pl.pallas_call(
        functools.partial(_cmpsel_kernel, tq=tq, n_cmp=n_cmp, nb=nb),
        grid=(b, seq // tq),
        in_specs=[qspec, cspec, cspec, _resident(ov.shape)],
        out_specs=[qspec, pl.BlockSpec((1, tq, NSA_WIDTH), lambda bi, i: (bi, i, 0))],
        out_shape=[jax.ShapeDtypeStruct(nq3.shape, BF16),
                   jax.ShapeDtypeStruct((b, seq, NSA_WIDTH), F32)],
        scratch_shapes=[pltpu.VMEM((g, LANES, nb), BF16)],
        compiler_params=pltpu.CompilerParams(dimension_semantics=("arbitrary",) * 2,
                                             vmem_limit_bytes=VMEM_LIMIT_BYTES),
        name="cmpsel",
    )(nq3, kcmp, vcmp, ov)


def _nsa_kernel(qs_ref, ks_ref, vs_ref, kw_ref, vw_ref, ocmp_ref, gates_ref, tab_ref, o_ref,
                vse_s, vso_s, vwe_s, vwo_s, m_s, acc_s, *, tq, tk, seq):
    i = pl.program_id(1)
    n_g, hg = NSA_KV_GROUPS, NSA_HEADS_PER_GROUP
    rows_all = hg * tq
    half = rows_all // 2
    wt = WINDOW // tq + 1

    @pl.when(i == 0)
    def _build_values():
        lane = lax.broadcasted_iota(jnp.int32, (tk, LANES), 1)
        one_e = (lane == HEAD_DIM).astype(F32)
        one_o = (lane == 0).astype(F32)

        def chunk(c, carry):
            rows = pl.ds(pl.multiple_of(c * tk, tk), tk)
            for g in range(n_g):
                cols = slice(g * LANES, (g + 1) * LANES)
                v = vs_ref[0, rows, cols].astype(F32)
                vse_s[g, c] = jnp.where(lane < HEAD_DIM, v, one_e).T.astype(BF16)
                vso_s[g, c] = jnp.where(lane < HEAD_DIM, one_o, v).T.astype(BF16)
                w = vw_ref[0, rows, cols].astype(F32)
                we = jnp.where(lane < HEAD_DIM, w, one_e).T.astype(BF16)
                wo = jnp.where(lane < HEAD_DIM, one_o, w).T.astype(BF16)
                for t in range(tk // tq):
                    vwe_s[g, c * (tk // tq) + t] = we[:, t * tq:(t + 1) * tq]
                    vwo_s[g, c * (tk // tq) + t] = wo[:, t * tq:(t + 1) * tq]
            return carry

        lax.fori_loop(0, seq // tk, chunk, 0)

    order = tuple(range(0, hg, 2)) + tuple(range(1, hg, 2))
    q = [jnp.concatenate([qs_ref[0, :, (g * hg + h) * LANES:(g * hg + h + 1) * LANES] for h in order], axis=0)
         for g in range(n_g)]

    n_full = (i * tq) // tk
    m_s[...] = jnp.full(m_s.shape, NEG_INF, F32)
    acc_s[...] = jnp.zeros(acc_s.shape, F32)

    def scores(j, g):
        rows = pl.ds(pl.multiple_of(j * tk, tk), tk)
        return _nt_dot(ks_ref[0, rows, g * LANES:(g + 1) * LANES], q[g])

    def consume(j, g, s, masked):
        if masked:
            s = s + _mask_bias(tab_ref, j * (tk // tq), tk // tq, i, hg)
        _softmax_step_t(s, lambda p: jnp.concatenate(
            [jnp.dot(vse_s[g, j], p[:, :half], preferred_element_type=F32),
             jnp.dot(vso_s[g, j], p[:, half:], preferred_element_type=F32)], axis=1), m_s, acc_s, g)

    st = jnp.maximum(i - (wt - 1), 0)

    def win_scores(g):
        wrows = pl.ds(pl.multiple_of(st * tq, tq), wt * tq)
        return _nt_dot(kw_ref[0, wrows, g * LANES:(g + 1) * LANES], q[g]) + _mask_bias(tab_ref, st, wt, i, hg)

    def win_consume(g, s_win):
        p = jnp.exp2(s_win - jnp.max(s_win, axis=0, keepdims=True)).astype(BF16)
        a_win = jnp.zeros((LANES, rows_all), F32)
        for t in range(wt):
            pt = p[t * tq:(t + 1) * tq]
            a_win = a_win + jnp.concatenate(
                [jnp.dot(vwe_s[g, st + t], pt[:, :half], preferred_element_type=F32),
                 jnp.dot(vwo_s[g, st + t], pt[:, half:], preferred_element_type=F32)], axis=1)
        return a_win

    def key_step(j, s):
        for g in range(n_g):
            nxt = scores(j, g + 1) if g + 1 < n_g else scores(j + 1, 0)
            consume(j, g, s, False)
            s = nxt
        return s

    s = lax.fori_loop(0, n_full, key_step, scores(0, 0))

    for g in range(n_g):
        if g + 1 < n_g:
            nxt = scores(n_full, g + 1) + _mask_bias(tab_ref, n_full * (tk // tq), tk // tq, i, hg)
        else:
            nxt = win_scores(0)
        consume(n_full, g, s, g == 0)
        s = nxt
    a_win = []
    for g in range(n_g):
        nxt = win_scores(g + 1) if g + 1 < n_g else None
        a_win.append(win_consume(g, s))
        s = nxt

    lane = lax.broadcasted_iota(jnp.int32, (tq, LANES), 1)
    left = lane < HEAD_DIM
    top = lax.broadcasted_iota(jnp.int32, (LANES, tq), 0) < HEAD_DIM
    for g in range(n_g):
        gates = gates_ref[0, :, g * LANES:(g + 1) * LANES]
        a_sel = acc_s[g]
        for hp in range(hg // 2):
            h0, h1 = 2 * hp, 2 * hp + 1
            cols = slice((g * (hg // 2) + hp) * LANES, (g * (hg // 2) + hp + 1) * LANES)

            def pair(a):
                ae = a[:, hp * tq:(hp + 1) * tq]
                ao = a[:, half + hp * tq:half + (hp + 1) * tq]
                return jnp.where(top, ae / jnp.maximum(ae[HEAD_DIM:HEAD_DIM + 1, :], 1e-30),
                                 ao / jnp.maximum(ao[0:1, :], 1e-30)).T

            def gate(branch):
                return jnp.where(left, gates[:, 3 * h0 + branch:3 * h0 + branch + 1],
                                 gates[:, 3 * h1 + branch:3 * h1 + branch + 1])

            o = gate(0) * ocmp_ref[0, :, cols] + gate(1) * pair(a_sel) + gate(2) * pair(a_win[g])
            o_ref[0, :, cols] = o.astype(BF16)


def _nsa(qs, ksl, vsl, kwn, vwn, ocmp, gates):
    b, seq, _ = qs.shape
    g, hg = NSA_KV_GROUPS, NSA_HEADS_PER_GROUP
    tq, tk = NSA_TQ, KV_STEP
    assert seq % tk == 0 and tk % tq == 0 and WINDOW % tq == 0 and seq >= WINDOW + tq and tq & (tq - 1) == 0
    assert hg % 2 == 0
    assert WINDOW == 2 * tq
    kv = pl.BlockSpec((1, seq, g * LANES), lambda bi, i: (bi, 0, 0))
    ospec = pl.BlockSpec((1, tq, NSA_WIDTH), lambda bi, i: (bi, i, 0))
    return pl.pallas_call(
        functools.partial(_nsa_kernel, tq=tq, tk=tk, seq=seq),
        grid=(b, seq // tq),
        in_specs=[pl.BlockSpec((1, tq, g * hg * LANES), lambda bi, i: (bi, i, 0)), kv, kv, kv, kv,
                  ospec, pl.BlockSpec((1, tq, g * LANES), lambda bi, i: (bi, i, 0)), _resident((4, tq, tq))],
        out_specs=ospec,
        out_shape=jax.ShapeDtypeStruct((b, seq, NSA_WIDTH), BF16),
        scratch_shapes=[pltpu.VMEM((g, seq // tk, LANES, tk), BF16)] * 2
        + [pltpu.VMEM((g, seq // tq, LANES, tq), BF16)] * 2
        + [pltpu.VMEM((g, 1, hg * tq), F32), pltpu.VMEM((g, LANES, hg * tq), F32)],
        compiler_params=pltpu.CompilerParams(dimension_semantics=("arbitrary",) * 2,
                                             vmem_limit_bytes=VMEM_LIMIT_BYTES),
        name="nsa",
    )(qs, ksl, vsl, kwn, vwn, ocmp, gates, _mask_table(tq))


def _rms(y, g):
    return y * lax.rsqrt(jnp.mean(y * y, axis=-1, keepdims=True) + RMS_EPS) * g


def _tail_kernel(x_ref, fox_ref, nsa_ref, ga_ref, gb_ref, wf_ref, wn_ref, wo_ref, wup_ref, wdn_ref,
                 gmix_ref, gpre_ref, gpost_ref, o_ref, *, d_ff):
    a = jnp.dot(fox_ref[...], wf_ref[...], preferred_element_type=F32)
    b = jnp.dot(nsa_ref[...], wn_ref[...], preferred_element_type=F32)
    mix = ga_ref[...].astype(F32) * a + gb_ref[...].astype(F32) * b
    y = jnp.dot(mix.astype(BF16), wo_ref[...], preferred_element_type=F32)
    x1 = x_ref[...] + _rms(y, gmix_ref[...])
    h = _rms(x1, gpre_ref[...]).astype(BF16)
    acc = jnp.zeros(x1.shape, F32)
    for c in range(d_ff // FF_CHUNK):
        cols = slice(c * FF_CHUNK, (c + 1) * FF_CHUNK)
        u = jnp.maximum(jnp.dot(h, wup_ref[:, cols], preferred_element_type=F32), 0.0)
        acc = acc + jnp.dot((u * u).astype(BF16), wdn_ref[cols, :], preferred_element_type=F32)
    o_ref[...] = x1 + _rms(acc, gpost_ref[...])


def _tail(x2, fox, nsa, sga, sgb, wf, wn, wo, wup, wdn, g_mix_post, g_mlp_pre, g_mlp_post):
    n_tok, d = x2.shape
    d_ff = wup.shape[1]
    tm = TAIL_TM
    assert n_tok % tm == 0 and d_ff % FF_CHUNK == 0

    def row(w):
        return pl.BlockSpec((tm, w), lambda i: (i, 0))

    return pl.pallas_call(
        functools.partial(_tail_kernel, d_ff=d_ff),
        grid=(n_tok // tm,),
        in_specs=[row(d), row(fox.shape[1]), row(nsa.shape[1]), row(d), row(d),
                  _resident(wf.shape), _resident(wn.shape), _resident(wo.shape),
                  _resident(wup.shape), _resident(wdn.shape),
                  _resident((1, d)), _resident((1, d)), _resident((1, d))],
        out_specs=row(d),
        out_shape=jax.ShapeDtypeStruct((n_tok, d), F32),
        compiler_params=pltpu.CompilerParams(dimension_semantics=("arbitrary",),
                                             vmem_limit_bytes=VMEM_LIMIT_BYTES),
        name="tail",
    )(x2, fox, nsa, sga, sgb, wf, wn, wo, wup, wdn, g_mix_post, g_mlp_pre, g_mlp_post)


def _layer(x, norm_mix_pre, norm_mix_post, norm_mlp_pre, norm_mlp_post, w_in, b_forget,
           cmp_pos_k, cmp_w1_k, cmp_w2_k, cmp_pos_v, cmp_w1_v, cmp_w2_v,
           w_fox_out, w_nsa_out, w_o, w_up, w_down):
    b, seq, d = x.shape
    n_tok = b * seq
    x2 = x.reshape(n_tok, d)
    w_small, w_ga, w_gb = _rearrange_w_in(w_in, d)
    bmisc = jnp.concatenate([b_forget.astype(F32), jnp.zeros((LANES - FOX_HEADS,), F32)]).reshape(1, LANES)
    (fq, fk, fv, nq, ksl, vsl, kwn, vwn, kc, vc, misc, gates, sga, sgb) = _proj(
        x2, norm_mix_pre.reshape(1, d), w_small, w_ga, w_gb, _rope_tables(seq), bmisc, seq)

    def seq3(t):
        return t.reshape(b, seq, t.shape[-1])

    fox = _fox(seq3(fq), seq3(fk), seq3(fv), _cumsum(seq3(misc)))

    kcmp, vcmp = _compress(kc, vc, b, seq, cmp_pos_k, cmp_w1_k, cmp_w2_k, cmp_pos_v, cmp_w1_v, cmp_w2_v)
    qs, ocmp = _cmpsel(seq3(nq), kcmp, vcmp)
    nsa = _nsa(qs, seq3(ksl), seq3(vsl), seq3(kwn), seq3(vwn), ocmp, seq3(gates))

    out = _tail(x2, fox.reshape(n_tok, FOX_WIDTH), nsa.reshape(n_tok, NSA_WIDTH), sga, sgb,
                w_fox_out.astype(BF16), w_nsa_out.astype(BF16), w_o.astype(BF16),
                w_up.astype(BF16), w_down.astype(BF16),
                norm_mix_post.reshape(1, d), norm_mlp_pre.reshape(1, d), norm_mlp_post.reshape(1, d))
    return out.reshape(b, seq, d)


def kernel(x, norm_mix_pre, norm_mix_post, norm_mlp_pre, norm_mlp_post, w_in, b_forget, cmp_pos_k, cmp_w1_k, cmp_w2_k, cmp_pos_v, cmp_w1_v, cmp_w2_v, w_fox_out, w_nsa_out, w_o, w_up, w_down):
    params = (norm_mix_pre, norm_mix_post, norm_mlp_pre, norm_mlp_post, w_in, b_forget,
              cmp_pos_k, cmp_w1_k, cmp_w2_k, cmp_pos_v, cmp_w1_v, cmp_w2_v,
              w_fox_out, w_nsa_out, w_o, w_up, w_down)
    for layer in range(w_in.shape[0]):
        x = _layer(x, *[p[layer] for p in params])
    return x
```

```python
import functools

import numpy as np
import jax
import jax.numpy as jnp
from jax import lax
from jax.experimental import pallas as pl
from jax.experimental.pallas import tpu as pltpu

HEAD_DIM = 64
FOX_HEADS = 8
NSA_HEADS = 8
NSA_KV_GROUPS = 2
NSA_HEADS_PER_GROUP = NSA_HEADS // NSA_KV_GROUPS
FOX_WIDTH = FOX_HEADS * HEAD_DIM
NSA_WIDTH = NSA_HEADS * HEAD_DIM
NSA_KV_WIDTH = NSA_KV_GROUPS * HEAD_DIM
ROPE_THETA = 10000.0
CMP_BLOCK = 32
CMP_STRIDE = 16
SEL_BLOCK = 64
SEL_TOPN = 8
WINDOW = 512
RMS_EPS = 1e-6
NEG_INF = -1e30
FORCED_BONUS = 1e4
ATTN_SCALE = HEAD_DIM ** -0.5
LOG2E = 1.4426950408889634
Q_SCALE = ATTN_SCALE * LOG2E

LANES = 128
VMEM_LIMIT_BYTES = 52 * 1024 * 1024
SEL_LANE0 = HEAD_DIM
MAX_SEL_BLOCKS = 32

F32 = jnp.float32
BF16 = jnp.bfloat16
HIGHEST = lax.Precision.HIGHEST

PROJ_TM = 512
FOX_TQ = 256
CMP_TQ = 256
NSA_TQ = 256
TAIL_TM = 512
FF_CHUNK = 1024
KV_STEP = 512
BIAS_LANE0 = HEAD_DIM


def _nt_dot(a, b):
    return lax.dot_general(a, b, (((1,), (1,)), ((), ())), preferred_element_type=F32)


def _sigmoid(t):
    return 1.0 / (1.0 + jnp.exp(-t))


def _resident(shape):
    nd = len(shape)
    return pl.BlockSpec(shape, lambda *_: (0,) * nd, pipeline_mode=pl.Buffered(1))


def _proj_layout():
    widths = [('fq', FOX_WIDTH), ('fk', FOX_WIDTH), ('fv', FOX_WIDTH),
              ('nq', NSA_WIDTH),
              ('ksl', NSA_KV_WIDTH), ('vsl', NSA_KV_WIDTH), ('kwn', NSA_KV_WIDTH), ('vwn', NSA_KV_WIDTH),
              ('kc', NSA_KV_WIDTH), ('vc', NSA_KV_WIDTH),
              ('misc', LANES)]
    out, c = {}, 0
    for name, w in widths:
        out[name] = (c, w)
        c += w
    return out, c


def _rearrange_w_in(w_in, d_model):
    splits = (FOX_WIDTH, FOX_WIDTH, FOX_WIDTH, FOX_HEADS,
              NSA_WIDTH, NSA_KV_WIDTH, NSA_KV_WIDTH, NSA_KV_WIDTH, NSA_KV_WIDTH, NSA_KV_WIDTH, NSA_KV_WIDTH,
              3 * NSA_HEADS, d_model, d_model)
    pts = [int(p) for p in np.cumsum(splits)[:-1]]
    fq, fk, fv, ff, nq, kc, vc, ksl, vsl, kwn, vwn, ng, ga, gb = jnp.split(w_in, pts, axis=-1)
    d = w_in.shape[0]
    misc = jnp.concatenate([ff, ng, jnp.zeros((d, LANES - FOX_HEADS - 3 * NSA_HEADS), w_in.dtype)], axis=-1)
    w_small = jnp.concatenate([fq, fk, fv, nq, ksl, vsl, kwn, vwn, kc, vc, misc], axis=-1)
    return w_small.astype(BF16), ga.astype(BF16), gb.astype(BF16)


def _rope_tables(seq):
    half = HEAD_DIM // 2
    inv = jnp.power(jnp.float32(ROPE_THETA), -jnp.arange(0, HEAD_DIM, 2, dtype=F32) / HEAD_DIM)
    ang = jnp.arange(seq, dtype=F32)[:, None] * inv[None, :]
    cos, sin = jnp.cos(ang), jnp.sin(ang)
    zero = jnp.zeros_like(sin)
    cos_t = jnp.concatenate([cos, cos, cos, cos], axis=-1)
    s_lo = jnp.concatenate([-sin, zero, -sin, zero], axis=-1)
    s_hi = jnp.concatenate([zero, sin, zero, sin], axis=-1)
    assert cos_t.shape == (seq, 4 * half)
    return cos_t, s_lo, s_hi


def _proj_kernel(x_ref, g_ref, w_ref, wga_ref, wgb_ref, cos_ref, slo_ref, shi_ref, bmisc_ref,
                 fq_ref, fk_ref, fv_ref, nq_ref, ksl_ref, vsl_ref, kwn_ref, vwn_ref,
                 kc_ref, vc_ref, misc_ref, gates_ref, ga_ref, gb_ref, *, tm, seq, layout):
    x = x_ref[...]
    h = x * lax.rsqrt(jnp.mean(x * x, axis=-1, keepdims=True) + RMS_EPS) * g_ref[...]
    h = h.astype(BF16)
    cos, slo, shi = cos_ref[...], slo_ref[...], shi_ref[...]
    half = HEAD_DIM // 2

    def mm(name):
        c0, n = layout[name]
        return jnp.dot(h, w_ref[:, c0:c0 + n], preferred_element_type=F32)

    def rope_block(t):
        return t * cos + pltpu.roll(t, LANES - half, axis=1) * slo + pltpu.roll(t, half, axis=1) * shi

    def rope(t):
        n = t.shape[1] // LANES
        return [rope_block(t[:, b * LANES:(b + 1) * LANES]) for b in range(n)]

    lane = lax.broadcasted_iota(jnp.int32, (tm, LANES), 1)
    left = lane < HEAD_DIM

    def blocks(t):
        return [t[:, b * LANES:(b + 1) * LANES] for b in range(t.shape[1] // LANES)]

    def split_heads(pairs, o_ref, scale, fill=0.0):
        for p, blk in enumerate(pairs):
            blk = blk * scale
            o_ref[:, (2 * p) * LANES:(2 * p + 1) * LANES] = jnp.where(left, blk, fill).astype(BF16)
            o_ref[:, (2 * p + 1) * LANES:(2 * p + 2) * LANES] = jnp.where(
                left, pltpu.roll(blk, HEAD_DIM, axis=1), fill).astype(BF16)

    def dup_heads(pairs, o_ref):
        for p, blk in enumerate(pairs):
            swapped = pltpu.roll(blk, HEAD_DIM, axis=1)
            o_ref[:, (2 * p) * LANES:(2 * p + 1) * LANES] = jnp.where(left, blk, swapped).astype(BF16)
            o_ref[:, (2 * p + 1) * LANES:(2 * p + 2) * LANES] = jnp.where(left, swapped, blk).astype(BF16)

    split_heads(blocks(mm('fq')), fq_ref, Q_SCALE)
    split_heads(blocks(mm('fk')), fk_ref, 1.0)
    fv = mm('fv')
    for p in range(FOX_HEADS // 2):
        blk = fv[:, p * LANES:(p + 1) * LANES]
        fv_ref[:, (2 * p) * LANES:(2 * p + 1) * LANES] = jnp.where(
            left, blk, (lane == HEAD_DIM).astype(F32)).astype(BF16)
        fv_ref[:, (2 * p + 1) * LANES:(2 * p + 2) * LANES] = jnp.where(
            left, (lane == 0).astype(F32), blk).astype(BF16)

    split_heads(rope(mm('nq')), nq_ref, Q_SCALE)

    pos = (pl.program_id(0) % (seq // tm)) * tm + lax.broadcasted_iota(jnp.int32, (tm, LANES), 0)
    onehot = ((lane - SEL_LANE0) == (pos // SEL_BLOCK)).astype(F32)
    split_heads(rope(mm('ksl')), ksl_ref, 1.0, fill=onehot)
    dup_heads(blocks(mm('vsl')), vsl_ref)
    split_heads(rope(mm('kwn')), kwn_ref, 1.0)
    dup_heads(blocks(mm('vwn')), vwn_ref)

    kc_ref[...] = rope(mm('kc'))[0]
    vc_ref[...] = mm('vc')

    t = mm('misc') + bmisc_ref[...]
    misc_ref[...] = -(jnp.maximum(-t, 0.0) + jnp.log(1.0 + jnp.exp(-jnp.abs(t))))
    per_group = 3 * NSA_HEADS_PER_GROUP
    for g in range(NSA_KV_GROUPS):
        gates_ref[:, g * LANES:(g + 1) * LANES] = _sigmoid(
            pltpu.roll(t, LANES - (FOX_HEADS + g * per_group), axis=1))

    ga_ref[...] = _sigmoid(jnp.dot(h, wga_ref[...], preferred_element_type=F32)).astype(BF16)
    gb_ref[...] = _sigmoid(jnp.dot(h, wgb_ref[...], preferred_element_type=F32)).astype(BF16)


def _proj(x2, g, w_small, w_ga, w_gb, tables, bmisc, seq):
    n_tok, d = x2.shape
    tm = PROJ_TM
    layout, n_cols = _proj_layout()
    assert n_tok % tm == 0 and seq % tm == 0 and w_small.shape == (d, n_cols)
    tiles_per_seq = seq // tm

    def row(w):
        return pl.BlockSpec((tm, w), lambda i: (i, 0))

    def tab():
        return pl.BlockSpec((tm, LANES), lambda i: (i % tiles_per_seq, 0))

    outs = [('fq', BF16), ('fk', BF16), ('fv', BF16), ('nq', BF16), ('ksl', BF16), ('vsl', BF16),
            ('kwn', BF16), ('vwn', BF16), ('kc', F32), ('vc', F32), ('misc', F32), ('gates', F32)]
    widths = dict(kc=NSA_KV_WIDTH, vc=NSA_KV_WIDTH, misc=LANES, gates=NSA_KV_GROUPS * LANES)
    widths.update({n: FOX_HEADS * LANES for n in ('fq', 'fk', 'fv')})
    widths.update(nq=NSA_HEADS * LANES, **{n: NSA_KV_GROUPS * LANES for n in ('ksl', 'vsl', 'kwn', 'vwn')})
    out_shape = [jax.ShapeDtypeStruct((n_tok, widths[n]), dt) for n, dt in outs]
    out_specs = [row(widths[n]) for n, _ in outs]
    out_shape += [jax.ShapeDtypeStruct((n_tok, d), BF16)] * 2
    out_specs += [row(d), row(d)]
    return pl.pallas_call(
        functools.partial(_proj_kernel, tm=tm, seq=seq, layout=layout),
        grid=(n_tok // tm,),
        in_specs=[row(d), _resident((1, d)), _resident(w_small.shape), _resident(w_ga.shape),
                  _resident(w_gb.shape), tab(), tab(), tab(), _resident((1, LANES))],
        out_specs=out_specs,
        out_shape=out_shape,
        compiler_params=pltpu.CompilerParams(dimension_semantics=("arbitrary",),
                                             vmem_limit_bytes=VMEM_LIMIT_BYTES),
        name="proj",
    )(x2, g, w_small, w_ga, w_gb, *tables, bmisc)


def _cumsum_kernel(m_ref, ccol_ref, *, seq):
    r = lax.broadcasted_iota(jnp.int32, (LANES, LANES), 0)
    c = lax.broadcasted_iota(jnp.int32, (LANES, LANES), 1)
    tri = (r >= c).astype(F32)
    carry = jnp.zeros((1, LANES), F32)
    for b in range(seq // LANES):
        blk = m_ref[0, b * LANES:(b + 1) * LANES, :]
        cs = jnp.dot(tri, blk, precision=HIGHEST, preferred_element_type=F32) + carry
        ccol_ref[0, b * LANES:(b + 1) * LANES, :] = cs
        carry = cs[LANES - 1:LANES, :]


def _cumsum(misc3):
    b, seq, _ = misc3.shape
    assert seq % LANES == 0
    return pl.pallas_call(
        functools.partial(_cumsum_kernel, seq=seq),
        grid=(b,),
        in_specs=[pl.BlockSpec((1, seq, LANES), lambda i: (i, 0, 0))],
        out_specs=pl.BlockSpec((1, seq, LANES), lambda i: (i, 0, 0)),
        out_shape=jax.ShapeDtypeStruct((b, seq, LANES), F32),
        compiler_params=pltpu.CompilerParams(dimension_semantics=("arbitrary",),
                                             vmem_limit_bytes=VMEM_LIMIT_BYTES),
        name="cumsum",
    )(misc3)


def _softmax_step(s, pv, m_ref, acc_ref, rows):
    m_prev = m_ref[rows]
    m_new = jnp.maximum(m_prev, jnp.max(s, axis=-1, keepdims=True))
    p = jnp.exp2(s - m_new).astype(BF16)
    acc_ref[rows] = jnp.exp2(m_prev - m_new) * acc_ref[rows] + pv(p)
    m_ref[rows] = m_new


def _mask_table(tq):
    d = np.arange(tq)[:, None] - np.arange(tq)[None, :]
    neg = np.full((tq, tq), NEG_INF, np.float32)
    zero = np.zeros((tq, tq), np.float32)
    return jnp.asarray(np.stack([np.where(d > 0, zero, neg), zero, np.where(d <= 0, zero, neg), neg]))


def _mask_bias(tab_ref, key_tile0, n_tiles, q_tile, reps):
    rows = []
    for u in range(n_tiles):
        t = tab_ref[jnp.clip(key_tile0 + u - q_tile + 2, 0, 3)]
        rows.append(jnp.concatenate([t] * reps, axis=1) if reps > 1 else t)
    return jnp.concatenate(rows, axis=0) if n_tiles > 1 else rows[0]


def _split3(c):
    hi = c.astype(BF16).astype(F32)
    r = c - hi
    mid = r.astype(BF16).astype(F32)
    lo = (r - mid).astype(BF16).astype(F32)
    return hi, mid, lo


def _softmax_step_t(s, pv, m_ref, acc_ref, idx):
    m_prev = m_ref[idx]
    m_new = jnp.maximum(m_prev, jnp.max(s, axis=0, keepdims=True))
    p = jnp.exp2(s - m_new).astype(BF16)
    acc_ref[idx] = jnp.exp2(m_prev - m_new) * acc_ref[idx] + pv(p)
    m_ref[idx] = m_new


def _fox_kernel(q_ref, k_ref, v_ref, ccol_ref, o_ref, kx_s, vt_s, qx_s, m_s, acc_s, *, tq, tk, seq):
    i = pl.program_id(1)
    lane = lax.broadcasted_iota(jnp.int32, (tq, LANES), 1)
    left = lane < HEAD_DIM
    b0 = BIAS_LANE0

    @pl.when(i == 0)
    def _build_keys():
        def chunk(r, carry):
            rows = pl.ds(pl.multiple_of(r * tq, tq), tq)
            his, mids, los = _split3(ccol_ref[0, rows, :] * LOG2E)
            for h in range(FOX_HEADS):
                cols = slice(h * LANES, (h + 1) * LANES)
                hi, mid, lo = his[:, h:h + 1], mids[:, h:h + 1], los[:, h:h + 1]
                e = jnp.where(lane < b0 + 3, 1.0,
                              jnp.where(lane == b0 + 3, -hi,
                                        jnp.where(lane == b0 + 4, -mid, jnp.where(lane == b0 + 5, -lo, 0.0))))
                kx_s[h, rows, :] = jnp.where(left, k_ref[0, rows, cols].astype(F32), e).astype(BF16)
            return carry

        lax.fori_loop(0, seq // tq, chunk, 0)

        def vchunk(c, carry):
            rows = pl.ds(pl.multiple_of(c * tk, tk), tk)
            for h in range(FOX_HEADS):
                cols = slice(h * LANES, (h + 1) * LANES)
                vt_s[c, h] = v_ref[0, rows, cols].astype(F32).T.astype(BF16)
            return carry

        lax.fori_loop(0, seq // tk, vchunk, 0)

    qrows = pl.ds(pl.multiple_of(i * tq, tq), tq)
    n_full = (i * tq) // tk
    row = lax.broadcasted_iota(jnp.int32, (LANES, tq), 0)
    n_pairs = FOX_HEADS // 2
    his, mids, los = _split3(ccol_ref[0, qrows, :] * LOG2E)
    for h in range(FOX_HEADS):
        hi, mid, lo = his[:, h:h + 1], mids[:, h:h + 1], los[:, h:h + 1]
        e = jnp.where(lane == b0, hi,
                      jnp.where(lane == b0 + 1, mid,
                                jnp.where(lane == b0 + 2, lo, jnp.where(lane < b0 + 6, 1.0, 0.0))))
        qx_s[h] = jnp.where(left, q_ref[0, :, h * LANES:(h + 1) * LANES].astype(F32), e).astype(BF16)
    m_s[...] = jnp.full(m_s.shape, NEG_INF, F32)
    acc_s[...] = jnp.zeros(acc_s.shape, F32)

    def scores(j, p, nk=tk):
        rows = pl.ds(pl.multiple_of(j * tk, tk), nk)
        return tuple(_nt_dot(kx_s[2 * p + slot, rows, :], qx_s[2 * p + slot]) for slot in range(2))

    def causal(j, ss, nk):
        visible = (j * tk + lax.broadcasted_iota(jnp.int32, (nk, tq), 0)
                   <= i * tq + lax.broadcasted_iota(jnp.int32, (nk, tq), 1))
        return tuple(jnp.where(visible, s, NEG_INF) for s in ss)

    def consume(j, p, ss, nk=tk):
        for slot in range(2):
            h = 2 * p + slot
            _softmax_step_t(ss[slot], lambda pr, h=h: jnp.dot(vt_s[j, h, :, 0:nk], pr, preferred_element_type=F32),
                            m_s, acc_s, h)

    def key_step(j, ss, masked, last, nk=tk):
        for p in range(n_pairs):
            if p + 1 < n_pairs:
                nxt = scores(j, p + 1, nk)
                if masked:
                    nxt = causal(j, nxt, nk)
            else:
                nxt = None if last else scores(j + 1, 0)
            consume(j, p, ss, nk)
            ss = nxt
        return ss

    ss = lax.fori_loop(0, n_full, lambda j, ss: key_step(j, ss, False, False), scores(0, 0))

    def tail_short(ss):
        key_step(n_full, causal(n_full, scores(n_full, 0, tq), tq), True, True, tq)
        return 0

    def tail_full(ss):
        key_step(n_full, causal(n_full, ss, tk), True, True)
        return 0

    lax.cond((i * tq) % tk == 0, tail_short, tail_full, ss)

    for hp in range(n_pairs):
        a0, a1 = acc_s[2 * hp], acc_s[2 * hp + 1]
        ot = jnp.where(row < HEAD_DIM, a0 / a0[HEAD_DIM:HEAD_DIM + 1, :], a1 / a1[0:1, :])
        o_ref[0, :, hp * LANES:(hp + 1) * LANES] = ot.T.astype(BF16)


def _fox(fq, fk, fv, ccol):
    b, seq, w = fq.shape
    tq, tk = FOX_TQ, KV_STEP
    assert seq % tq == 0 and seq % tk == 0 and tk % tq == 0 and w == FOX_HEADS * LANES
    whole = pl.BlockSpec((1, seq, w), lambda bi, i: (bi, 0, 0))
    return pl.pallas_call(
        functools.partial(_fox_kernel, tq=tq, tk=tk, seq=seq),
        grid=(b, seq // tq),
        in_specs=[pl.BlockSpec((1, tq, w), lambda bi, i: (bi, i, 0)), whole, whole,
                  pl.BlockSpec((1, seq, LANES), lambda bi, i: (bi, 0, 0))],
        out_specs=pl.BlockSpec((1, tq, FOX_WIDTH), lambda bi, i: (bi, i, 0)),
        out_shape=jax.ShapeDtypeStruct((b, seq, FOX_WIDTH), BF16),
        scratch_shapes=[pltpu.VMEM((FOX_HEADS, seq, LANES), BF16),
                        pltpu.VMEM((seq // tk, FOX_HEADS, LANES, tk), BF16),
                        pltpu.VMEM((FOX_HEADS, tq, LANES), BF16),
                        pltpu.VMEM((FOX_HEADS, 1, tq), F32), pltpu.VMEM((FOX_HEADS, LANES, tq), F32)],
        compiler_params=pltpu.CompilerParams(dimension_semantics=("arbitrary", "arbitrary"),
                                             vmem_limit_bytes=VMEM_LIMIT_BYTES),
        name="fox",
    )(fq, fk, fv, ccol)


def _compress_kernel(yk_ref, yv_ref, posk_ref, w1k_ref, w2k_ref, posv_ref, w1v_ref, w2v_ref,
                     kc_ref, vc_ref, *, nb):
    half = CMP_STRIDE * HEAD_DIM

    lane = lax.broadcasted_iota(jnp.int32, (nb, LANES), 1)
    left = lane < HEAD_DIM

    def regroup(x_ref):
        y = [[], []]
        for j in range(CMP_STRIDE // 2):
            xe = x_ref[0, pl.ds(2 * j, nb, stride=CMP_STRIDE), :]
            xo = x_ref[0, pl.ds(2 * j + 1, nb, stride=CMP_STRIDE), :]
            y[0].append(jnp.where(left, xe, pltpu.roll(xo, HEAD_DIM, axis=1)))
            y[1].append(jnp.where(left, pltpu.roll(xe, HEAD_DIM, axis=1), xo))
        return [jnp.concatenate(t, axis=1) for t in y]

    def comp(y, pos_ref, w1_ref, w2_ref):
        pos = pos_ref[...]
        def dot3(x, r0):
            xh = x.astype(BF16)
            xl = (x - xh.astype(F32)).astype(BF16)
            wh, wl = w1_ref[0, r0:r0 + half, :], w1_ref[1, r0:r0 + half, :]
            return (jnp.dot(xh, wh, preferred_element_type=F32) + jnp.dot(xh, wl, preferred_element_type=F32)
                    + jnp.dot(xl, wh, preferred_element_type=F32))

        a = dot3(y + pos[:, :half], 0)
        bm = dot3(y + pos[:, half:], half)
        pre = a + pltpu.roll(bm, nb - 1, axis=0)
        return jnp.dot(pre * _sigmoid(pre), w2_ref[...], precision=HIGHEST, preferred_element_type=F32)

    for g, (yk, yv) in enumerate(zip(regroup(yk_ref), regroup(yv_ref))):
        kc_ref[0, g] = comp(yk, posk_ref, w1k_ref, w2k_ref).astype(BF16)
        vc_ref[0, g] = comp(yv, posv_ref, w1v_ref, w2v_ref).astype(BF16)


def _compress(kc, vc, b, seq, pos_k, w1_k, w2_k, pos_v, w1_v, w2_v):
    g = NSA_KV_GROUPS
    nb = seq // CMP_STRIDE
    half = CMP_STRIDE * HEAD_DIM
    assert CMP_BLOCK == 2 * CMP_STRIDE and seq % CMP_STRIDE == 0 and nb % 8 == 0
    assert g == 2 and kc.shape[-1] == LANES

    zeros1 = jnp.zeros((CMP_BLOCK * HEAD_DIM, HEAD_DIM), F32)
    zeros2 = jnp.zeros((HEAD_DIM, HEAD_DIM), F32)
    def hi_lo(w):
        w = jnp.concatenate([w, zeros1], axis=-1)
        hi = w.astype(BF16)
        return jnp.stack([hi, (w - hi.astype(F32)).astype(BF16)])

    w1k, w1v = hi_lo(w1_k), hi_lo(w1_v)
    w2k = jnp.concatenate([jnp.concatenate([w2_k, zeros2], axis=-1), jnp.zeros((HEAD_DIM, LANES), F32)], axis=0)
    w2v = jnp.concatenate([jnp.concatenate([w2_v, w2_v], axis=-1), jnp.zeros((HEAD_DIM, LANES), F32)], axis=0)
    yspec = pl.BlockSpec((1, seq, LANES), lambda bi: (bi, 0, 0))
    ospec = pl.BlockSpec((1, g, nb, LANES), lambda bi: (bi, 0, 0, 0))
    return pl.pallas_call(
        functools.partial(_compress_kernel, nb=nb),
        grid=(b,),
        in_specs=[yspec, yspec,
                  _resident((1, 2 * half)), _resident(w1k.shape), _resident(w2k.shape),
                  _resident((1, 2 * half)), _resident(w1v.shape), _resident(w2v.shape)],
        out_specs=[ospec, ospec],
        out_shape=[jax.ShapeDtypeStruct((b, g, nb, LANES), BF16)] * 2,
        compiler_params=pltpu.CompilerParams(dimension_semantics=("arbitrary",),
                                             vmem_limit_bytes=VMEM_LIMIT_BYTES),
        name="compress",
    )(kc.reshape(b, seq, LANES), vc.reshape(b, seq, LANES), pos_k.reshape(1, 2 * half), w1k, w2k,
      pos_v.reshape(1, 2 * half), w1v, w2v)


def _overlap_matrix(nb, n_cmp, n_sel):
    ci = np.arange(nb)[:, None] * CMP_STRIDE
    sj = (np.arange(LANES)[None, :] - SEL_LANE0) * SEL_BLOCK
    jj = np.arange(LANES)[None, :] - SEL_LANE0
    ov = (ci < sj + SEL_BLOCK) & (ci + CMP_BLOCK > sj) & (jj >= 0) & (jj < n_sel) & (np.arange(nb)[:, None] < n_cmp)
    return jnp.asarray(ov.astype(np.float32))


def _cmpsel_kernel(q_ref, kc_ref, vc_ref, ovt_ref, qs_ref, ocmp_ref, vct_s, *, tq, n_cmp, nb):
    i = pl.program_id(1)
    n_g, hg = NSA_KV_GROUPS, NSA_HEADS_PER_GROUP

    @pl.when(i == 0)
    def _transpose_values():
        for g in range(n_g):
            vct_s[g] = vc_ref[0, g].astype(F32).T.astype(BF16)

    n_idx = lax.broadcasted_iota(jnp.int32, (nb, tq), 0)
    t_pos = i * tq + lax.broadcasted_iota(jnp.int32, (nb, tq), 1)
    cmask = (n_idx * CMP_STRIDE + (CMP_BLOCK - 1) <= t_pos) & (n_idx < n_cmp)
    top = lax.broadcasted_iota(jnp.int32, (LANES, tq), 0) < HEAD_DIM
    psums = []
    for g in range(n_g):
        kc = kc_ref[0, g]
        vct = vct_s[g]
        psum = jnp.zeros((nb, tq), F32)
        outs = []
        for h in range(hg):
            cols = slice((g * hg + h) * LANES, (g * hg + h + 1) * LANES)
            s = jnp.where(cmask, _nt_dot(kc, q_ref[0, :, cols]), NEG_INF)
            m = jnp.max(s, axis=0, keepdims=True)
            p = jnp.where(cmask, jnp.exp2(s - m), 0.0)
            p = p / jnp.maximum(jnp.sum(p, axis=0, keepdims=True), 1e-30)
            psum = psum + p
            outs.append(jnp.dot(vct, p.astype(BF16), preferred_element_type=F32))
        psums.append(psum)
        for hp in range(hg // 2):
            ocols = slice((g * (hg // 2) + hp) * LANES, (g * (hg // 2) + hp + 1) * LANES)
            ocmp_ref[0, :, ocols] = jnp.where(top, outs[2 * hp], outs[2 * hp + 1]).T

    wide = n_g * tq
    ovt = ovt_ref[...]
    p_slc = sum(jnp.dot(ovt, piece.astype(BF16), preferred_element_type=F32)
                for piece in _split3(jnp.concatenate(psums, axis=1)))
    sc = p_slc[SEL_LANE0:SEL_LANE0 + MAX_SEL_BLOCKS, :]
    j = lax.broadcasted_iota(jnp.int32, (MAX_SEL_BLOCKS, wide), 0)
    cur = (i * tq + (lax.broadcasted_iota(jnp.int32, (MAX_SEL_BLOCKS, wide), 1) & (tq - 1))) // SEL_BLOCK
    forced = (j == 0) | (j == cur) | (j == cur - 1)
    score = jnp.where(j <= cur, sc + FORCED_BONUS * forced.astype(F32), NEG_INF)
    j_f = j.astype(F32)
    sel = jnp.zeros((MAX_SEL_BLOCKS, wide), jnp.bool_)
    for _ in range(SEL_TOPN):
        mx = jnp.max(score, axis=0, keepdims=True)
        first = jnp.min(jnp.where(score == mx, j_f, 1e4), axis=0, keepdims=True)
        chosen = j_f == first
        sel = sel | chosen
        score = jnp.where(chosen, -jnp.inf, score)
    bias_t = jnp.concatenate([jnp.zeros((SEL_LANE0, wide), F32), jnp.where(sel, 0.0, NEG_INF),
                              jnp.zeros((LANES - SEL_LANE0 - MAX_SEL_BLOCKS, wide), F32)], axis=0)
    for g in range(n_g):
        bias = bias_t[:, g * tq:(g + 1) * tq].T
        for h in range(hg):
            cols = slice((g * hg + h) * LANES, (g * hg + h + 1) * LANES)
            qs_ref[0, :, cols] = (q_ref[0, :, cols].astype(F32) + bias).astype(BF16)


def _cmpsel(nq3, kcmp, vcmp):
    b, seq, _ = nq3.shape
    g, hg = NSA_KV_GROUPS, NSA_HEADS_PER_GROUP
    tq = CMP_TQ
    nb = kcmp.shape[2]
    n_cmp = (seq - CMP_BLOCK) // CMP_STRIDE + 1
    n_sel = seq // SEL_BLOCK
    assert seq % tq == 0 and n_sel <= MAX_SEL_BLOCKS and n_sel >= SEL_TOPN
    ov = _overlap_matrix(nb, n_cmp, n_sel).T.astype(BF16)
    assert tq & (tq - 1) == 0
    qspec = pl.BlockSpec((1, tq, g * hg * LANES), lambda bi, i: (bi, i, 0))
    cspec = pl.BlockSpec((1, g, nb, LANES), lambda bi, i: (bi, 0, 0, 0))
    return pl.pallas_call(
        functools.partial(_cmpsel_kernel, tq=tq, n_cmp=n_cmp, nb=nb),
        grid=(b, seq // tq),
        in_specs=[qspec, cspec, cspec, _resident(ov.shape)],
        out_specs=[qspec, pl.BlockSpec((1, tq, NSA_WIDTH), lambda bi, i: (bi, i, 0))],
        out_shape=[jax.ShapeDtypeStruct(nq3.shape, BF16),
                   jax.ShapeDtypeStruct((b, seq, NSA_WIDTH), F32)],
        scratch_shapes=[pltpu.VMEM((g, LANES, nb), BF16)],
        compiler_params=pltpu.CompilerParams(dimension_semantics=("arbitrary",) * 2,
                                             vmem_limit_bytes=VMEM_LIMIT_BYTES),
        name="cmpsel",
    )(nq3, kcmp, vcmp, ov)


def _nsa_kernel(qs_ref, ks_ref, vs_ref, kw_ref, vw_ref, ocmp_ref, gates_ref, tab_ref, o_ref,
                vse_s, vso_s, vwe_s, vwo_s, m_s, acc_s, *, tq, tk, seq):
    i = pl.program_id(1)
    n_g, hg = NSA_KV_GROUPS, NSA_HEADS_PER_GROUP
    rows_all = hg * tq
    half = rows_all // 2
    wt = WINDOW // tq + 1

    @pl.when(i == 0)
    def _build_values():
        lane = lax.broadcasted_iota(jnp.int32, (tk, LANES), 1)
        one_e = (lane == HEAD_DIM).astype(F32)
        one_o = (lane == 0).astype(F32)

        def chunk(c, carry):
            rows = pl.ds(pl.multiple_of(c * tk, tk), tk)
            for g in range(n_g):
                cols = slice(g * LANES, (g + 1) * LANES)
                v = vs_ref[0, rows, cols].astype(F32)
                vse_s[g, c] = jnp.where(lane < HEAD_DIM, v, one_e).T.astype(BF16)
                vso_s[g, c] = jnp.where(lane < HEAD_DIM, one_o, v).T.astype(BF16)
                w = vw_ref[0, rows, cols].astype(F32)
                we = jnp.where(lane < HEAD_DIM, w, one_e).T.astype(BF16)
                wo = jnp.where(lane < HEAD_DIM, one_o, w).T.astype(BF16)
                for t in range(tk // tq):
                    vwe_s[g, c * (tk // tq) + t] = we[:, t * tq:(t + 1) * tq]
                    vwo_s[g, c * (tk // tq) + t] = wo[:, t * tq:(t + 1) * tq]
            return carry

        lax.fori_loop(0, seq // tk, chunk, 0)

    order = tuple(range(0, hg, 2)) + tuple(range(1, hg, 2))
    q = [jnp.concatenate([qs_ref[0, :, (g * hg + h) * LANES:(g * hg + h + 1) * LANES] for h in order], axis=0)
         for g in range(n_g)]

    n_full = (i * tq) // tk
    m_s[...] = jnp.full(m_s.shape, NEG_INF, F32)
    acc_s[...] = jnp.zeros(acc_s.shape, F32)

    def scores(j, g):
        rows = pl.ds(pl.multiple_of(j * tk, tk), tk)
        return _nt_dot(ks_ref[0, rows, g * LANES:(g + 1) * LANES], q[g])

    def consume(j, g, s, masked):
        if masked:
            s = s + _mask_bias(tab_ref, j * (tk // tq), tk // tq, i, hg)
        _softmax_step_t(s, lambda p: jnp.concatenate(
            [jnp.dot(vse_s[g, j], p[:, :half], preferred_element_type=F32),
             jnp.dot(vso_s[g, j], p[:, half:], preferred_element_type=F32)], axis=1), m_s, acc_s, g)

    st = jnp.maximum(i - (wt - 1), 0)

    def win_scores(g):
        wrows = pl.ds(pl.multiple_of(st * tq, tq), wt * tq)
        return _nt_dot(kw_ref[0, wrows, g * LANES:(g + 1) * LANES], q[g]) + _mask_bias(tab_ref, st, wt, i, hg)

    def win_consume(g, s_win):
        p = jnp.exp2(s_win - jnp.max(s_win, axis=0, keepdims=True)).astype(BF16)
        a_win = jnp.zeros((LANES, rows_all), F32)
        for t in range(wt):
            pt = p[t * tq:(t + 1) * tq]
            a_win = a_win + jnp.concatenate(
                [jnp.dot(vwe_s[g, st + t], pt[:, :half], preferred_element_type=F32),
                 jnp.dot(vwo_s[g, st + t], pt[:, half:], preferred_element_type=F32)], axis=1)
        return a_win

    def key_step(j, s):
        for g in range(n_g):
            nxt = scores(j, g + 1) if g + 1 < n_g else scores(j + 1, 0)
            consume(j, g, s, False)
            s = nxt
        return s

    s = lax.fori_loop(0, n_full, key_step, scores(0, 0))

    for g in range(n_g):
        if g + 1 < n_g:
            nxt = scores(n_full, g + 1) + _mask_bias(tab_ref, n_full * (tk // tq), tk // tq, i, hg)
        else:
            nxt = win_scores(0)
        consume(n_full, g, s, g == 0)
        s = nxt
    a_win = []
    for g in range(n_g):
        nxt = win_scores(g + 1) if g + 1 < n_g else None
        a_win.append(win_consume(g, s))
        s = nxt

    lane = lax.broadcasted_iota(jnp.int32, (tq, LANES), 1)
    left = lane < HEAD_DIM
    top = lax.broadcasted_iota(jnp.int32, (LANES, tq), 0) < HEAD_DIM
    for g in range(n_g):
        gates = gates_ref[0, :, g * LANES:(g + 1) * LANES]
        a_sel = acc_s[g]
        for hp in range(hg // 2):
            h0, h1 = 2 * hp, 2 * hp + 1
            cols = slice((g * (hg // 2) + hp) * LANES, (g * (hg // 2) + hp + 1) * LANES)

            def pair(a):
                ae = a[:, hp * tq:(hp + 1) * tq]
                ao = a[:, half + hp * tq:half + (hp + 1) * tq]
                return jnp.where(top, ae / jnp.maximum(ae[HEAD_DIM:HEAD_DIM + 1, :], 1e-30),
                                 ao / jnp.maximum(ao[0:1, :], 1e-30)).T

            def gate(branch):
                return jnp.where(left, gates[:, 3 * h0 + branch:3 * h0 + branch + 1],
                                 gates[:, 3 * h1 + branch:3 * h1 + branch + 1])

            o = gate(0) * ocmp_ref[0, :, cols] + gate(1) * pair(a_sel) + gate(2) * pair(a_win[g])
            o_ref[0, :, cols] = o.astype(BF16)


def _nsa(qs, ksl, vsl, kwn, vwn, ocmp, gates):
    b, seq, _ = qs.shape
    g, hg = NSA_KV_GROUPS, NSA_HEADS_PER_GROUP
    tq, tk = NSA_TQ, KV_STEP
    assert seq % tk == 0 and tk % tq == 0 and WINDOW % tq == 0 and seq >= WINDOW + tq and tq & (tq - 1) == 0
    assert hg % 2 == 0
    assert WINDOW == 2 * tq
    kv = pl.BlockSpec((1, seq, g * LANES), lambda bi, i: (bi, 0, 0))
    ospec = pl.BlockSpec((1, tq, NSA_WIDTH), lambda bi, i: (bi, i, 0))
    return pl.pallas_call(
        functools.partial(_nsa_kernel, tq=tq, tk=tk, seq=seq),
        grid=(b, seq // tq),
        in_specs=[pl.BlockSpec((1, tq, g * hg * LANES), lambda bi, i: (bi, i, 0)), kv, kv, kv, kv,
                  ospec, pl.BlockSpec((1, tq, g * LANES), lambda bi, i: (bi, i, 0)), _resident((4, tq, tq))],
        out_specs=ospec,
        out_shape=jax.ShapeDtypeStruct((b, seq, NSA_WIDTH), BF16),
        scratch_shapes=[pltpu.VMEM((g, seq // tk, LANES, tk), BF16)] * 2
        + [pltpu.VMEM((g, seq // tq, LANES, tq), BF16)] * 2
        + [pltpu.VMEM((g, 1, hg * tq), F32), pltpu.VMEM((g, LANES, hg * tq), F32)],
        compiler_params=pltpu.CompilerParams(dimension_semantics=("arbitrary",) * 2,
                                             vmem_limit_bytes=VMEM_LIMIT_BYTES),
        name="nsa",
    )(qs, ksl, vsl, kwn, vwn, ocmp, gates, _mask_table(tq))


def _rms(y, g):
    return y * lax.rsqrt(jnp.mean(y * y, axis=-1, keepdims=True) + RMS_EPS) * g


def _tail_kernel(x_ref, fox_ref, nsa_ref, ga_ref, gb_ref, wf_ref, wn_ref, wo_ref, wup_ref, wdn_ref,
                 gmix_ref, gpre_ref, gpost_ref, o_ref, *, d_ff):
    a = jnp.dot(fox_ref[...], wf_ref[...], preferred_element_type=F32)
    b = jnp.dot(nsa_ref[...], wn_ref[...], preferred_element_type=F32)
    mix = ga_ref[...].astype(F32) * a + gb_ref[...].astype(F32) * b
    y = jnp.dot(mix.astype(BF16), wo_ref[...], preferred_element_type=F32)
    x1 = x_ref[...] + _rms(y, gmix_ref[...])
    h = _rms(x1, gpre_ref[...]).astype(BF16)
    acc = jnp.zeros(x1.shape, F32)
    for c in range(d_ff // FF_CHUNK):
        cols = slice(c * FF_CHUNK, (c + 1) * FF_CHUNK)
        u = jnp.maximum(jnp.dot(h, wup_ref[:, cols], preferred_element_type=F32), 0.0)
        acc = acc + jnp.dot((u * u).astype(BF16), wdn_ref[cols, :], preferred_element_type=F32)
    o_ref[...] = x1 + _rms(acc, gpost_ref[...])


def _tail(x2, fox, nsa, sga, sgb, wf, wn, wo, wup, wdn, g_mix_post, g_mlp_pre, g_mlp_post):
    n_tok, d = x2.shape
    d_ff = wup.shape[1]
    tm = TAIL_TM
    assert n_tok % tm == 0 and d_ff % FF_CHUNK == 0

    def row(w):
        return pl.BlockSpec((tm, w), lambda i: (i, 0))

    return pl.pallas_call(
        functools.partial(_tail_kernel, d_ff=d_ff),
        grid=(n_tok // tm,),
        in_specs=[row(d), row(fox.shape[1]), row(nsa.shape[1]), row(d), row(d),
                  _resident(wf.shape), _resident(wn.shape), _resident(wo.shape),
                  _resident(wup.shape), _resident(wdn.shape),
                  _resident((1, d)), _resident((1, d)), _resident((1, d))],
        out_specs=row(d),
        out_shape=jax.ShapeDtypeStruct((n_tok, d), F32),
        compiler_params=pltpu.CompilerParams(dimension_semantics=("arbitrary",),
                                             vmem_limit_bytes=VMEM_LIMIT_BYTES),
        name="tail",
    )(x2, fox, nsa, sga, sgb, wf, wn, wo, wup, wdn, g_mix_post, g_mlp_pre, g_mlp_post)


def _layer(x, norm_mix_pre, norm_mix_post, norm_mlp_pre, norm_mlp_post, w_in, b_forget,
           cmp_pos_k, cmp_w1_k, cmp_w2_k, cmp_pos_v, cmp_w1_v, cmp_w2_v,
           w_fox_out, w_nsa_out, w_o, w_up, w_down):
    b, seq, d = x.shape
    n_tok = b * seq
    x2 = x.reshape(n_tok, d)
    w_small, w_ga, w_gb = _rearrange_w_in(w_in, d)
    bmisc = jnp.concatenate([b_forget.astype(F32), jnp.zeros((LANES - FOX_HEADS,), F32)]).reshape(1, LANES)
    (fq, fk, fv, nq, ksl, vsl, kwn, vwn, kc, vc, misc, gates, sga, sgb) = _proj(
        x2, norm_mix_pre.reshape(1, d), w_small, w_ga, w_gb, _rope_tables(seq), bmisc, seq)

    def seq3(t):
        return t.reshape(b, seq, t.shape[-1])

    fox = _fox(seq3(fq), seq3(fk), seq3(fv), _cumsum(seq3(misc)))

    kcmp, vcmp = _compress(kc, vc, b, seq, cmp_pos_k, cmp_w1_k, cmp_w2_k, cmp_pos_v, cmp_w1_v, cmp_w2_v)
    qs, ocmp = _cmpsel(seq3(nq), kcmp, vcmp)
    nsa = _nsa(qs, seq3(ksl), seq3(vsl), seq3(kwn), seq3(vwn), ocmp, seq3(gates))

    out = _tail(x2, fox.reshape(n_tok, FOX_WIDTH), nsa.reshape(n_tok, NSA_WIDTH), sga, sgb,
                w_fox_out.astype(BF16), w_nsa_out.astype(BF16), w_o.astype(BF16),
                w_up.astype(BF16), w_down.astype(BF16),
                norm_mix_post.reshape(1, d), norm_mlp_pre.reshape(1, d), norm_mlp_post.reshape(1, d))
    return out.reshape(b, seq, d)


def kernel(x, norm_mix_pre, norm_mix_post, norm_mlp_pre, norm_mlp_post, w_in, b_forget, cmp_pos_k, cmp_w1_k, cmp_w2_k, cmp_pos_v, cmp_w1_v, cmp_w2_v, w_fox_out, w_nsa_out, w_o, w_up, w_down):
    params = (norm_mix_pre, norm_mix_post, norm_mlp_pre, norm_mlp_post, w_in, b_forget,
              cmp_pos_k, cmp_w1_k, cmp_w2_k, cmp_pos_v, cmp_w1_v, cmp_w2_v,
              w_fox_out, w_nsa_out, w_o, w_up, w_down)
    for layer in range(w_in.shape[0]):
        x = _layer(x, *[p[layer] for p in params])
    return x
```

```python
import functools

import numpy as np
import jax
import jax.numpy as jnp
from jax import lax
from jax.experimental import pallas as pl
from jax.experimental.pallas import tpu as pltpu

HEAD_DIM = 64
FOX_HEADS = 8
NSA_HEADS = 8
NSA_KV_GROUPS = 2
NSA_HEADS_PER_GROUP = NSA_HEADS // NSA_KV_GROUPS
FOX_WIDTH = FOX_HEADS * HEAD_DIM
NSA_WIDTH = NSA_HEADS * HEAD_DIM
NSA_KV_WIDTH = NSA_KV_GROUPS * HEAD_DIM
ROPE_THETA = 10000.0
CMP_BLOCK = 32
CMP_STRIDE = 16
SEL_BLOCK = 64
SEL_TOPN = 8
WINDOW = 512
RMS_EPS = 1e-6
NEG_INF = -1e30
FORCED_BONUS = 1e4
ATTN_SCALE = HEAD_DIM ** -0.5
LOG2E = 1.4426950408889634
Q_SCALE = ATTN_SCALE * LOG2E

LANES = 128
VMEM_LIMIT_BYTES = 52 * 1024 * 1024
SEL_LANE0 = HEAD_DIM
MAX_SEL_BLOCKS = 32

F32 = jnp.float32
BF16 = jnp.bfloat16
HIGHEST = lax.Precision.HIGHEST

PROJ_TM = 512
FOX_TQ = 256
CMP_TQ = 256
NSA_TQ = 256
TAIL_TM = 512
FF_CHUNK = 1024
KV_STEP = 512
BIAS_LANE0 = HEAD_DIM


def _nt_dot(a, b):
    return lax.dot_general(a, b, (((1,), (1,)), ((), ())), preferred_element_type=F32)


def _sigmoid(t):
    return 1.0 / (1.0 + jnp.exp(-t))


def _resident(shape):
    nd = len(shape)
    return pl.BlockSpec(shape, lambda *_: (0,) * nd, pipeline_mode=pl.Buffered(1))


def _proj_layout():
    widths = [('fq', FOX_WIDTH), ('fk', FOX_WIDTH), ('fv', FOX_WIDTH),
              ('nq', NSA_WIDTH),
              ('ksl', NSA_KV_WIDTH), ('vsl', NSA_KV_WIDTH), ('kwn', NSA_KV_WIDTH), ('vwn', NSA_KV_WIDTH),
              ('kc', NSA_KV_WIDTH), ('vc', NSA_KV_WIDTH),
              ('misc', LANES)]
    out, c = {}, 0
    for name, w in widths:
        out[name] = (c, w)
        c += w
    return out, c


def _rearrange_w_in(w_in, d_model):
    splits = (FOX_WIDTH, FOX_WIDTH, FOX_WIDTH, FOX_HEADS,
              NSA_WIDTH, NSA_KV_WIDTH, NSA_KV_WIDTH, NSA_KV_WIDTH, NSA_KV_WIDTH, NSA_KV_WIDTH, NSA_KV_WIDTH,
              3 * NSA_HEADS, d_model, d_model)
    pts = [int(p) for p in np.cumsum(splits)[:-1]]
    fq, fk, fv, ff, nq, kc, vc, ksl, vsl, kwn, vwn, ng, ga, gb = jnp.split(w_in, pts, axis=-1)
    d = w_in.shape[0]
    misc = jnp.concatenate([ff, ng, jnp.zeros((d, LANES - FOX_HEADS - 3 * NSA_HEADS), w_in.dtype)], axis=-1)
    w_small = jnp.concatenate([fq, fk, fv, nq, ksl, vsl, kwn, vwn, kc, vc, misc], axis=-1)
    return w_small.astype(BF16), ga.astype(BF16), gb.astype(BF16)


def _rope_tables(seq):
    half = HEAD_DIM // 2
    inv = jnp.power(jnp.float32(ROPE_THETA), -jnp.arange(0, HEAD_DIM, 2, dtype=F32) / HEAD_DIM)
    ang = jnp.arange(seq, dtype=F32)[:, None] * inv[None, :]
    cos, sin = jnp.cos(ang), jnp.sin(ang)
    zero = jnp.zeros_like(sin)
    cos_t = jnp.concatenate([cos, cos, cos, cos], axis=-1)
    s_lo = jnp.concatenate([-sin, zero, -sin, zero], axis=-1)
    s_hi = jnp.concatenate([zero, sin, zero, sin], axis=-1)
    assert cos_t.shape == (seq, 4 * half)
    return cos_t, s_lo, s_hi


def _proj_kernel(x_ref, g_ref, w_ref, wga_ref, wgb_ref, cos_ref, slo_ref, shi_ref, bmisc_ref,
                 fq_ref, fk_ref, fv_ref, nq_ref, ksl_ref, vsl_ref, kwn_ref, vwn_ref,
                 kc_ref, vc_ref, misc_ref, gates_ref, ga_ref, gb_ref, *, tm, seq, layout):
    x = x_ref[...]
    h = x * lax.rsqrt(jnp.mean(x * x, axis=-1, keepdims=True) + RMS_EPS) * g_ref[...]
    h = h.astype(BF16)
    cos, slo, shi = cos_ref[...], slo_ref[...], shi_ref[...]
    half = HEAD_DIM // 2

    def mm(name):
        c0, n = layout[name]
        return jnp.dot(h, w_ref[:, c0:c0 + n], preferred_element_type=F32)

    def rope_block(t):
        return t * cos + pltpu.roll(t, LANES - half, axis=1) * slo + pltpu.roll(t, half, axis=1) * shi

    def rope(t):
        n = t.shape[1] // LANES
        return [rope_block(t[:, b * LANES:(b + 1) * LANES]) for b in range(n)]

    lane = lax.broadcasted_iota(jnp.int32, (tm, LANES), 1)
    left = lane < HEAD_DIM

    def blocks(t):
        return [t[:, b * LANES:(b + 1) * LANES] for b in range(t.shape[1] // LANES)]

    def split_heads(pairs, o_ref, scale, fill=0.0):
        for p, blk in enumerate(pairs):
            blk = blk * scale
            o_ref[:, (2 * p) * LANES:(2 * p + 1) * LANES] = jnp.where(left, blk, fill).astype(BF16)
            o_ref[:, (2 * p + 1) * LANES:(2 * p + 2) * LANES] = jnp.where(
                left, pltpu.roll(blk, HEAD_DIM, axis=1), fill).astype(BF16)

    def dup_heads(pairs, o_ref):
        for p, blk in enumerate(pairs):
            swapped = pltpu.roll(blk, HEAD_DIM, axis=1)
            o_ref[:, (2 * p) * LANES:(2 * p + 1) * LANES] = jnp.where(left, blk, swapped).astype(BF16)
            o_ref[:, (2 * p + 1) * LANES:(2 * p + 2) * LANES] = jnp.where(left, swapped, blk).astype(BF16)

    b0 = BIAS_LANE0
    split_heads(blocks(mm('fq')), fq_ref, Q_SCALE, fill=((lane >= b0 + 3) & (lane < b0 + 6)).astype(F32))
    split_heads(blocks(mm('fk')), fk_ref, 1.0, fill=((lane >= b0) & (lane < b0 + 3)).astype(F32))
    fv = mm('fv')
    for p in range(FOX_HEADS // 2):
        blk = fv[:, p * LANES:(p + 1) * LANES]
        fv_ref[:, (2 * p) * LANES:(2 * p + 1) * LANES] = jnp.where(
            left, blk, (lane == HEAD_DIM).astype(F32)).astype(BF16)
        fv_ref[:, (2 * p + 1) * LANES:(2 * p + 2) * LANES] = jnp.where(
            left, (lane == 0).astype(F32), blk).astype(BF16)

    split_heads(rope(mm('nq')), nq_ref, Q_SCALE)

    pos = (pl.program_id(0) % (seq // tm)) * tm + lax.broadcasted_iota(jnp.int32, (tm, LANES), 0)
    onehot = ((lane - SEL_LANE0) == (pos // SEL_BLOCK)).astype(F32)
    split_heads(rope(mm('ksl')), ksl_ref, 1.0, fill=onehot)
    dup_heads(blocks(mm('vsl')), vsl_ref)
    split_heads(rope(mm('kwn')), kwn_ref, 1.0)
    dup_heads(blocks(mm('vwn')), vwn_ref)

    kc_ref[...] = rope(mm('kc'))[0]
    vc_ref[...] = mm('vc')

    t = mm('misc') + bmisc_ref[...]
    misc_ref[...] = -(jnp.maximum(-t, 0.0) + jnp.log(1.0 + jnp.exp(-jnp.abs(t))))
    per_group = 3 * NSA_HEADS_PER_GROUP
    for g in range(NSA_KV_GROUPS):
        gates_ref[:, g * LANES:(g + 1) * LANES] = _sigmoid(
            pltpu.roll(t, LANES - (FOX_HEADS + g * per_group), axis=1))

    ga_ref[...] = _sigmoid(jnp.dot(h, wga_ref[...], preferred_element_type=F32)).astype(BF16)
    gb_ref[...] = _sigmoid(jnp.dot(h, wgb_ref[...], preferred_element_type=F32)).astype(BF16)


def _proj(x2, g, w_small, w_ga, w_gb, tables, bmisc, seq):
    n_tok, d = x2.shape
    tm = PROJ_TM
    layout, n_cols = _proj_layout()
    assert n_tok % tm == 0 and seq % tm == 0 and w_small.shape == (d, n_cols)
    tiles_per_seq = seq // tm

    def row(w):
        return pl.BlockSpec((tm, w), lambda i: (i, 0))

    def tab():
        return pl.BlockSpec((tm, LANES), lambda i: (i % tiles_per_seq, 0))

    outs = [('fq', BF16), ('fk', BF16), ('fv', BF16), ('nq', BF16), ('ksl', BF16), ('vsl', BF16),
            ('kwn', BF16), ('vwn', BF16), ('kc', F32), ('vc', F32), ('misc', F32), ('gates', F32)]
    widths = dict(kc=NSA_KV_WIDTH, vc=NSA_KV_WIDTH, misc=LANES, gates=NSA_KV_GROUPS * LANES)
    widths.update({n: FOX_HEADS * LANES for n in ('fq', 'fk', 'fv')})
    widths.update(nq=NSA_HEADS * LANES, **{n: NSA_KV_GROUPS * LANES for n in ('ksl', 'vsl', 'kwn', 'vwn')})
    out_shape = [jax.ShapeDtypeStruct((n_tok, widths[n]), dt) for n, dt in outs]
    out_specs = [row(widths[n]) for n, _ in outs]
    out_shape += [jax.ShapeDtypeStruct((n_tok, d), BF16)] * 2
    out_specs += [row(d), row(d)]
    return pl.pallas_call(
        functools.partial(_proj_kernel, tm=tm, seq=seq, layout=layout),
        grid=(n_tok // tm,),
        in_specs=[row(d), _resident((1, d)), _resident(w_small.shape), _resident(w_ga.shape),
                  _resident(w_gb.shape), tab(), tab(), tab(), _resident((1, LANES))],
        out_specs=out_specs,
        out_shape=out_shape,
        compiler_params=pltpu.CompilerParams(dimension_semantics=("arbitrary",),
                                             vmem_limit_bytes=VMEM_LIMIT_BYTES),
        name="proj",
    )(x2, g, w_small, w_ga, w_gb, *tables, bmisc)


def _cumsum_kernel(m_ref, ccol_ref, *, seq):
    r = lax.broadcasted_iota(jnp.int32, (LANES, LANES), 0)
    c = lax.broadcasted_iota(jnp.int32, (LANES, LANES), 1)
    tri = (r >= c).astype(F32)
    carry = jnp.zeros((1, LANES), F32)
    for b in range(seq // LANES):
        blk = m_ref[0, b * LANES:(b + 1) * LANES, :]
        cs = jnp.dot(tri, blk, precision=HIGHEST, preferred_element_type=F32) + carry
        ccol_ref[0, b * LANES:(b + 1) * LANES, :] = cs
        carry = cs[LANES - 1:LANES, :]


def _cumsum(misc3):
    b, seq, _ = misc3.shape
    assert seq % LANES == 0
    return pl.pallas_call(
        functools.partial(_cumsum_kernel, seq=seq),
        grid=(b,),
        in_specs=[pl.BlockSpec((1, seq, LANES), lambda i: (i, 0, 0))],
        out_specs=pl.BlockSpec((1, seq, LANES), lambda i: (i, 0, 0)),
        out_shape=jax.ShapeDtypeStruct((b, seq, LANES), F32),
        compiler_params=pltpu.CompilerParams(dimension_semantics=("arbitrary",),
                                             vmem_limit_bytes=VMEM_LIMIT_BYTES),
        name="cumsum",
    )(misc3)


def _mask_table(tq):
    d = np.arange(tq)[:, None] - np.arange(tq)[None, :]
    neg = np.full((tq, tq), NEG_INF, np.float32)
    zero = np.zeros((tq, tq), np.float32)
    return jnp.asarray(np.stack([np.where(d > 0, zero, neg), zero, np.where(d <= 0, zero, neg), neg]))


def _mask_bias(tab_ref, key_tile0, n_tiles, q_tile, reps):
    rows = []
    for u in range(n_tiles):
        t = tab_ref[jnp.clip(key_tile0 + u - q_tile + 2, 0, 3)]
        rows.append(jnp.concatenate([t] * reps, axis=1) if reps > 1 else t)
    return jnp.concatenate(rows, axis=0) if n_tiles > 1 else rows[0]


def _split3(c):
    hi = c.astype(BF16).astype(F32)
    r = c - hi
    mid = r.astype(BF16).astype(F32)
    lo = (r - mid).astype(BF16).astype(F32)
    return hi, mid, lo


def _softmax_step_t(s, pv, m_ref, acc_ref, idx):
    m_prev = m_ref[idx]
    m_new = jnp.maximum(m_prev, jnp.max(s, axis=0, keepdims=True))
    p = jnp.exp2(s - m_new).astype(BF16)
    acc_ref[idx] = jnp.exp2(m_prev - m_new) * acc_ref[idx] + pv(p)
    m_ref[idx] = m_new


def _bias_selectors():
    sq = np.zeros((LANES, FOX_HEADS * LANES), np.float32)
    sk = np.zeros((LANES, FOX_HEADS * LANES), np.float32)
    for piece in range(3):
        for h in range(FOX_HEADS):
            sq[piece * FOX_HEADS + h, h * LANES + BIAS_LANE0 + piece] = 1.0
            sk[piece * FOX_HEADS + h, h * LANES + BIAS_LANE0 + 3 + piece] = -1.0
    return jnp.asarray(sq, BF16), jnp.asarray(sk, BF16)


def _fox_kernel(q_ref, k_ref, v_ref, ccol_ref, selq_ref, selk_ref, o_ref, kx_s, vt_s, qx_s, m_s, acc_s,
                *, tq, tk, seq):
    i = pl.program_id(1)
    lane = lax.broadcasted_iota(jnp.int32, (tq, LANES), 1)

    def bias_lanes(c, sel_ref):
        his, mids, los = (jnp.where(lane < FOX_HEADS, t, 0.0) for t in _split3(c * LOG2E))
        packed = his + pltpu.roll(mids, FOX_HEADS, axis=1) + pltpu.roll(los, 2 * FOX_HEADS, axis=1)
        return jnp.dot(packed.astype(BF16), sel_ref[...], preferred_element_type=F32)

    @pl.when(i == 0)
    def _build_keys():
        def chunk(r, carry):
            rows = pl.ds(pl.multiple_of(r * tq, tq), tq)
            e = bias_lanes(ccol_ref[0, rows, :], selk_ref)
            for h in range(FOX_HEADS):
                cols = slice(h * LANES, (h + 1) * LANES)
                kx_s[h, rows, :] = (k_ref[0, rows, cols].astype(F32) + e[:, cols]).astype(BF16)
            return carry

        lax.fori_loop(0, seq // tq, chunk, 0)

        def vchunk(c, carry):
            rows = pl.ds(pl.multiple_of(c * tk, tk), tk)
            for h in range(FOX_HEADS):
                cols = slice(h * LANES, (h + 1) * LANES)
                vt_s[c, h] = v_ref[0, rows, cols].astype(F32).T.astype(BF16)
            return carry

        lax.fori_loop(0, seq // tk, vchunk, 0)

    qrows = pl.ds(pl.multiple_of(i * tq, tq), tq)
    n_full = (i * tq) // tk
    row = lax.broadcasted_iota(jnp.int32, (LANES, tq), 0)
    n_pairs = FOX_HEADS // 2
    e = bias_lanes(ccol_ref[0, qrows, :], selq_ref)
    for h in range(FOX_HEADS):
        cols = slice(h * LANES, (h + 1) * LANES)
        qx_s[h] = (q_ref[0, :, cols].astype(F32) + e[:, cols]).astype(BF16)
    m_s[...] = jnp.full(m_s.shape, NEG_INF, F32)
    acc_s[...] = jnp.zeros(acc_s.shape, F32)

    def scores(j, p, nk=tk):
        rows = pl.ds(pl.multiple_of(j * tk, tk), nk)
        return tuple(_nt_dot(kx_s[2 * p + slot, rows, :], qx_s[2 * p + slot]) for slot in range(2))

    def causal(j, ss, nk):
        visible = (j * tk + lax.broadcasted_iota(jnp.int32, (nk, tq), 0)
                   <= i * tq + lax.broadcasted_iota(jnp.int32, (nk, tq), 1))
        return tuple(jnp.where(visible, s, NEG_INF) for s in ss)

    def consume(j, p, ss, nk=tk):
        for slot in range(2):
            h = 2 * p + slot
            _softmax_step_t(ss[slot], lambda pr, h=h: jnp.dot(vt_s[j, h, :, 0:nk], pr, preferred_element_type=F32),
                            m_s, acc_s, h)

    def key_step(j, ss, masked, last, nk=tk):
        for p in range(n_pairs):
            if p + 1 < n_pairs:
                nxt = scores(j, p + 1, nk)
                if masked:
                    nxt = causal(j, nxt, nk)
            else:
                nxt = None if last else scores(j + 1, 0)
            consume(j, p, ss, nk)
            ss = nxt
        return ss

    ss = lax.fori_loop(0, n_full, lambda j, ss: key_step(j, ss, False, False), scores(0, 0))

    def tail_short(ss):
        key_step(n_full, causal(n_full, scores(n_full, 0, tq), tq), True, True, tq)
        return 0

    def tail_full(ss):
        key_step(n_full, causal(n_full, ss, tk), True, True)
        return 0

    lax.cond((i * tq) % tk == 0, tail_short, tail_full, ss)

    for hp in range(n_pairs):
        a0, a1 = acc_s[2 * hp], acc_s[2 * hp + 1]
        ot = jnp.where(row < HEAD_DIM, a0 / a0[HEAD_DIM:HEAD_DIM + 1, :], a1 / a1[0:1, :])
        o_ref[0, :, hp * LANES:(hp + 1) * LANES] = ot.T.astype(BF16)


def _fox(fq, fk, fv, ccol):
    b, seq, w = fq.shape
    tq, tk = FOX_TQ, KV_STEP
    assert seq % tq == 0 and seq % tk == 0 and tk % tq == 0 and w == FOX_HEADS * LANES
    assert 3 * FOX_HEADS <= LANES and tk == 2 * tq
    sel_q, sel_k = _bias_selectors()
    whole = pl.BlockSpec((1, seq, w), lambda bi, i: (bi, 0, 0))
    return pl.pallas_call(
        functools.partial(_fox_kernel, tq=tq, tk=tk, seq=seq),
        grid=(b, seq // tq),
        in_specs=[pl.BlockSpec((1, tq, w), lambda bi, i: (bi, i, 0)), whole, whole,
                  pl.BlockSpec((1, seq, LANES), lambda bi, i: (bi, 0, 0)),
                  _resident(sel_q.shape), _resident(sel_k.shape)],
        out_specs=pl.BlockSpec((1, tq, FOX_WIDTH), lambda bi, i: (bi, i, 0)),
        out_shape=jax.ShapeDtypeStruct((b, seq, FOX_WIDTH), BF16),
        scratch_shapes=[pltpu.VMEM((FOX_HEADS, seq, LANES), BF16),
                        pltpu.VMEM((seq // tk, FOX_HEADS, LANES, tk), BF16),
                        pltpu.VMEM((FOX_HEADS, tq, LANES), BF16),
                        pltpu.VMEM((FOX_HEADS, 1, tq), F32), pltpu.VMEM((FOX_HEADS, LANES, tq), F32)],
        compiler_params=pltpu.CompilerParams(dimension_semantics=("arbitrary", "arbitrary"),
                                             vmem_limit_bytes=VMEM_LIMIT_BYTES),
        name="fox",
    )(fq, fk, fv, ccol, sel_q, sel_k)


def _compress_kernel(yk_ref, yv_ref, posk_ref, w1k_ref, w2k_ref, posv_ref, w1v_ref, w2v_ref,
                     kc_ref, vc_ref, *, nb):
    half = CMP_STRIDE * HEAD_DIM

    lane = lax.broadcasted_iota(jnp.int32, (nb, LANES), 1)
    left = lane < HEAD_DIM

    def regroup(x_ref):
        y = [[], []]
        for j in range(CMP_STRIDE // 2):
            xe = x_ref[0, pl.ds(2 * j, nb, stride=CMP_STRIDE), :]
            xo = x_ref[0, pl.ds(2 * j + 1, nb, stride=CMP_STRIDE), :]
            y[0].append(jnp.where(left, xe, pltpu.roll(xo, HEAD_DIM, axis=1)))
            y[1].append(jnp.where(left, pltpu.roll(xe, HEAD_DIM, axis=1), xo))
        return [jnp.concatenate(t, axis=1) for t in y]

    def comp(y, pos_ref, w1_ref, w2_ref):
        pos = pos_ref[...]
        def dot3(x, r0):
            xh = x.astype(BF16)
            xl = (x - xh.astype(F32)).astype(BF16)
            wh, wl = w1_ref[0, r0:r0 + half, :], w1_ref[1, r0:r0 + half, :]
            return (jnp.dot(xh, wh, preferred_element_type=F32) + jnp.dot(xh, wl, preferred_element_type=F32)
                    + jnp.dot(xl, wh, preferred_element_type=F32))

        a = dot3(y + pos[:, :half], 0)
        bm = dot3(y + pos[:, half:], half)
        pre = a + pltpu.roll(bm, nb - 1, axis=0)
        return jnp.dot(pre * _sigmoid(pre), w2_ref[...], precision=HIGHEST, preferred_element_type=F32)

    for g, (yk, yv) in enumerate(zip(regroup(yk_ref), regroup(yv_ref))):
        kc_ref[0, g] = comp(yk, posk_ref, w1k_ref, w2k_ref).astype(BF16)
        vc_ref[0, g] = comp(yv, posv_ref, w1v_ref, w2v_ref).astype(BF16)


def _compress(kc, vc, b, seq, pos_k, w1_k, w2_k, pos_v, w1_v, w2_v):
    g = NSA_KV_GROUPS
    nb = seq // CMP_STRIDE
    half = CMP_STRIDE * HEAD_DIM
    assert CMP_BLOCK == 2 * CMP_STRIDE and seq % CMP_STRIDE == 0 and nb % 8 == 0
    assert g == 2 and kc.shape[-1] == LANES

    zeros1 = jnp.zeros((CMP_BLOCK * HEAD_DIM, HEAD_DIM), F32)
    zeros2 = jnp.zeros((HEAD_DIM, HEAD_DIM), F32)
    def hi_lo(w):
        w = jnp.concatenate([w, zeros1], axis=-1)
        hi = w.astype(BF16)
        return jnp.stack([hi, (w - hi.astype(F32)).astype(BF16)])

    w1k, w1v = hi_lo(w1_k), hi_lo(w1_v)
    w2k = jnp.concatenate([jnp.concatenate([w2_k, zeros2], axis=-1), jnp.zeros((HEAD_DIM, LANES), F32)], axis=0)
    w2v = jnp.concatenate([jnp.concatenate([w2_v, w2_v], axis=-1), jnp.zeros((HEAD_DIM, LANES), F32)], axis=0)
    yspec = pl.BlockSpec((1, seq, LANES), lambda bi: (bi, 0, 0))
    ospec = pl.BlockSpec((1, g, nb, LANES), lambda bi: (bi, 0, 0, 0))
    return pl.pallas_call(
        functools.partial(_compress_kernel, nb=nb),
        grid=(b,),
        in_specs=[yspec, yspec,
                  _resident((1, 2 * half)), _resident(w1k.shape), _resident(w2k.shape),
                  _resident((1, 2 * half)), _resident(w1v.shape), _resident(w2v.shape)],
        out_specs=[ospec, ospec],
        out_shape=[jax.ShapeDtypeStruct((b, g, nb, LANES), BF16)] * 2,
        compiler_params=pltpu.CompilerParams(dimension_semantics=("arbitrary",),
                                             vmem_limit_bytes=VMEM_LIMIT_BYTES),
        name="compress",
    )(kc.reshape(b, seq, LANES), vc.reshape(b, seq, LANES), pos_k.reshape(1, 2 * half), w1k, w2k,
      pos_v.reshape(1, 2 * half), w1v, w2v)


def _overlap_matrix(nb, n_cmp, n_sel):
    ci = np.arange(nb)[:, None] * CMP_STRIDE
    sj = (np.arange(LANES)[None, :] - SEL_LANE0) * SEL_BLOCK
    jj = np.arange(LANES)[None, :] - SEL_LANE0
    ov = (ci < sj + SEL_BLOCK) & (ci + CMP_BLOCK > sj) & (jj >= 0) & (jj < n_sel) & (np.arange(nb)[:, None] < n_cmp)
    return jnp.asarray(ov.astype(np.float32))


def _cmpsel_kernel(q_ref, kc_ref, vc_ref, ovt_ref, qs_ref, ocmp_ref, vct_s, *, tq, n_cmp, nb):
    i = pl.program_id(1)
    n_g, hg = NSA_KV_GROUPS, NSA_HEADS_PER_GROUP

    @pl.when(i == 0)
    def _transpose_values():
        for g in range(n_g):
            vct_s[g] = vc_ref[0, g].astype(F32).T.astype(BF16)

    n_idx = lax.broadcasted_iota(jnp.int32, (nb, tq), 0)
    t_pos = i * tq + lax.broadcasted_iota(jnp.int32, (nb, tq), 1)
    cmask = (n_idx * CMP_STRIDE + (CMP_BLOCK - 1) <= t_pos) & (n_idx < n_cmp)
    top = lax.broadcasted_iota(jnp.int32, (LANES, tq), 0) < HEAD_DIM
    psums = []
    for g in range(n_g):
        kc = kc_ref[0, g]
        vct = vct_s[g]
        psum = jnp.zeros((nb, tq), F32)
        outs = []
        for h in range(hg):
            cols = slice((g * hg + h) * LANES, (g * hg + h + 1) * LANES)
            s = jnp.where(cmask, _nt_dot(kc, q_ref[0, :, cols]), NEG_INF)
            m = jnp.max(s, axis=0, keepdims=True)
            p = jnp.where(cmask, jnp.exp2(s - m), 0.0)
            p = p / jnp.maximum(jnp.sum(p, axis=0, keepdims=True), 1e-30)
            psum = psum + p
            outs.append(jnp.dot(vct, p.astype(BF16), preferred_element_type=F32))
        psums.append(psum)
        for hp in range(hg // 2):
            ocols = slice((g * (hg // 2) + hp) * LANES, (g * (hg // 2) + hp + 1) * LANES)
            ocmp_ref[0, :, ocols] = jnp.where(top, outs[2 * hp], outs[2 * hp + 1]).T

    wide = n_g * tq
    ovt = ovt_ref[...]
    p_slc = sum(jnp.dot(ovt, piece.astype(BF16), preferred_element_type=F32)
                for piece in _split3(jnp.concatenate(psums, axis=1)))
    sc = p_slc[SEL_LANE0:SEL_LANE0 + MAX_SEL_BLOCKS, :]
    j = lax.broadcasted_iota(jnp.int32, (MAX_SEL_BLOCKS, wide), 0)
    cur = (i * tq + (lax.broadcasted_iota(jnp.int32, (MAX_SEL_BLOCKS, wide), 1) & (tq - 1))) // SEL_BLOCK
    forced = (j == 0) | (j == cur) | (j == cur - 1)
    score = jnp.where(j <= cur, sc + FORCED_BONUS * forced.astype(F32), NEG_INF)
    j_f = j.astype(F32)
    sel = jnp.zeros((MAX_SEL_BLOCKS, wide), jnp.bool_)
    for _ in range(SEL_TOPN):
        mx = jnp.max(score, axis=0, keepdims=True)
        first = jnp.min(jnp.where(score == mx, j_f, float(MAX_SEL_BLOCKS)), axis=0, keepdims=True)
        chosen = j_f == first
        sel = sel | chosen
        score = jnp.where(chosen, -jnp.inf, score)
    bias_t = jnp.concatenate([jnp.zeros((SEL_LANE0, wide), F32), jnp.where(sel, 0.0, NEG_INF),
                              jnp.zeros((LANES - SEL_LANE0 - MAX_SEL_BLOCKS, wide), F32)], axis=0)
    for g in range(n_g):
        bias = bias_t[:, g * tq:(g + 1) * tq].T
        for h in range(hg):
            cols = slice((g * hg + h) * LANES, (g * hg + h + 1) * LANES)
            qs_ref[0, :, cols] = (q_ref[0, :, cols].astype(F32) + bias).astype(BF16)


def _cmpsel(nq3, kcmp, vcmp):
    b, seq, _ = nq3.shape
    g, hg = NSA_KV_GROUPS, NSA_HEADS_PER_GROUP
    tq = CMP_TQ
    nb = kcmp.shape[2]
    n_cmp = (seq - CMP_BLOCK) // CMP_STRIDE + 1
    n_sel = seq // SEL_BLOCK
    assert seq % tq == 0 and n_sel <= MAX_SEL_BLOCKS and n_sel >= SEL_TOPN
    ov = _overlap_matrix(nb, n_cmp, n_sel).T.astype(BF16)
    assert tq & (tq - 1) == 0
    qspec = pl.BlockSpec((1, tq, g * hg * LANES), lambda bi, i: (bi, i, 0))
    cspec = pl.BlockSpec((1, g, nb, LANES), lambda bi, i: (bi, 0, 0, 0))
    return pl.pallas_call(
        functools.partial(_cmpsel_kernel, tq=tq, n_cmp=n_cmp, nb=nb),
        grid=(b, seq // tq),
        in_specs=[qspec, cspec, cspec, _resident(ov.shape)],
        out_specs=[qspec, pl.BlockSpec((1, tq, NSA_WIDTH), lambda bi, i: (bi, i, 0))],
        out_shape=[jax.ShapeDtypeStruct(nq3.shape, BF16),
                   jax.ShapeDtypeStruct((b, seq, NSA_WIDTH), F32)],
        scratch_shapes=[pltpu.VMEM((g, LANES, nb), BF16)],
        compiler_params=pltpu.CompilerParams(dimension_semantics=("arbitrary",) * 2,
                                             vmem_limit_bytes=VMEM_LIMIT_BYTES),
        name="cmpsel",
    )(nq3, kcmp, vcmp, ov)


def _nsa_kernel(qs_ref, ks_ref, vs_ref, kw_ref, vw_ref, ocmp_ref, gates_ref, tab_ref, o_ref,
                vse_s, vso_s, vwe_s, vwo_s, m_s, acc_s, *, tq, tk, seq):
    i = pl.program_id(1)
    n_g, hg = NSA_KV_GROUPS, NSA_HEADS_PER_GROUP
    rows_all = hg * tq
    half = rows_all // 2
    wt = WINDOW // tq + 1

    @pl.when(i == 0)
    def _build_values():
        lane = lax.broadcasted_iota(jnp.int32, (tk, LANES), 1)
        one_e = (lane == HEAD_DIM).astype(F32)
        one_o = (lane == 0).astype(F32)

        def chunk(c, carry):
            rows = pl.ds(pl.multiple_of(c * tk, tk), tk)
            for g in range(n_g):
                cols = slice(g * LANES, (g + 1) * LANES)
                v = vs_ref[0, rows, cols].astype(F32)
                vse_s[g, c] = jnp.where(lane < HEAD_DIM, v, one_e).T.astype(BF16)
                vso_s[g, c] = jnp.where(lane < HEAD_DIM, one_o, v).T.astype(BF16)
                w = vw_ref[0, rows, cols].astype(F32)
                we = jnp.where(lane < HEAD_DIM, w, one_e).T.astype(BF16)
                wo = jnp.where(lane < HEAD_DIM, one_o, w).T.astype(BF16)
                for t in range(tk // tq):
                    vwe_s[g, c * (tk // tq) + t] = we[:, t * tq:(t + 1) * tq]
                    vwo_s[g, c * (tk // tq) + t] = wo[:, t * tq:(t + 1) * tq]
            return carry

        lax.fori_loop(0, seq // tk, chunk, 0)

    order = tuple(range(0, hg, 2)) + tuple(range(1, hg, 2))
    q = [jnp.concatenate([qs_ref[0, :, (g * hg + h) * LANES:(g * hg + h + 1) * LANES] for h in order], axis=0)
         for g in range(n_g)]

    n_full = (i * tq) // tk
    m_s[...] = jnp.full(m_s.shape, NEG_INF, F32)
    acc_s[...] = jnp.zeros(acc_s.shape, F32)

    def scores(j, g):
        rows = pl.ds(pl.multiple_of(j * tk, tk), tk)
        return _nt_dot(ks_ref[0, rows, g * LANES:(g + 1) * LANES], q[g])

    def consume(j, g, s, masked):
        if masked:
            s = s + _mask_bias(tab_ref, j * (tk // tq), tk // tq, i, hg)
        _softmax_step_t(s, lambda p: jnp.concatenate(
            [jnp.dot(vse_s[g, j], p[:, :half], preferred_element_type=F32),
             jnp.dot(vso_s[g, j], p[:, half:], preferred_element_type=F32)], axis=1), m_s, acc_s, g)

    st = jnp.maximum(i - (wt - 1), 0)

    def win_scores(g):
        wrows = pl.ds(pl.multiple_of(st * tq, tq), wt * tq)
        return _nt_dot(kw_ref[0, wrows, g * LANES:(g + 1) * LANES], q[g]) + _mask_bias(tab_ref, st, wt, i, hg)

    def win_consume(g, s_win):
        p = jnp.exp2(s_win - jnp.max(s_win, axis=0, keepdims=True)).astype(BF16)
        a_win = jnp.zeros((LANES, rows_all), F32)
        for t in range(wt):
            pt = p[t * tq:(t + 1) * tq]
            a_win = a_win + jnp.concatenate(
                [jnp.dot(vwe_s[g, st + t], pt[:, :half], preferred_element_type=F32),
                 jnp.dot(vwo_s[g, st + t], pt[:, half:], preferred_element_type=F32)], axis=1)
        return a_win

    def key_step(j, s):
        for g in range(n_g):
            nxt = scores(j, g + 1) if g + 1 < n_g else scores(j + 1, 0)
            consume(j, g, s, False)
            s = nxt
        return s

    s = lax.fori_loop(0, n_full, key_step, scores(0, 0))

    for g in range(n_g):
        if g + 1 < n_g:
            nxt = scores(n_full, g + 1) + _mask_bias(tab_ref, n_full * (tk // tq), tk // tq, i, hg)
        else:
            nxt = win_scores(0)
        consume(n_full, g, s, g == 0)
        s = nxt
    a_win = []
    for g in range(n_g):
        nxt = win_scores(g + 1) if g + 1 < n_g else None
        a_win.append(win_consume(g, s))
        s = nxt

    lane = lax.broadcasted_iota(jnp.int32, (tq, LANES), 1)
    left = lane < HEAD_DIM
    top = lax.broadcasted_iota(jnp.int32, (LANES, tq), 0) < HEAD_DIM
    for g in range(n_g):
        gates = gates_ref[0, :, g * LANES:(g + 1) * LANES]
        a_sel = acc_s[g]
        for hp in range(hg // 2):
            h0, h1 = 2 * hp, 2 * hp + 1
            cols = slice((g * (hg // 2) + hp) * LANES, (g * (hg // 2) + hp + 1) * LANES)

            def pair(a):
                ae = a[:, hp * tq:(hp + 1) * tq]
                ao = a[:, half + hp * tq:half + (hp + 1) * tq]
                return jnp.where(top, ae / jnp.maximum(ae[HEAD_DIM:HEAD_DIM + 1, :], 1e-30),
                                 ao / jnp.maximum(ao[0:1, :], 1e-30)).T

            def gate(branch):
                return jnp.where(left, gates[:, 3 * h0 + branch:3 * h0 + branch + 1],
                                 gates[:, 3 * h1 + branch:3 * h1 + branch + 1])

            o = gate(0) * ocmp_ref[0, :, cols] + gate(1) * pair(a_sel) + gate(2) * pair(a_win[g])
            o_ref[0, :, cols] = o.astype(BF16)


def _nsa(qs, ksl, vsl, kwn, vwn, ocmp, gates):
    b, seq, _ = qs.shape
    g, hg = NSA_KV_GROUPS, NSA_HEADS_PER_GROUP
    tq, tk = NSA_TQ, KV_STEP
    assert seq % tk == 0 and tk % tq == 0 and WINDOW % tq == 0 and seq >= WINDOW + tq and tq & (tq - 1) == 0
    assert hg % 2 == 0
    assert WINDOW == 2 * tq
    kv = pl.BlockSpec((1, seq, g * LANES), lambda bi, i: (bi, 0, 0))
    ospec = pl.BlockSpec((1, tq, NSA_WIDTH), lambda bi, i: (bi, i, 0))
    return pl.pallas_call(
        functools.partial(_nsa_kernel, tq=tq, tk=tk, seq=seq),
        grid=(b, seq // tq),
        in_specs=[pl.BlockSpec((1, tq, g * hg * LANES), lambda bi, i: (bi, i, 0)), kv, kv, kv, kv,
                  ospec, pl.BlockSpec((1, tq, g * LANES), lambda bi, i: (bi, i, 0)), _resident((4, tq, tq))],
        out_specs=ospec,
        out_shape=jax.ShapeDtypeStruct((b, seq, NSA_WIDTH), BF16),
        scratch_shapes=[pltpu.VMEM((g, seq // tk, LANES, tk), BF16)] * 2
        + [pltpu.VMEM((g, seq // tq, LANES, tq), BF16)] * 2
        + [pltpu.VMEM((g, 1, hg * tq), F32), pltpu.VMEM((g, LANES, hg * tq), F32)],
        compiler_params=pltpu.CompilerParams(dimension_semantics=("arbitrary",) * 2,
                                             vmem_limit_bytes=VMEM_LIMIT_BYTES),
        name="nsa",
    )(qs, ksl, vsl, kwn, vwn, ocmp, gates, _mask_table(tq))


def _rms(y, g):
    return y * lax.rsqrt(jnp.mean(y * y, axis=-1, keepdims=True) + RMS_EPS) * g


def _tail_kernel(x_ref, fox_ref, nsa_ref, ga_ref, gb_ref, wf_ref, wn_ref, wo_ref, wup_ref, wdn_ref,
                 gmix_ref, gpre_ref, gpost_ref, o_ref, *, d_ff):
    a = jnp.dot(fox_ref[...], wf_ref[...], preferred_element_type=F32)
    b = jnp.dot(nsa_ref[...], wn_ref[...], preferred_element_type=F32)
    mix = ga_ref[...].astype(F32) * a + gb_ref[...].astype(F32) * b
    y = jnp.dot(mix.astype(BF16), wo_ref[...], preferred_element_type=F32)
    x1 = x_ref[...] + _rms(y, gmix_ref[...])
    h = _rms(x1, gpre_ref[...]).astype(BF16)
    acc = jnp.zeros(x1.shape, F32)
    for c in range(d_ff // FF_CHUNK):
        cols = slice(c * FF_CHUNK, (c + 1) * FF_CHUNK)
        u = jnp.maximum(jnp.dot(h, wup_ref[:, cols], preferred_element_type=F32), 0.0)
        acc = acc + jnp.dot((u * u).astype(BF16), wdn_ref[cols, :], preferred_element_type=F32)
    o_ref[...] = x1 + _rms(acc, gpost_ref[...])


def _tail(x2, fox, nsa, sga, sgb, wf, wn, wo, wup, wdn, g_mix_post, g_mlp_pre, g_mlp_post):
    n_tok, d = x2.shape
    d_ff = wup.shape[1]
    tm = TAIL_TM
    assert n_tok % tm == 0 and d_ff % FF_CHUNK == 0

    def row(w):
        return pl.BlockSpec((tm, w), lambda i: (i, 0))

    return pl.pallas_call(
        functools.partial(_tail_kernel, d_ff=d_ff),
        grid=(n_tok // tm,),
        in_specs=[row(d), row(fox.shape[1]), row(nsa.shape[1]), row(d), row(d),
                  _resident(wf.shape), _resident(wn.shape), _resident(wo.shape),
                  _resident(wup.shape), _resident(wdn.shape),
                  _resident((1, d)), _resident((1, d)), _resident((1, d))],
        out_specs=row(d),
        out_shape=jax.ShapeDtypeStruct((n_tok, d), F32),
        compiler_params=pltpu.CompilerParams(dimension_semantics=("arbitrary",),
                                             vmem_limit_bytes=VMEM_LIMIT_BYTES),
        name="tail",
    )(x2, fox, nsa, sga, sgb, wf, wn, wo, wup, wdn, g_mix_post, g_mlp_pre, g_mlp_post)


def _layer(x, norm_mix_pre, norm_mix_post, norm_mlp_pre, norm_mlp_post, w_in, b_forget,
           cmp_pos_k, cmp_w1_k, cmp_w2_k, cmp_pos_v, cmp_w1_v, cmp_w2_v,
           w_fox_out, w_nsa_out, w_o, w_up, w_down):
    b, seq, d = x.shape
    n_tok = b * seq
    x2 = x.reshape(n_tok, d)
    w_small, w_ga, w_gb = _rearrange_w_in(w_in, d)
    bmisc = jnp.concatenate([b_forget.astype(F32), jnp.zeros((LANES - FOX_HEADS,), F32)]).reshape(1, LANES)
    (fq, fk, fv, nq, ksl, vsl, kwn, vwn, kc, vc, misc, gates, sga, sgb) = _proj(
        x2, norm_mix_pre.reshape(1, d), w_small, w_ga, w_gb, _rope_tables(seq), bmisc, seq)

    def seq3(t):
        return t.reshape(b, seq, t.shape[-1])

    fox = _fox(seq3(fq), seq3(fk), seq3(fv), _cumsum(seq3(misc)))

    kcmp, vcmp = _compress(kc, vc, b, seq, cmp_pos_k, cmp_w1_k, cmp_w2_k, cmp_pos_v, cmp_w1_v, cmp_w2_v)
    qs, ocmp = _cmpsel(seq3(nq), kcmp, vcmp)
    nsa = _nsa(qs, seq3(ksl), seq3(vsl), seq3(kwn), seq3(vwn), ocmp, seq3(gates))

    out = _tail(x2, fox.reshape(n_tok, FOX_WIDTH), nsa.reshape(n_tok, NSA_WIDTH), sga, sgb,
                w_fox_out.astype(BF16), w_nsa_out.astype(BF16), w_o.astype(BF16),
                w_up.astype(BF16), w_down.astype(BF16),
                norm_mix_post.reshape(1, d), norm_mlp_pre.reshape(1, d), norm_mlp_post.reshape(1, d))
    return out.reshape(b, seq, d)


def kernel(x, norm_mix_pre, norm_mix_post, norm_mlp_pre, norm_mlp_post, w_in, b_forget, cmp_pos_k, cmp_w1_k, cmp_w2_k, cmp_pos_v, cmp_w1_v, cmp_w2_v, w_fox_out, w_nsa_out, w_o, w_up, w_down):
    params = (norm_mix_pre, norm_mix_post, norm_mlp_pre, norm_mlp_post, w_in, b_forget,
              cmp_pos_k, cmp_w1_k, cmp_w2_k, cmp_pos_v, cmp_w1_v, cmp_w2_v,
              w_fox_out, w_nsa_out, w_o, w_up, w_down)
    for layer in range(w_in.shape[0]):
        x = _layer(x, *[p[layer] for p in params])
    return x
```

```python
import functools

import numpy as np
import jax
import jax.numpy as jnp
from jax import lax
from jax.experimental import pallas as pl
from jax.experimental.pallas import tpu as pltpu

HEAD_DIM = 64
FOX_HEADS = 8
NSA_HEADS = 8
NSA_KV_GROUPS = 2
NSA_HEADS_PER_GROUP = NSA_HEADS // NSA_KV_GROUPS
FOX_WIDTH = FOX_HEADS * HEAD_DIM
NSA_WIDTH = NSA_HEADS * HEAD_DIM
NSA_KV_WIDTH = NSA_KV_GROUPS * HEAD_DIM
ROPE_THETA = 10000.0
CMP_BLOCK = 32
CMP_STRIDE = 16
SEL_BLOCK = 64
SEL_TOPN = 8
WINDOW = 512
RMS_EPS = 1e-6
NEG_INF = -1e30
FORCED_BONUS = 1e4
ATTN_SCALE = HEAD_DIM ** -0.5
LOG2E = 1.4426950408889634
Q_SCALE = ATTN_SCALE * LOG2E

LANES = 128
VMEM_LIMIT_BYTES = 52 * 1024 * 1024
SEL_LANE0 = HEAD_DIM
MAX_SEL_BLOCKS = 32

F32 = jnp.float32
BF16 = jnp.bfloat16
HIGHEST = lax.Precision.HIGHEST

PROJ_TM = 512
FOX_TQ = 256
CMP_TQ = 256
NSA_TQ = 256
TAIL_TM = 512
FF_CHUNK = 1024
KV_STEP = 512
BIAS_LANE0 = HEAD_DIM


def _nt_dot(a, b):
    return lax.dot_general(a, b, (((1,), (1,)), ((), ())), preferred_element_type=F32)


def _sigmoid(t):
    return 1.0 / (1.0 + jnp.exp(-t))


def _resident(shape):
    nd = len(shape)
    return pl.BlockSpec(shape, lambda *_: (0,) * nd, pipeline_mode=pl.Buffered(1))


def _proj_layout():
    widths = [('fq', FOX_WIDTH), ('fk', FOX_WIDTH), ('fv', FOX_WIDTH),
              ('nq', NSA_WIDTH),
              ('ksl', NSA_KV_WIDTH), ('vsl', NSA_KV_WIDTH), ('kwn', NSA_KV_WIDTH), ('vwn', NSA_KV_WIDTH),
              ('kc', NSA_KV_WIDTH), ('vc', NSA_KV_WIDTH),
              ('misc', LANES)]
    out, c = {}, 0
    for name, w in widths:
        out[name] = (c, w)
        c += w
    return out, c


def _rearrange_w_in(w_in, d_model):
    splits = (FOX_WIDTH, FOX_WIDTH, FOX_WIDTH, FOX_HEADS,
              NSA_WIDTH, NSA_KV_WIDTH, NSA_KV_WIDTH, NSA_KV_WIDTH, NSA_KV_WIDTH, NSA_KV_WIDTH, NSA_KV_WIDTH,
              3 * NSA_HEADS, d_model, d_model)
    pts = [int(p) for p in np.cumsum(splits)[:-1]]
    fq, fk, fv, ff, nq, kc, vc, ksl, vsl, kwn, vwn, ng, ga, gb = jnp.split(w_in, pts, axis=-1)
    d = w_in.shape[0]
    misc = jnp.concatenate([ff, ng, jnp.zeros((d, LANES - FOX_HEADS - 3 * NSA_HEADS), w_in.dtype)], axis=-1)
    w_small = jnp.concatenate([fq, fk, fv, nq, ksl, vsl, kwn, vwn, kc, vc, misc], axis=-1)
    return w_small.astype(BF16), ga.astype(BF16), gb.astype(BF16)


def _rope_tables(seq):
    half = HEAD_DIM // 2
    inv = jnp.power(jnp.float32(ROPE_THETA), -jnp.arange(0, HEAD_DIM, 2, dtype=F32) / HEAD_DIM)
    ang = jnp.arange(seq, dtype=F32)[:, None] * inv[None, :]
    cos, sin = jnp.cos(ang), jnp.sin(ang)
    zero = jnp.zeros_like(sin)
    cos_t = jnp.concatenate([cos, cos, cos, cos], axis=-1)
    s_lo = jnp.concatenate([-sin, zero, -sin, zero], axis=-1)
    s_hi = jnp.concatenate([zero, sin, zero, sin], axis=-1)
    assert cos_t.shape == (seq, 4 * half)
    return cos_t, s_lo, s_hi


def _proj_kernel(x_ref, g_ref, w_ref, wga_ref, wgb_ref, cos_ref, slo_ref, shi_ref, bmisc_ref,
                 fq_ref, fk_ref, fv_ref, nq_ref, ksl_ref, vsl_ref, kwn_ref, vwn_ref,
                 kc_ref, vc_ref, misc_ref, gates_ref, ga_ref, gb_ref, *, tm, seq, layout):
    x = x_ref[...]
    h = x * lax.rsqrt(jnp.mean(x * x, axis=-1, keepdims=True) + RMS_EPS) * g_ref[...]
    h = h.astype(BF16)
    cos, slo, shi = cos_ref[...], slo_ref[...], shi_ref[...]
    half = HEAD_DIM // 2

    def mm(name):
        c0, n = layout[name]
        return jnp.dot(h, w_ref[:, c0:c0 + n], preferred_element_type=F32)

    def rope_block(t):
        return t * cos + pltpu.roll(t, LANES - half, axis=1) * slo + pltpu.roll(t, half, axis=1) * shi

    def rope(t):
        n = t.shape[1] // LANES
        return [rope_block(t[:, b * LANES:(b + 1) * LANES]) for b in range(n)]

    lane = lax.broadcasted_iota(jnp.int32, (tm, LANES), 1)
    left = lane < HEAD_DIM

    def blocks(t):
        return [t[:, b * LANES:(b + 1) * LANES] for b in range(t.shape[1] // LANES)]

    def split_heads(pairs, o_ref, scale, fill=0.0):
        for p, blk in enumerate(pairs):
            blk = blk * scale
            o_ref[:, (2 * p) * LANES:(2 * p + 1) * LANES] = jnp.where(left, blk, fill).astype(BF16)
            o_ref[:, (2 * p + 1) * LANES:(2 * p + 2) * LANES] = jnp.where(
                left, pltpu.roll(blk, HEAD_DIM, axis=1), fill).astype(BF16)

    def dup_heads(pairs, o_ref):
        for p, blk in enumerate(pairs):
            swapped = pltpu.roll(blk, HEAD_DIM, axis=1)
            o_ref[:, (2 * p) * LANES:(2 * p + 1) * LANES] = jnp.where(left, blk, swapped).astype(BF16)
            o_ref[:, (2 * p + 1) * LANES:(2 * p + 2) * LANES] = jnp.where(left, swapped, blk).astype(BF16)

    b0 = BIAS_LANE0
    split_heads(blocks(mm('fq')), fq_ref, Q_SCALE, fill=((lane >= b0 + 3) & (lane < b0 + 6)).astype(F32))
    split_heads(blocks(mm('fk')), fk_ref, 1.0, fill=((lane >= b0) & (lane < b0 + 3)).astype(F32))
    fv = mm('fv')
    for p in range(FOX_HEADS // 2):
        blk = fv[:, p * LANES:(p + 1) * LANES]
        fv_ref[:, (2 * p) * LANES:(2 * p + 1) * LANES] = jnp.where(
            left, blk, (lane == HEAD_DIM).astype(F32)).astype(BF16)
        fv_ref[:, (2 * p + 1) * LANES:(2 * p + 2) * LANES] = jnp.where(
            left, (lane == 0).astype(F32), blk).astype(BF16)

    split_heads(rope(mm('nq')), nq_ref, Q_SCALE)

    pos = (pl.program_id(0) % (seq // tm)) * tm + lax.broadcasted_iota(jnp.int32, (tm, LANES), 0)
    onehot = ((lane - SEL_LANE0) == (pos // SEL_BLOCK)).astype(F32)
    def mm_pair(a, b):
        c0 = layout[a][0]
        assert layout[b][0] == c0 + LANES and layout[a][1] == layout[b][1] == LANES
        t = jnp.dot(h, w_ref[:, c0:c0 + 2 * LANES], preferred_element_type=F32)
        return t[:, :LANES], t[:, LANES:]

    ksl, vsl = mm_pair('ksl', 'vsl')
    split_heads([rope_block(ksl)], ksl_ref, 1.0, fill=onehot)
    dup_heads([vsl], vsl_ref)
    kwn, vwn = mm_pair('kwn', 'vwn')
    split_heads([rope_block(kwn)], kwn_ref, 1.0)
    dup_heads([vwn], vwn_ref)

    kc, vc = mm_pair('kc', 'vc')
    kc_ref[...] = rope_block(kc)
    vc_ref[...] = vc

    t = mm('misc') + bmisc_ref[...]
    misc_ref[...] = -(jnp.maximum(-t, 0.0) + jnp.log(1.0 + jnp.exp(-jnp.abs(t))))
    per_group = 3 * NSA_HEADS_PER_GROUP
    for g in range(NSA_KV_GROUPS):
        gates_ref[:, g * LANES:(g + 1) * LANES] = _sigmoid(
            pltpu.roll(t, LANES - (FOX_HEADS + g * per_group), axis=1))

    ga_ref[...] = _sigmoid(jnp.dot(h, wga_ref[...], preferred_element_type=F32)).astype(BF16)
    gb_ref[...] = _sigmoid(jnp.dot(h, wgb_ref[...], preferred_element_type=F32)).astype(BF16)


def _proj(x2, g, w_small, w_ga, w_gb, tables, bmisc, seq):
    n_tok, d = x2.shape
    tm = PROJ_TM
    layout, n_cols = _proj_layout()
    assert n_tok % tm == 0 and seq % tm == 0 and w_small.shape == (d, n_cols)
    tiles_per_seq = seq // tm

    def row(w):
        return pl.BlockSpec((tm, w), lambda i: (i, 0))

    def tab():
        return pl.BlockSpec((tm, LANES), lambda i: (i % tiles_per_seq, 0))

    outs = [('fq', BF16), ('fk', BF16), ('fv', BF16), ('nq', BF16), ('ksl', BF16), ('vsl', BF16),
            ('kwn', BF16), ('vwn', BF16), ('kc', F32), ('vc', F32), ('misc', F32), ('gates', F32)]
    widths = dict(kc=NSA_KV_WIDTH, vc=NSA_KV_WIDTH, misc=LANES, gates=NSA_KV_GROUPS * LANES)
    widths.update({n: FOX_HEADS * LANES for n in ('fq', 'fk', 'fv')})
    widths.update(nq=NSA_HEADS * LANES, **{n: NSA_KV_GROUPS * LANES for n in ('ksl', 'vsl', 'kwn', 'vwn')})
    out_shape = [jax.ShapeDtypeStruct((n_tok, widths[n]), dt) for n, dt in outs]
    out_specs = [row(widths[n]) for n, _ in outs]
    out_shape += [jax.ShapeDtypeStruct((n_tok, d), BF16)] * 2
    out_specs += [row(d), row(d)]
    return pl.pallas_call(
        functools.partial(_proj_kernel, tm=tm, seq=seq, layout=layout),
        grid=(n_tok // tm,),
        in_specs=[row(d), _resident((1, d)), _resident(w_small.shape), _resident(w_ga.shape),
                  _resident(w_gb.shape), tab(), tab(), tab(), _resident((1, LANES))],
        out_specs=out_specs,
        out_shape=out_shape,
        compiler_params=pltpu.CompilerParams(dimension_semantics=("arbitrary",),
                                             vmem_limit_bytes=VMEM_LIMIT_BYTES),
        name="proj",
    )(x2, g, w_small, w_ga, w_gb, *tables, bmisc)


def _cumsum_kernel(m_ref, ccol_ref, *, seq):
    r = lax.broadcasted_iota(jnp.int32, (LANES, LANES), 0)
    c = lax.broadcasted_iota(jnp.int32, (LANES, LANES), 1)
    tri = (r >= c).astype(F32)
    carry = jnp.zeros((1, LANES), F32)
    for b in range(seq // LANES):
        blk = m_ref[0, b * LANES:(b + 1) * LANES, :]
        cs = jnp.dot(tri, blk, precision=HIGHEST, preferred_element_type=F32) + carry
        ccol_ref[0, b * LANES:(b + 1) * LANES, :] = cs
        carry = cs[LANES - 1:LANES, :]


def _cumsum(misc3):
    b, seq, _ = misc3.shape
    assert seq % LANES == 0
    return pl.pallas_call(
        functools.partial(_cumsum_kernel, seq=seq),
        grid=(b,),
        in_specs=[pl.BlockSpec((1, seq, LANES), lambda i: (i, 0, 0))],
        out_specs=pl.BlockSpec((1, seq, LANES), lambda i: (i, 0, 0)),
        out_shape=jax.ShapeDtypeStruct((b, seq, LANES), F32),
        compiler_params=pltpu.CompilerParams(dimension_semantics=("arbitrary",),
                                             vmem_limit_bytes=VMEM_LIMIT_BYTES),
        name="cumsum",
    )(misc3)


def _mask_table(tq):
    d = np.arange(tq)[:, None] - np.arange(tq)[None, :]
    neg = np.full((tq, tq), NEG_INF, np.float32)
    zero = np.zeros((tq, tq), np.float32)
    return jnp.asarray(np.stack([np.where(d > 0, zero, neg), zero, np.where(d <= 0, zero, neg), neg]))


def _mask_bias(tab_ref, key_tile0, n_tiles, q_tile, reps):
    rows = []
    for u in range(n_tiles):
        t = tab_ref[jnp.clip(key_tile0 + u - q_tile + 2, 0, 3)]
        rows.append(jnp.concatenate([t] * reps, axis=1) if reps > 1 else t)
    return jnp.concatenate(rows, axis=0) if n_tiles > 1 else rows[0]


def _split3(c):
    hi = c.astype(BF16).astype(F32)
    r = c - hi
    mid = r.astype(BF16).astype(F32)
    lo = (r - mid).astype(BF16).astype(F32)
    return hi, mid, lo


def _softmax_step_t(s, pv, m_ref, acc_ref, idx):
    m_prev = m_ref[idx]
    m_new = jnp.maximum(m_prev, jnp.max(s, axis=0, keepdims=True))
    p = jnp.exp2(s - m_new).astype(BF16)
    acc_ref[idx] = jnp.exp2(m_prev - m_new) * acc_ref[idx] + pv(p)
    m_ref[idx] = m_new


def _bias_selectors():
    sq = np.zeros((LANES, FOX_HEADS * LANES), np.float32)
    sk = np.zeros((LANES, FOX_HEADS * LANES), np.float32)
    for piece in range(3):
        for h in range(FOX_HEADS):
            sq[piece * FOX_HEADS + h, h * LANES + BIAS_LANE0 + piece] = 1.0
            sk[piece * FOX_HEADS + h, h * LANES + BIAS_LANE0 + 3 + piece] = -1.0
    return jnp.asarray(sq, BF16), jnp.asarray(sk, BF16)


def _fox_kernel(q_ref, k_ref, v_ref, ccol_ref, selq_ref, selk_ref, o_ref, kx_s, vt_s, qx_s, m_s, acc_s,
                *, tq, tk, seq):
    i = pl.program_id(1)
    lane = lax.broadcasted_iota(jnp.int32, (tq, LANES), 1)

    def bias_lanes(c, sel_ref):
        his, mids, los = (jnp.where(lane < FOX_HEADS, t, 0.0) for t in _split3(c * LOG2E))
        packed = his + pltpu.roll(mids, FOX_HEADS, axis=1) + pltpu.roll(los, 2 * FOX_HEADS, axis=1)
        return jnp.dot(packed.astype(BF16), sel_ref[...], preferred_element_type=F32)

    @pl.when(i == 0)
    def _build_keys():
        def chunk(r, carry):
            rows = pl.ds(pl.multiple_of(r * tq, tq), tq)
            e = bias_lanes(ccol_ref[0, rows, :], selk_ref)
            for h in range(FOX_HEADS):
                cols = slice(h * LANES, (h + 1) * LANES)
                kx_s[h, rows, :] = (k_ref[0, rows, cols].astype(F32) + e[:, cols]).astype(BF16)
            return carry

        lax.fori_loop(0, seq // tq, chunk, 0)

        def vchunk(c, carry):
            rows = pl.ds(pl.multiple_of(c * tk, tk), tk)
            for h in range(FOX_HEADS):
                cols = slice(h * LANES, (h + 1) * LANES)
                vt_s[c, h] = v_ref[0, rows, cols].astype(F32).T.astype(BF16)
            return carry

        lax.fori_loop(0, seq // tk, vchunk, 0)

    qrows = pl.ds(pl.multiple_of(i * tq, tq), tq)
    n_full = (i * tq) // tk
    row = lax.broadcasted_iota(jnp.int32, (LANES, tq), 0)
    n_pairs = FOX_HEADS // 2
    e = bias_lanes(ccol_ref[0, qrows, :], selq_ref)
    for h in range(FOX_HEADS):
        cols = slice(h * LANES, (h + 1) * LANES)
        qx_s[h] = (q_ref[0, :, cols].astype(F32) + e[:, cols]).astype(BF16)
    m_s[...] = jnp.full(m_s.shape, NEG_INF, F32)
    acc_s[...] = jnp.zeros(acc_s.shape, F32)

    def scores(j, p, nk=tk):
        rows = pl.ds(pl.multiple_of(j * tk, tk), nk)
        return tuple(_nt_dot(kx_s[2 * p + slot, rows, :], qx_s[2 * p + slot]) for slot in range(2))

    def causal(j, ss, nk):
        visible = (j * tk + lax.broadcasted_iota(jnp.int32, (nk, tq), 0)
                   <= i * tq + lax.broadcasted_iota(jnp.int32, (nk, tq), 1))
        return tuple(jnp.where(visible, s, NEG_INF) for s in ss)

    def consume(j, p, ss, nk=tk):
        for slot in range(2):
            h = 2 * p + slot
            _softmax_step_t(ss[slot], lambda pr, h=h: jnp.dot(vt_s[j, h, :, 0:nk], pr, preferred_element_type=F32),
                            m_s, acc_s, h)

    def key_step(j, ss, masked, last, nk=tk):
        for p in range(n_pairs):
            if p + 1 < n_pairs:
                nxt = scores(j, p + 1, nk)
                if masked:
                    nxt = causal(j, nxt, nk)
            else:
                nxt = None if last else scores(j + 1, 0)
            consume(j, p, ss, nk)
            ss = nxt
        return ss

    ss = lax.fori_loop(0, n_full, lambda j, ss: key_step(j, ss, False, False), scores(0, 0))

    def tail_short(ss):
        key_step(n_full, causal(n_full, scores(n_full, 0, tq), tq), True, True, tq)
        return 0

    def tail_full(ss):
        key_step(n_full, causal(n_full, ss, tk), True, True)
        return 0

    lax.cond((i * tq) % tk == 0, tail_short, tail_full, ss)

    for hp in range(n_pairs):
        a0, a1 = acc_s[2 * hp], acc_s[2 * hp + 1]
        ot = jnp.where(row < HEAD_DIM, a0 / a0[HEAD_DIM:HEAD_DIM + 1, :], a1 / a1[0:1, :])
        o_ref[0, :, hp * LANES:(hp + 1) * LANES] = ot.T.astype(BF16)


def _fox(fq, fk, fv, ccol):
    b, seq, w = fq.shape
    tq, tk = FOX_TQ, KV_STEP
    assert seq % tq == 0 and seq % tk == 0 and tk % tq == 0 and w == FOX_HEADS * LANES
    assert 3 * FOX_HEADS <= LANES and tk == 2 * tq
    sel_q, sel_k = _bias_selectors()
    whole = pl.BlockSpec((1, seq, w), lambda bi, i: (bi, 0, 0))
    return pl.pallas_call(
        functools.partial(_fox_kernel, tq=tq, tk=tk, seq=seq),
        grid=(b, seq // tq),
        in_specs=[pl.BlockSpec((1, tq, w), lambda bi, i: (bi, i, 0)), whole, whole,
                  pl.BlockSpec((1, seq, LANES), lambda bi, i: (bi, 0, 0)),
                  _resident(sel_q.shape), _resident(sel_k.shape)],
        out_specs=pl.BlockSpec((1, tq, FOX_WIDTH), lambda bi, i: (bi, i, 0)),
        out_shape=jax.ShapeDtypeStruct((b, seq, FOX_WIDTH), BF16),
        scratch_shapes=[pltpu.VMEM((FOX_HEADS, seq, LANES), BF16),
                        pltpu.VMEM((seq // tk, FOX_HEADS, LANES, tk), BF16),
                        pltpu.VMEM((FOX_HEADS, tq, LANES), BF16),
                        pltpu.VMEM((FOX_HEADS, 1, tq), F32), pltpu.VMEM((FOX_HEADS, LANES, tq), F32)],
        compiler_params=pltpu.CompilerParams(dimension_semantics=("arbitrary", "arbitrary"),
                                             vmem_limit_bytes=VMEM_LIMIT_BYTES),
        name="fox",
    )(fq, fk, fv, ccol, sel_q, sel_k)


def _compress_kernel(yk_ref, yv_ref, posk_ref, w1k_ref, w2k_ref, posv_ref, w1v_ref, w2v_ref,
                     kc_ref, vc_ref, *, nb):
    half = CMP_STRIDE * HEAD_DIM

    lane = lax.broadcasted_iota(jnp.int32, (nb, LANES), 1)
    left = lane < HEAD_DIM

    def regroup(x_ref):
        y = [[], []]
        for j in range(CMP_STRIDE // 2):
            xe = x_ref[0, pl.ds(2 * j, nb, stride=CMP_STRIDE), :]
            xo = x_ref[0, pl.ds(2 * j + 1, nb, stride=CMP_STRIDE), :]
            y[0].append(jnp.where(left, xe, pltpu.roll(xo, HEAD_DIM, axis=1)))
            y[1].append(jnp.where(left, pltpu.roll(xe, HEAD_DIM, axis=1), xo))
        return [jnp.concatenate(t, axis=1) for t in y]

    def comp(y, pos_ref, w1_ref, w2_ref):
        pos = pos_ref[...]
        def dot3(x, r0):
            xh = x.astype(BF16)
            xl = (x - xh.astype(F32)).astype(BF16)
            wh, wl = w1_ref[0, r0:r0 + half, :], w1_ref[1, r0:r0 + half, :]
            return (jnp.dot(xh, wh, preferred_element_type=F32) + jnp.dot(xh, wl, preferred_element_type=F32)
                    + jnp.dot(xl, wh, preferred_element_type=F32))

        a = dot3(y + pos[:, :half], 0)
        bm = dot3(y + pos[:, half:], half)
        pre = a + pltpu.roll(bm, nb - 1, axis=0)
        return jnp.dot(pre * _sigmoid(pre), w2_ref[...], precision=HIGHEST, preferred_element_type=F32)

    for g, (yk, yv) in enumerate(zip(regroup(yk_ref), regroup(yv_ref))):
        kc_ref[0, g] = comp(yk, posk_ref, w1k_ref, w2k_ref).astype(BF16)
        vc_ref[0, g] = comp(yv, posv_ref, w1v_ref, w2v_ref).astype(BF16)


def _compress(kc, vc, b, seq, pos_k, w1_k, w2_k, pos_v, w1_v, w2_v):
    g = NSA_KV_GROUPS
    nb = seq // CMP_STRIDE
    half = CMP_STRIDE * HEAD_DIM
    assert CMP_BLOCK == 2 * CMP_STRIDE and seq % CMP_STRIDE == 0 and nb % 8 == 0
    assert g == 2 and kc.shape[-1] == LANES

    zeros1 = jnp.zeros((CMP_BLOCK * HEAD_DIM, HEAD_DIM), F32)
    zeros2 = jnp.zeros((HEAD_DIM, HEAD_DIM), F32)
    def hi_lo(w):
        w = jnp.concatenate([w, zeros1], axis=-1)
        hi = w.astype(BF16)
        return jnp.stack([hi, (w - hi.astype(F32)).astype(BF16)])

    w1k, w1v = hi_lo(w1_k), hi_lo(w1_v)
    w2k = jnp.concatenate([jnp.concatenate([w2_k, zeros2], axis=-1), jnp.zeros((HEAD_DIM, LANES), F32)], axis=0)
    w2v = jnp.concatenate([jnp.concatenate([w2_v, w2_v], axis=-1), jnp.zeros((HEAD_DIM, LANES), F32)], axis=0)
    yspec = pl.BlockSpec((1, seq, LANES), lambda bi: (bi, 0, 0))
    ospec = pl.BlockSpec((1, g, nb, LANES), lambda bi: (bi, 0, 0, 0))
    return pl.pallas_call(
        functools.partial(_compress_kernel, nb=nb),
        grid=(b,),
        in_specs=[yspec, yspec,
                  _resident((1, 2 * half)), _resident(w1k.shape), _resident(w2k.shape),
                  _resident((1, 2 * half)), _resident(w1v.shape), _resident(w2v.shape)],
        out_specs=[ospec, ospec],
        out_shape=[jax.ShapeDtypeStruct((b, g, nb, LANES), BF16)] * 2,
        compiler_params=pltpu.CompilerParams(dimension_semantics=("arbitrary",),
                                             vmem_limit_bytes=VMEM_LIMIT_BYTES),
        name="compress",
    )(kc.reshape(b, seq, LANES), vc.reshape(b, seq, LANES), pos_k.reshape(1, 2 * half), w1k, w2k,
      pos_v.reshape(1, 2 * half), w1v, w2v)


def _overlap_matrix(nb, n_cmp, n_sel):
    ci = np.arange(nb)[:, None] * CMP_STRIDE
    sj = (np.arange(LANES)[None, :] - SEL_LANE0) * SEL_BLOCK
    jj = np.arange(LANES)[None, :] - SEL_LANE0
    ov = (ci < sj + SEL_BLOCK) & (ci + CMP_BLOCK > sj) & (jj >= 0) & (jj < n_sel) & (np.arange(nb)[:, None] < n_cmp)
    return jnp.asarray(ov.astype(np.float32))


def _cmpsel_kernel(q_ref, kc_ref, vc_ref, ovt_ref, qs_ref, ocmp_ref, vct_s, *, tq, n_cmp, nb):
    i = pl.program_id(1)
    n_g, hg = NSA_KV_GROUPS, NSA_HEADS_PER_GROUP

    @pl.when(i == 0)
    def _transpose_values():
        for g in range(n_g):
            vct_s[g] = vc_ref[0, g].astype(F32).T.astype(BF16)

    n_idx = lax.broadcasted_iota(jnp.int32, (nb, tq), 0)
    t_pos = i * tq + lax.broadcasted_iota(jnp.int32, (nb, tq), 1)
    cmask = (n_idx * CMP_STRIDE + (CMP_BLOCK - 1) <= t_pos) & (n_idx < n_cmp)
    top = lax.broadcasted_iota(jnp.int32, (LANES, tq), 0) < HEAD_DIM
    psums = []
    for g in range(n_g):
        kc = kc_ref[0, g]
        vct = vct_s[g]
        psum = jnp.zeros((nb, tq), F32)
        outs = []
        for h in range(hg):
            cols = slice((g * hg + h) * LANES, (g * hg + h + 1) * LANES)
            s = jnp.where(cmask, _nt_dot(kc, q_ref[0, :, cols]), NEG_INF)
            m = jnp.max(s, axis=0, keepdims=True)
            p = jnp.where(cmask, jnp.exp2(s - m), 0.0)
            p = p / jnp.maximum(jnp.sum(p, axis=0, keepdims=True), 1e-30)
            psum = psum + p
            outs.append(jnp.dot(vct, p.astype(BF16), preferred_element_type=F32))
        psums.append(psum)
        for hp in range(hg // 2):
            ocols = slice((g * (hg // 2) + hp) * LANES, (g * (hg // 2) + hp + 1) * LANES)
            ocmp_ref[0, :, ocols] = jnp.where(top, outs[2 * hp], outs[2 * hp + 1]).T

    wide = n_g * tq
    ovt = ovt_ref[...]
    p_slc = sum(jnp.dot(ovt, piece.astype(BF16), preferred_element_type=F32)
                for piece in _split3(jnp.concatenate(psums, axis=1)))
    sc = p_slc[SEL_LANE0:SEL_LANE0 + MAX_SEL_BLOCKS, :]
    j = lax.broadcasted_iota(jnp.int32, (MAX_SEL_BLOCKS, wide), 0)
    cur = (i * tq + (lax.broadcasted_iota(jnp.int32, (MAX_SEL_BLOCKS, wide), 1) & (tq - 1))) // SEL_BLOCK
    forced = (j == 0) | (j == cur) | (j == cur - 1)
    score = jnp.where(j <= cur, sc + FORCED_BONUS * forced.astype(F32), NEG_INF)
    j_f = j.astype(F32)
    sel = jnp.zeros((MAX_SEL_BLOCKS, wide), jnp.bool_)
    for _ in range(SEL_TOPN):
        mx = jnp.max(score, axis=0, keepdims=True)
        first = jnp.min(jnp.where(score == mx, j_f, float(MAX_SEL_BLOCKS)), axis=0, keepdims=True)
        chosen = j_f == first
        sel = sel | chosen
        score = jnp.where(chosen, -jnp.inf, score)
    bias_t = jnp.concatenate([jnp.zeros((SEL_LANE0, wide), F32), jnp.where(sel, 0.0, NEG_INF),
                              jnp.zeros((LANES - SEL_LANE0 - MAX_SEL_BLOCKS, wide), F32)], axis=0)
    for g in range(n_g):
        bias = bias_t[:, g * tq:(g + 1) * tq].T
        for h in range(hg):
            cols = slice((g * hg + h) * LANES, (g * hg + h + 1) * LANES)
            qs_ref[0, :, cols] = (q_ref[0, :, cols].astype(F32) + bias).astype(BF16)


def _cmpsel(nq3, kcmp, vcmp):
    b, seq, _ = nq3.shape
    g, hg = NSA_KV_GROUPS, NSA_HEADS_PER_GROUP
    tq = CMP_TQ
    nb = kcmp.shape[2]
    n_cmp = (seq - CMP_BLOCK) // CMP_STRIDE + 1
    n_sel = seq // SEL_BLOCK
    assert seq % tq == 0 and n_sel <= MAX_SEL_BLOCKS and n_sel >= SEL_TOPN
    ov = _overlap_matrix(nb, n_cmp, n_sel).T.astype(BF16)
    assert tq & (tq - 1) == 0
    qspec = pl.BlockSpec((1, tq, g * hg * LANES), lambda bi, i: (bi, i, 0))
    cspec = pl.BlockSpec((1, g, nb, LANES), lambda bi, i: (bi, 0, 0, 0))
    return pl.pallas_call(
        functools.partial(_cmpsel_kernel, tq=tq, n_cmp=n_cmp, nb=nb),
        grid=(b, seq // tq),
        in_specs=[qspec, cspec, cspec, _resident(ov.shape)],
        out_specs=[qspec, pl.BlockSpec((1, tq, NSA_WIDTH), lambda bi, i: (bi, i, 0))],
        out_shape=[jax.ShapeDtypeStruct(nq3.shape, BF16),
                   jax.ShapeDtypeStruct((b, seq, NSA_WIDTH), F32)],
        scratch_shapes=[pltpu.VMEM((g, LANES, nb), BF16)],
        compiler_params=pltpu.CompilerParams(dimension_semantics=("arbitrary",) * 2,
                                             vmem_limit_bytes=VMEM_LIMIT_BYTES),
        name="cmpsel",
    )(nq3, kcmp, vcmp, ov)


def _nsa_kernel(qs_ref, ks_ref, vs_ref, kw_ref, vw_ref, ocmp_ref, gates_ref, tab_ref, o_ref,
                vse_s, vso_s, vwe_s, vwo_s, m_s, acc_s, *, tq, tk, seq):
    i = pl.program_id(1)
    n_g, hg = NSA_KV_GROUPS, NSA_HEADS_PER_GROUP
    rows_all = hg * tq
    half = rows_all // 2
    wt = WINDOW // tq + 1

    @pl.when(i == 0)
    def _build_values():
        lane = lax.broadcasted_iota(jnp.int32, (tk, LANES), 1)
        one_e = (lane == HEAD_DIM).astype(F32)
        one_o = (lane == 0).astype(F32)

        def chunk(c, carry):
            rows = pl.ds(pl.multiple_of(c * tk, tk), tk)
            for g in range(n_g):
                cols = slice(g * LANES, (g + 1) * LANES)
                v = vs_ref[0, rows, cols].astype(F32)
                vse_s[g, c] = jnp.where(lane < HEAD_DIM, v, one_e).T.astype(BF16)
                vso_s[g, c] = jnp.where(lane < HEAD_DIM, one_o, v).T.astype(BF16)
                w = vw_ref[0, rows, cols].astype(F32)
                we = jnp.where(lane < HEAD_DIM, w, one_e).T.astype(BF16)
                wo = jnp.where(lane < HEAD_DIM, one_o, w).T.astype(BF16)
                for t in range(tk // tq):
                    vwe_s[g, c * (tk // tq) + t] = we[:, t * tq:(t + 1) * tq]
                    vwo_s[g, c * (tk // tq) + t] = wo[:, t * tq:(t + 1) * tq]
            return carry

        lax.fori_loop(0, seq // tk, chunk, 0)

    order = tuple(range(0, hg, 2)) + tuple(range(1, hg, 2))
    q = [jnp.concatenate([qs_ref[0, :, (g * hg + h) * LANES:(g * hg + h + 1) * LANES] for h in order], axis=0)
         for g in range(n_g)]

    n_full = (i * tq) // tk
    m_s[...] = jnp.full(m_s.shape, NEG_INF, F32)
    acc_s[...] = jnp.zeros(acc_s.shape, F32)

    def scores(j, g):
        rows = pl.ds(pl.multiple_of(j * tk, tk), tk)
        return _nt_dot(ks_ref[0, rows, g * LANES:(g + 1) * LANES], q[g])

    def consume(j, g, s, masked):
        if masked:
            s = s + _mask_bias(tab_ref, j * (tk // tq), tk // tq, i, hg)
        _softmax_step_t(s, lambda p: jnp.concatenate(
            [jnp.dot(vse_s[g, j], p[:, :half], preferred_element_type=F32),
             jnp.dot(vso_s[g, j], p[:, half:], preferred_element_type=F32)], axis=1), m_s, acc_s, g)

    st = jnp.maximum(i - (wt - 1), 0)

    def win_scores(g):
        wrows = pl.ds(pl.multiple_of(st * tq, tq), wt * tq)
        return _nt_dot(kw_ref[0, wrows, g * LANES:(g + 1) * LANES], q[g]) + _mask_bias(tab_ref, st, wt, i, hg)

    def win_consume(g, s_win):
        p = jnp.exp2(s_win - jnp.max(s_win, axis=0, keepdims=True)).astype(BF16)
        a_win = jnp.zeros((LANES, rows_all), F32)
        for t in range(wt):
            pt = p[t * tq:(t + 1) * tq]
            a_win = a_win + jnp.concatenate(
                [jnp.dot(vwe_s[g, st + t], pt[:, :half], preferred_element_type=F32),
                 jnp.dot(vwo_s[g, st + t], pt[:, half:], preferred_element_type=F32)], axis=1)
        return a_win

    def key_step(j, s):
        for g in range(n_g):
            nxt = scores(j, g + 1) if g + 1 < n_g else scores(j + 1, 0)
            consume(j, g, s, False)
            s = nxt
        return s

    s = lax.fori_loop(0, n_full, key_step, scores(0, 0))

    for g in range(n_g):
        if g + 1 < n_g:
            nxt = scores(n_full, g + 1) + _mask_bias(tab_ref, n_full * (tk // tq), tk // tq, i, hg)
        else:
            nxt = win_scores(0)
        consume(n_full, g, s, g == 0)
        s = nxt
    a_win = []
    for g in range(n_g):
        nxt = win_scores(g + 1) if g + 1 < n_g else None
        a_win.append(win_consume(g, s))
        s = nxt

    lane = lax.broadcasted_iota(jnp.int32, (tq, LANES), 1)
    left = lane < HEAD_DIM
    top = lax.broadcasted_iota(jnp.int32, (LANES, tq), 0) < HEAD_DIM
    for g in range(n_g):
        gates = gates_ref[0, :, g * LANES:(g + 1) * LANES]
        a_sel = acc_s[g]
        for hp in range(hg // 2):
            h0, h1 = 2 * hp, 2 * hp + 1
            cols = slice((g * (hg // 2) + hp) * LANES, (g * (hg // 2) + hp + 1) * LANES)

            def pair(a):
                ae = a[:, hp * tq:(hp + 1) * tq]
                ao = a[:, half + hp * tq:half + (hp + 1) * tq]
                return jnp.where(top, ae / jnp.maximum(ae[HEAD_DIM:HEAD_DIM + 1, :], 1e-30),
                                 ao / jnp.maximum(ao[0:1, :], 1e-30)).T

            def gate(branch):
                return jnp.where(left, gates[:, 3 * h0 + branch:3 * h0 + branch + 1],
                                 gates[:, 3 * h1 + branch:3 * h1 + branch + 1])

            o = gate(0) * ocmp_ref[0, :, cols] + gate(1) * pair(a_sel) + gate(2) * pair(a_win[g])
            o_ref[0, :, cols] = o.astype(BF16)


def _nsa(qs, ksl, vsl, kwn, vwn, ocmp, gates):
    b, seq, _ = qs.shape
    g, hg = NSA_KV_GROUPS, NSA_HEADS_PER_GROUP
    tq, tk = NSA_TQ, KV_STEP
    assert seq % tk == 0 and tk % tq == 0 and WINDOW % tq == 0 and seq >= WINDOW + tq and tq & (tq - 1) == 0
    assert hg % 2 == 0
    assert WINDOW == 2 * tq
    kv = pl.BlockSpec((1, seq, g * LANES), lambda bi, i: (bi, 0, 0))
    ospec = pl.BlockSpec((1, tq, NSA_WIDTH), lambda bi, i: (bi, i, 0))
    return pl.pallas_call(
        functools.partial(_nsa_kernel, tq=tq, tk=tk, seq=seq),
        grid=(b, seq // tq),
        in_specs=[pl.BlockSpec((1, tq, g * hg * LANES), lambda bi, i: (bi, i, 0)), kv, kv, kv, kv,
                  ospec, pl.BlockSpec((1, tq, g * LANES), lambda bi, i: (bi, i, 0)), _resident((4, tq, tq))],
        out_specs=ospec,
        out_shape=jax.ShapeDtypeStruct((b, seq, NSA_WIDTH), BF16),
        scratch_shapes=[pltpu.VMEM((g, seq // tk, LANES, tk), BF16)] * 2
        + [pltpu.VMEM((g, seq // tq, LANES, tq), BF16)] * 2
        + [pltpu.VMEM((g, 1, hg * tq), F32), pltpu.VMEM((g, LANES, hg * tq), F32)],
        compiler_params=pltpu.CompilerParams(dimension_semantics=("arbitrary",) * 2,
                                             vmem_limit_bytes=VMEM_LIMIT_BYTES),
        name="nsa",
    )(qs, ksl, vsl, kwn, vwn, ocmp, gates, _mask_table(tq))


def _rms(y, g):
    return y * lax.rsqrt(jnp.mean(y * y, axis=-1, keepdims=True) + RMS_EPS) * g


def _tail_kernel(x_ref, fox_ref, nsa_ref, ga_ref, gb_ref, wf_ref, wn_ref, wo_ref, wup_ref, wdn_ref,
                 gmix_ref, gpre_ref, gpost_ref, o_ref, *, d_ff):
    a = jnp.dot(fox_ref[...], wf_ref[...], preferred_element_type=F32)
    b = jnp.dot(nsa_ref[...], wn_ref[...], preferred_element_type=F32)
    mix = ga_ref[...].astype(F32) * a + gb_ref[...].astype(F32) * b
    y = jnp.dot(mix.astype(BF16), wo_ref[...], preferred_element_type=F32)
    x1 = x_ref[...] + _rms(y, gmix_ref[...])
    h = _rms(x1, gpre_ref[...]).astype(BF16)
    acc = jnp.zeros(x1.shape, F32)
    for c in range(d_ff // FF_CHUNK):
        cols = slice(c * FF_CHUNK, (c + 1) * FF_CHUNK)
        u = jnp.maximum(jnp.dot(h, wup_ref[:, cols], preferred_element_type=F32), 0.0)
        acc = acc + jnp.dot((u * u).astype(BF16), wdn_ref[cols, :], preferred_element_type=F32)
    o_ref[...] = x1 + _rms(acc, gpost_ref[...])


def _tail(x2, fox, nsa, sga, sgb, wf, wn, wo, wup, wdn, g_mix_post, g_mlp_pre, g_mlp_post):
    n_tok, d = x2.shape
    d_ff = wup.shape[1]
    tm = TAIL_TM
    assert n_tok % tm == 0 and d_ff % FF_CHUNK == 0

    def row(w):
        return pl.BlockSpec((tm, w), lambda i: (i, 0))

    return pl.pallas_call(
        functools.partial(_tail_kernel, d_ff=d_ff),
        grid=(n_tok // tm,),
        in_specs=[row(d), row(fox.shape[1]), row(nsa.shape[1]), row(d), row(d),
                  _resident(wf.shape), _resident(wn.shape), _resident(wo.shape),
                  _resident(wup.shape), _resident(wdn.shape),
                  _resident((1, d)), _resident((1, d)), _resident((1, d))],
        out_specs=row(d),
        out_shape=jax.ShapeDtypeStruct((n_tok, d), F32),
        compiler_params=pltpu.CompilerParams(dimension_semantics=("arbitrary",),
                                             vmem_limit_bytes=VMEM_LIMIT_BYTES),
        name="tail",
    )(x2, fox, nsa, sga, sgb, wf, wn, wo, wup, wdn, g_mix_post, g_mlp_pre, g_mlp_post)


def _layer(x, norm_mix_pre, norm_mix_post, norm_mlp_pre, norm_mlp_post, w_in, b_forget,
           cmp_pos_k, cmp_w1_k, cmp_w2_k, cmp_pos_v, cmp_w1_v, cmp_w2_v,
           w_fox_out, w_nsa_out, w_o, w_up, w_down):
    b, seq, d = x.shape
    n_tok = b * seq
    x2 = x.reshape(n_tok, d)
    w_small, w_ga, w_gb = _rearrange_w_in(w_in, d)
    bmisc = jnp.concatenate([b_forget.astype(F32), jnp.zeros((LANES - FOX_HEADS,), F32)]).reshape(1, LANES)
    (fq, fk, fv, nq, ksl, vsl, kwn, vwn, kc, vc, misc, gates, sga, sgb) = _proj(
        x2, norm_mix_pre.reshape(1, d), w_small, w_ga, w_gb, _rope_tables(seq), bmisc, seq)

    def seq3(t):
        return t.reshape(b, seq, t.shape[-1])

    fox = _fox(seq3(fq), seq3(fk), seq3(fv), _cumsum(seq3(misc)))

    kcmp, vcmp = _compress(kc, vc, b, seq, cmp_pos_k, cmp_w1_k, cmp_w2_k, cmp_pos_v, cmp_w1_v, cmp_w2_v)
    qs, ocmp = _cmpsel(seq3(nq), kcmp, vcmp)
    nsa = _nsa(qs, seq3(ksl), seq3(vsl), seq3(kwn), seq3(vwn), ocmp, seq3(gates))

    out = _tail(x2, fox.reshape(n_tok, FOX_WIDTH), nsa.reshape(n_tok, NSA_WIDTH), sga, sgb,
                w_fox_out.astype(BF16), w_nsa_out.astype(BF16), w_o.astype(BF16),
                w_up.astype(BF16), w_down.astype(BF16),
                norm_mix_post.reshape(1, d), norm_mlp_pre.reshape(1, d), norm_mlp_post.reshape(1, d))
    return out.reshape(b, seq, d)


def kernel(x, norm_mix_pre, norm_mix_post, norm_mlp_pre, norm_mlp_post, w_in, b_forget, cmp_pos_k, cmp_w1_k, cmp_w2_k, cmp_pos_v, cmp_w1_v, cmp_w2_v, w_fox_out, w_nsa_out, w_o, w_up, w_down):
    params = (norm_mix_pre, norm_mix_post, norm_mlp_pre, norm_mlp_post, w_in, b_forget,
              cmp_pos_k, cmp_w1_k, cmp_w2_k, cmp_pos_v, cmp_w1_v, cmp_w2_v,
              w_fox_out, w_nsa_out, w_o, w_up, w_down)
    for layer in range(w_in.shape[0]):
        x = _layer(x, *[p[layer] for p in params])
    return x
```

```python
import functools

import numpy as np
import jax
import jax.numpy as jnp
from jax import lax
from jax.experimental import pallas as pl
from jax.experimental.pallas import tpu as pltpu

HEAD_DIM = 64
FOX_HEADS = 8
NSA_HEADS = 8
NSA_KV_GROUPS = 2
NSA_HEADS_PER_GROUP = NSA_HEADS // NSA_KV_GROUPS
FOX_WIDTH = FOX_HEADS * HEAD_DIM
NSA_WIDTH = NSA_HEADS * HEAD_DIM
NSA_KV_WIDTH = NSA_KV_GROUPS * HEAD_DIM
ROPE_THETA = 10000.0
CMP_BLOCK = 32
CMP_STRIDE = 16
SEL_BLOCK = 64
SEL_TOPN = 8
WINDOW = 512
RMS_EPS = 1e-6
NEG_INF = -1e30
FORCED_BONUS = 1e4
ATTN_SCALE = HEAD_DIM ** -0.5
LOG2E = 1.4426950408889634
Q_SCALE = ATTN_SCALE * LOG2E

LANES = 128
VMEM_LIMIT_BYTES = 52 * 1024 * 1024
SEL_LANE0 = HEAD_DIM
MAX_SEL_BLOCKS = 32

F32 = jnp.float32
BF16 = jnp.bfloat16
HIGHEST = lax.Precision.HIGHEST

PROJ_TM = 512
FOX_TQ = 256
CMP_TQ = 256
NSA_TQ = 256
TAIL_TM = 512
FF_CHUNK = 1024
KV_STEP = 512
BIAS_LANE0 = HEAD_DIM


def _nt_dot(a, b):
    return lax.dot_general(a, b, (((1,), (1,)), ((), ())), preferred_element_type=F32)


def _sigmoid(t):
    return 1.0 / (1.0 + jnp.exp(-t))


def _resident(shape):
    nd = len(shape)
    return pl.BlockSpec(shape, lambda *_: (0,) * nd, pipeline_mode=pl.Buffered(1))


def _proj_layout():
    widths = [('fq', FOX_WIDTH), ('fk', FOX_WIDTH), ('fv', FOX_WIDTH),
              ('nq', NSA_WIDTH),
              ('ksl', NSA_KV_WIDTH), ('vsl', NSA_KV_WIDTH), ('kwn', NSA_KV_WIDTH), ('vwn', NSA_KV_WIDTH),
              ('kc', NSA_KV_WIDTH), ('vc', NSA_KV_WIDTH),
              ('misc', LANES)]
    out, c = {}, 0
    for name, w in widths:
        out[name] = (c, w)
        c += w
    return out, c


def _rearrange_w_in(w_in, d_model):
    splits = (FOX_WIDTH, FOX_WIDTH, FOX_WIDTH, FOX_HEADS,
              NSA_WIDTH, NSA_KV_WIDTH, NSA_KV_WIDTH, NSA_KV_WIDTH, NSA_KV_WIDTH, NSA_KV_WIDTH, NSA_KV_WIDTH,
              3 * NSA_HEADS, d_model, d_model)
    pts = [int(p) for p in np.cumsum(splits)[:-1]]
    fq, fk, fv, ff, nq, kc, vc, ksl, vsl, kwn, vwn, ng, ga, gb = jnp.split(w_in, pts, axis=-1)
    d = w_in.shape[0]
    misc = jnp.concatenate([ff, ng, jnp.zeros((d, LANES - FOX_HEADS - 3 * NSA_HEADS), w_in.dtype)], axis=-1)
    w_small = jnp.concatenate([fq, fk, fv, nq, ksl, vsl, kwn, vwn, kc, vc, misc], axis=-1)
    return w_small.astype(BF16), ga.astype(BF16), gb.astype(BF16)


def _rope_tables(seq):
    half = HEAD_DIM // 2
    inv = jnp.power(jnp.float32(ROPE_THETA), -jnp.arange(0, HEAD_DIM, 2, dtype=F32) / HEAD_DIM)
    ang = jnp.arange(seq, dtype=F32)[:, None] * inv[None, :]
    cos, sin = jnp.cos(ang), jnp.sin(ang)
    zero = jnp.zeros_like(sin)
    cos_t = jnp.concatenate([cos, cos, cos, cos], axis=-1)
    s_lo = jnp.concatenate([-sin, zero, -sin, zero], axis=-1)
    s_hi = jnp.concatenate([zero, sin, zero, sin], axis=-1)
    assert cos_t.shape == (seq, 4 * half)
    return cos_t, s_lo, s_hi


def _proj_kernel(x_ref, g_ref, w_ref, wga_ref, wgb_ref, cos_ref, slo_ref, shi_ref, bmisc_ref,
                 fq_ref, fk_ref, fv_ref, nq_ref, ksl_ref, vsl_ref, kwn_ref, vwn_ref,
                 kc_ref, vc_ref, misc_ref, gates_ref, ga_ref, gb_ref, *, tm, seq, layout):
    x = x_ref[...]
    h = x * lax.rsqrt(jnp.mean(x * x, axis=-1, keepdims=True) + RMS_EPS) * g_ref[...]
    h = h.astype(BF16)
    cos, slo, shi = cos_ref[...], slo_ref[...], shi_ref[...]
    half = HEAD_DIM // 2

    def mm(name):
        c0, n = layout[name]
        return jnp.dot(h, w_ref[:, c0:c0 + n], preferred_element_type=F32)

    def rope_block(t):
        return t * cos + pltpu.roll(t, LANES - half, axis=1) * slo + pltpu.roll(t, half, axis=1) * shi

    def rope(t):
        n = t.shape[1] // LANES
        return [rope_block(t[:, b * LANES:(b + 1) * LANES]) for b in range(n)]

    lane = lax.broadcasted_iota(jnp.int32, (tm, LANES), 1)
    left = lane < HEAD_DIM

    def blocks(t):
        return [t[:, b * LANES:(b + 1) * LANES] for b in range(t.shape[1] // LANES)]

    def split_heads(pairs, o_ref, scale, fill=0.0):
        for p, blk in enumerate(pairs):
            blk = blk * scale
            o_ref[:, (2 * p) * LANES:(2 * p + 1) * LANES] = jnp.where(left, blk, fill).astype(BF16)
            o_ref[:, (2 * p + 1) * LANES:(2 * p + 2) * LANES] = jnp.where(
                left, pltpu.roll(blk, HEAD_DIM, axis=1), fill).astype(BF16)

    def dup_heads(pairs, o_ref):
        for p, blk in enumerate(pairs):
            swapped = pltpu.roll(blk, HEAD_DIM, axis=1)
            o_ref[:, (2 * p) * LANES:(2 * p + 1) * LANES] = jnp.where(left, blk, swapped).astype(BF16)
            o_ref[:, (2 * p + 1) * LANES:(2 * p + 2) * LANES] = jnp.where(left, swapped, blk).astype(BF16)

    b0 = BIAS_LANE0
    split_heads(blocks(mm('fq')), fq_ref, Q_SCALE, fill=((lane >= b0 + 3) & (lane < b0 + 6)).astype(F32))
    split_heads(blocks(mm('fk')), fk_ref, 1.0, fill=((lane >= b0) & (lane < b0 + 3)).astype(F32))
    fv = mm('fv')
    for p in range(FOX_HEADS // 2):
        blk = fv[:, p * LANES:(p + 1) * LANES]
        fv_ref[:, (2 * p) * LANES:(2 * p + 1) * LANES] = jnp.where(
            left, blk, (lane == HEAD_DIM).astype(F32)).astype(BF16)
        fv_ref[:, (2 * p + 1) * LANES:(2 * p + 2) * LANES] = jnp.where(
            left, (lane == 0).astype(F32), blk).astype(BF16)

    split_heads(rope(mm('nq')), nq_ref, Q_SCALE)

    pos = (pl.program_id(0) % (seq // tm)) * tm + lax.broadcasted_iota(jnp.int32, (tm, LANES), 0)
    onehot = ((lane - SEL_LANE0) == (pos // SEL_BLOCK)).astype(F32)
    def mm_pair(a, b):
        c0 = layout[a][0]
        assert layout[b][0] == c0 + LANES and layout[a][1] == layout[b][1] == LANES
        t = jnp.dot(h, w_ref[:, c0:c0 + 2 * LANES], preferred_element_type=F32)
        return t[:, :LANES], t[:, LANES:]

    ksl, vsl = mm_pair('ksl', 'vsl')
    split_heads([rope_block(ksl)], ksl_ref, 1.0, fill=onehot)
    dup_heads([vsl], vsl_ref)
    kwn, vwn = mm_pair('kwn', 'vwn')
    split_heads([rope_block(kwn)], kwn_ref, 1.0)
    dup_heads([vwn], vwn_ref)

    kc, vc = mm_pair('kc', 'vc')
    kc_ref[...] = rope_block(kc)
    vc_ref[...] = vc

    t = mm('misc') + bmisc_ref[...]
    misc_ref[...] = -(jnp.maximum(-t, 0.0) + jnp.log(1.0 + jnp.exp(-jnp.abs(t))))
    per_group = 3 * NSA_HEADS_PER_GROUP
    for g in range(NSA_KV_GROUPS):
        gates_ref[:, g * LANES:(g + 1) * LANES] = _sigmoid(
            pltpu.roll(t, LANES - (FOX_HEADS + g * per_group), axis=1))

    ga_ref[...] = _sigmoid(jnp.dot(h, wga_ref[...], preferred_element_type=F32)).astype(BF16)
    gb_ref[...] = _sigmoid(jnp.dot(h, wgb_ref[...], preferred_element_type=F32)).astype(BF16)


def _proj(x2, g, w_small, w_ga, w_gb, tables, bmisc, seq):
    n_tok, d = x2.shape
    tm = PROJ_TM
    layout, n_cols = _proj_layout()
    assert n_tok % tm == 0 and seq % tm == 0 and w_small.shape == (d, n_cols)
    tiles_per_seq = seq // tm

    def row(w):
        return pl.BlockSpec((tm, w), lambda i: (i, 0))

    def tab():
        return pl.BlockSpec((tm, LANES), lambda i: (i % tiles_per_seq, 0))

    outs = [('fq', BF16), ('fk', BF16), ('fv', BF16), ('nq', BF16), ('ksl', BF16), ('vsl', BF16),
            ('kwn', BF16), ('vwn', BF16), ('kc', F32), ('vc', F32), ('misc', F32), ('gates', F32)]
    widths = dict(kc=NSA_KV_WIDTH, vc=NSA_KV_WIDTH, misc=LANES, gates=NSA_KV_GROUPS * LANES)
    widths.update({n: FOX_HEADS * LANES for n in ('fq', 'fk', 'fv')})
    widths.update(nq=NSA_HEADS * LANES, **{n: NSA_KV_GROUPS * LANES for n in ('ksl', 'vsl', 'kwn', 'vwn')})
    out_shape = [jax.ShapeDtypeStruct((n_tok, widths[n]), dt) for n, dt in outs]
    out_specs = [row(widths[n]) for n, _ in outs]
    out_shape += [jax.ShapeDtypeStruct((n_tok, d), BF16)] * 2
    out_specs += [row(d), row(d)]
    return pl.pallas_call(
        functools.partial(_proj_kernel, tm=tm, seq=seq, layout=layout),
        grid=(n_tok // tm,),
        in_specs=[row(d), _resident((1, d)), _resident(w_small.shape), _resident(w_ga.shape),
                  _resident(w_gb.shape), tab(), tab(), tab(), _resident((1, LANES))],
        out_specs=out_specs,
        out_shape=out_shape,
        compiler_params=pltpu.CompilerParams(dimension_semantics=("arbitrary",),
                                             vmem_limit_bytes=VMEM_LIMIT_BYTES),
        name="proj",
    )(x2, g, w_small, w_ga, w_gb, *tables, bmisc)


def _cumsum_kernel(m_ref, ccol_ref, *, seq):
    r = lax.broadcasted_iota(jnp.int32, (LANES, LANES), 0)
    c = lax.broadcasted_iota(jnp.int32, (LANES, LANES), 1)
    tri = (r >= c).astype(F32)
    carry = jnp.zeros((1, LANES), F32)
    for b in range(seq // LANES):
        blk = m_ref[0, b * LANES:(b + 1) * LANES, :]
        cs = jnp.dot(tri, blk, precision=HIGHEST, preferred_element_type=F32) + carry
        ccol_ref[0, b * LANES:(b + 1) * LANES, :] = cs
        carry = cs[LANES - 1:LANES, :]


def _cumsum(misc3):
    b, seq, _ = misc3.shape
    assert seq % LANES == 0
    return pl.pallas_call(
        functools.partial(_cumsum_kernel, seq=seq),
        grid=(b,),
        in_specs=[pl.BlockSpec((1, seq, LANES), lambda i: (i, 0, 0))],
        out_specs=pl.BlockSpec((1, seq, LANES), lambda i: (i, 0, 0)),
        out_shape=jax.ShapeDtypeStruct((b, seq, LANES), F32),
        compiler_params=pltpu.CompilerParams(dimension_semantics=("arbitrary",),
                                             vmem_limit_bytes=VMEM_LIMIT_BYTES),
        name="cumsum",
    )(misc3)


def _mask_table(tq):
    d = np.arange(tq)[:, None] - np.arange(tq)[None, :]
    neg = np.full((tq, tq), NEG_INF, np.float32)
    zero = np.zeros((tq, tq), np.float32)
    return jnp.asarray(np.stack([np.where(d > 0, zero, neg), zero, np.where(d <= 0, zero, neg), neg]))


def _mask_bias(tab_ref, key_tile0, n_tiles, q_tile, reps):
    rows = []
    for u in range(n_tiles):
        t = tab_ref[jnp.clip(key_tile0 + u - q_tile + 2, 0, 3)]
        rows.append(jnp.concatenate([t] * reps, axis=1) if reps > 1 else t)
    return jnp.concatenate(rows, axis=0) if n_tiles > 1 else rows[0]


def _split3(c):
    hi = c.astype(BF16).astype(F32)
    r = c - hi
    mid = r.astype(BF16).astype(F32)
    lo = (r - mid).astype(BF16).astype(F32)
    return hi, mid, lo


def _softmax_step_t(s, pv, m_ref, acc_ref, idx):
    m_prev = m_ref[idx]
    m_new = jnp.maximum(m_prev, jnp.max(s, axis=0, keepdims=True))
    p = jnp.exp2(s - m_new).astype(BF16)
    acc_ref[idx] = jnp.exp2(m_prev - m_new) * acc_ref[idx] + pv(p)
    m_ref[idx] = m_new


def _bias_selectors():
    sq = np.zeros((LANES, FOX_HEADS * LANES), np.float32)
    sk = np.zeros((LANES, FOX_HEADS * LANES), np.float32)
    for piece in range(3):
        for h in range(FOX_HEADS):
            sq[piece * FOX_HEADS + h, h * LANES + BIAS_LANE0 + piece] = 1.0
            sk[piece * FOX_HEADS + h, h * LANES + BIAS_LANE0 + 3 + piece] = -1.0
    return jnp.asarray(sq, BF16), jnp.asarray(sk, BF16)


def _fox_kernel(q_ref, k_ref, v_ref, ccol_ref, selq_ref, selk_ref, o_ref, kx_s, vt_s, qx_s, m_s, acc_s,
                *, tq, tk, seq):
    i = pl.program_id(1)
    lane = lax.broadcasted_iota(jnp.int32, (tq, LANES), 1)

    def bias_lanes(c, sel_ref):
        his, mids, los = (jnp.where(lane < FOX_HEADS, t, 0.0) for t in _split3(c * LOG2E))
        packed = his + pltpu.roll(mids, FOX_HEADS, axis=1) + pltpu.roll(los, 2 * FOX_HEADS, axis=1)
        return jnp.dot(packed.astype(BF16), sel_ref[...], preferred_element_type=F32)

    @pl.when(i == 0)
    def _build_keys():
        def chunk(r, carry):
            rows = pl.ds(pl.multiple_of(r * tq, tq), tq)
            e = bias_lanes(ccol_ref[0, rows, :], selk_ref)
            for h in range(FOX_HEADS):
                cols = slice(h * LANES, (h + 1) * LANES)
                kx_s[h, rows, :] = (k_ref[0, rows, cols].astype(F32) + e[:, cols]).astype(BF16)
            return carry

        lax.fori_loop(0, seq // tq, chunk, 0)

        def vchunk(c, carry):
            rows = pl.ds(pl.multiple_of(c * tk, tk), tk)
            for h in range(FOX_HEADS):
                cols = slice(h * LANES, (h + 1) * LANES)
                vt_s[c, h] = v_ref[0, rows, cols].astype(F32).T.astype(BF16)
            return carry

        lax.fori_loop(0, seq // tk, vchunk, 0)

    qrows = pl.ds(pl.multiple_of(i * tq, tq), tq)
    n_full = (i * tq) // tk
    row = lax.broadcasted_iota(jnp.int32, (LANES, tq), 0)
    n_pairs = FOX_HEADS // 2
    e = bias_lanes(ccol_ref[0, qrows, :], selq_ref)
    for h in range(FOX_HEADS):
        cols = slice(h * LANES, (h + 1) * LANES)
        qx_s[h] = (q_ref[0, :, cols].astype(F32) + e[:, cols]).astype(BF16)
    m_s[...] = jnp.full(m_s.shape, NEG_INF, F32)
    acc_s[...] = jnp.zeros(acc_s.shape, F32)

    def scores(j, p, nk=tk):
        rows = pl.ds(pl.multiple_of(j * tk, tk), nk)
        return tuple(_nt_dot(kx_s[2 * p + slot, rows, :], qx_s[2 * p + slot]) for slot in range(2))

    def causal(j, ss, nk):
        visible = (j * tk + lax.broadcasted_iota(jnp.int32, (nk, tq), 0)
                   <= i * tq + lax.broadcasted_iota(jnp.int32, (nk, tq), 1))
        return tuple(jnp.where(visible, s, NEG_INF) for s in ss)

    def consume(j, p, ss, nk=tk):
        for slot in range(2):
            h = 2 * p + slot
            _softmax_step_t(ss[slot], lambda pr, h=h: jnp.dot(vt_s[j, h, :, 0:nk], pr, preferred_element_type=F32),
                            m_s, acc_s, h)

    def key_step(j, ss, masked, last, nk=tk):
        for p in range(n_pairs):
            if p + 1 < n_pairs:
                nxt = scores(j, p + 1, nk)
                if masked:
                    nxt = causal(j, nxt, nk)
            else:
                nxt = None if last else scores(j + 1, 0)
            consume(j, p, ss, nk)
            ss = nxt
        return ss

    ss = lax.fori_loop(0, n_full, lambda j, ss: key_step(j, ss, False, False), scores(0, 0))

    def tail_short(ss):
        key_step(n_full, causal(n_full, scores(n_full, 0, tq), tq), True, True, tq)
        return 0

    def tail_full(ss):
        key_step(n_full, causal(n_full, ss, tk), True, True)
        return 0

    lax.cond((i * tq) % tk == 0, tail_short, tail_full, ss)

    for hp in range(n_pairs):
        a0, a1 = acc_s[2 * hp], acc_s[2 * hp + 1]
        ot = jnp.where(row < HEAD_DIM, a0 / a0[HEAD_DIM:HEAD_DIM + 1, :], a1 / a1[0:1, :])
        o_ref[0, :, hp * LANES:(hp + 1) * LANES] = ot.T.astype(BF16)


def _fox(fq, fk, fv, ccol):
    b, seq, w = fq.shape
    tq, tk = FOX_TQ, KV_STEP
    assert seq % tq == 0 and seq % tk == 0 and tk % tq == 0 and w == FOX_HEADS * LANES
    assert 3 * FOX_HEADS <= LANES and tk == 2 * tq
    sel_q, sel_k = _bias_selectors()
    whole = pl.BlockSpec((1, seq, w), lambda bi, i: (bi, 0, 0))
    return pl.pallas_call(
        functools.partial(_fox_kernel, tq=tq, tk=tk, seq=seq),
        grid=(b, seq // tq),
        in_specs=[pl.BlockSpec((1, tq, w), lambda bi, i: (bi, i, 0)), whole, whole,
                  pl.BlockSpec((1, seq, LANES), lambda bi, i: (bi, 0, 0)),
                  _resident(sel_q.shape), _resident(sel_k.shape)],
        out_specs=pl.BlockSpec((1, tq, FOX_WIDTH), lambda bi, i: (bi, i, 0)),
        out_shape=jax.ShapeDtypeStruct((b, seq, FOX_WIDTH), BF16),
        scratch_shapes=[pltpu.VMEM((FOX_HEADS, seq, LANES), BF16),
                        pltpu.VMEM((seq // tk, FOX_HEADS, LANES, tk), BF16),
                        pltpu.VMEM((FOX_HEADS, tq, LANES), BF16),
                        pltpu.VMEM((FOX_HEADS, 1, tq), F32), pltpu.VMEM((FOX_HEADS, LANES, tq), F32)],
        compiler_params=pltpu.CompilerParams(dimension_semantics=("arbitrary", "arbitrary"),
                                             vmem_limit_bytes=VMEM_LIMIT_BYTES),
        name="fox",
    )(fq, fk, fv, ccol, sel_q, sel_k)


def _compress_kernel(yk_ref, yv_ref, posk_ref, w1k_ref, w2k_ref, posv_ref, w1v_ref, w2v_ref,
                     kc_ref, vc_ref, *, nb):
    half = CMP_STRIDE * HEAD_DIM

    lane = lax.broadcasted_iota(jnp.int32, (nb, LANES), 1)
    left = lane < HEAD_DIM

    def regroup(x_ref):
        y = [[], []]
        for j in range(CMP_STRIDE // 2):
            xe = x_ref[0, pl.ds(2 * j, nb, stride=CMP_STRIDE), :]
            xo = x_ref[0, pl.ds(2 * j + 1, nb, stride=CMP_STRIDE), :]
            y[0].append(jnp.where(left, xe, pltpu.roll(xo, HEAD_DIM, axis=1)))
            y[1].append(jnp.where(left, pltpu.roll(xe, HEAD_DIM, axis=1), xo))
        return [jnp.concatenate(t, axis=1) for t in y]

    def comp(y, pos_ref, w1_ref, w2_ref):
        pos = pos_ref[...]
        def dot3(x, r0):
            xh = x.astype(BF16)
            xl = (x - xh.astype(F32)).astype(BF16)
            wh, wl = w1_ref[0, r0:r0 + half, :], w1_ref[1, r0:r0 + half, :]
            return (jnp.dot(xh, wh, preferred_element_type=F32) + jnp.dot(xh, wl, preferred_element_type=F32)
                    + jnp.dot(xl, wh, preferred_element_type=F32))

        a = dot3(y + pos[:, :half], 0)
        bm = dot3(y + pos[:, half:], half)
        pre = a + pltpu.roll(bm, nb - 1, axis=0)
        return jnp.dot(pre * _sigmoid(pre), w2_ref[...], precision=HIGHEST, preferred_element_type=F32)

    for g, (yk, yv) in enumerate(zip(regroup(yk_ref), regroup(yv_ref))):
        kc_ref[0, g] = comp(yk, posk_ref, w1k_ref, w2k_ref).astype(BF16)
        vc_ref[0, g] = comp(yv, posv_ref, w1v_ref, w2v_ref).astype(BF16)


def _compress(kc, vc, b, seq, pos_k, w1_k, w2_k, pos_v, w1_v, w2_v):
    g = NSA_KV_GROUPS
    nb = seq // CMP_STRIDE
    half = CMP_STRIDE * HEAD_DIM
    assert CMP_BLOCK == 2 * CMP_STRIDE and seq % CMP_STRIDE == 0 and nb % 8 == 0
    assert g == 2 and kc.shape[-1] == LANES

    zeros1 = jnp.zeros((CMP_BLOCK * HEAD_DIM, HEAD_DIM), F32)
    zeros2 = jnp.zeros((HEAD_DIM, HEAD_DIM), F32)
    def hi_lo(w):
        w = jnp.concatenate([w, zeros1], axis=-1)
        hi = w.astype(BF16)
        return jnp.stack([hi, (w - hi.astype(F32)).astype(BF16)])

    w1k, w1v = hi_lo(w1_k), hi_lo(w1_v)
    w2k = jnp.concatenate([jnp.concatenate([w2_k, zeros2], axis=-1), jnp.zeros((HEAD_DIM, LANES), F32)], axis=0)
    w2v = jnp.concatenate([jnp.concatenate([w2_v, w2_v], axis=-1), jnp.zeros((HEAD_DIM, LANES), F32)], axis=0)
    yspec = pl.BlockSpec((1, seq, LANES), lambda bi: (bi, 0, 0))
    ospec = pl.BlockSpec((1, g, nb, LANES), lambda bi: (bi, 0, 0, 0))
    return pl.pallas_call(
        functools.partial(_compress_kernel, nb=nb),
        grid=(b,),
        in_specs=[yspec, yspec,
                  _resident((1, 2 * half)), _resident(w1k.shape), _resident(w2k.shape),
                  _resident((1, 2 * half)), _resident(w1v.shape), _resident(w2v.shape)],
        out_specs=[ospec, ospec],
        out_shape=[jax.ShapeDtypeStruct((b, g, nb, LANES), BF16)] * 2,
        compiler_params=pltpu.CompilerParams(dimension_semantics=("arbitrary",),
                                             vmem_limit_bytes=VMEM_LIMIT_BYTES),
        name="compress",
    )(kc.reshape(b, seq, LANES), vc.reshape(b, seq, LANES), pos_k.reshape(1, 2 * half), w1k, w2k,
      pos_v.reshape(1, 2 * half), w1v, w2v)


def _overlap_matrix(nb, n_cmp, n_sel):
    ci = np.arange(nb)[:, None] * CMP_STRIDE
    sj = (np.arange(LANES)[None, :] - SEL_LANE0) * SEL_BLOCK
    jj = np.arange(LANES)[None, :] - SEL_LANE0
    ov = (ci < sj + SEL_BLOCK) & (ci + CMP_BLOCK > sj) & (jj >= 0) & (jj < n_sel) & (np.arange(nb)[:, None] < n_cmp)
    return jnp.asarray(ov.astype(np.float32))


def _cmpsel_kernel(q_ref, kc_ref, vc_ref, ovt_ref, qs_ref, ocmp_ref, vct_s, *, tq, n_cmp, nb):
    i = pl.program_id(1)
    n_g, hg = NSA_KV_GROUPS, NSA_HEADS_PER_GROUP

    @pl.when(i == 0)
    def _transpose_values():
        for g in range(n_g):
            vct_s[g] = vc_ref[0, g].astype(F32).T.astype(BF16)

    n_idx = lax.broadcasted_iota(jnp.int32, (nb, tq), 0)
    t_pos = i * tq + lax.broadcasted_iota(jnp.int32, (nb, tq), 1)
    cmask = (n_idx * CMP_STRIDE + (CMP_BLOCK - 1) <= t_pos) & (n_idx < n_cmp)
    top = lax.broadcasted_iota(jnp.int32, (LANES, tq), 0) < HEAD_DIM
    psums = []
    for g in range(n_g):
        kc = kc_ref[0, g]
        vct = vct_s[g]
        psum = jnp.zeros((nb, tq), F32)
        outs = []
        for h in range(hg):
            cols = slice((g * hg + h) * LANES, (g * hg + h + 1) * LANES)
            s = jnp.where(cmask, _nt_dot(kc, q_ref[0, :, cols]), NEG_INF)
            m = jnp.max(s, axis=0, keepdims=True)
            p = jnp.where(cmask, jnp.exp2(s - m), 0.0)
            p = p / jnp.maximum(jnp.sum(p, axis=0, keepdims=True), 1e-30)
            psum = psum + p
            outs.append(jnp.dot(vct, p.astype(BF16), preferred_element_type=F32))
        psums.append(psum)
        for hp in range(hg // 2):
            ocols = slice((g * (hg // 2) + hp) * LANES, (g * (hg // 2) + hp + 1) * LANES)
            ocmp_ref[0, :, ocols] = jnp.where(top, outs[2 * hp], outs[2 * hp + 1]).T

    wide = n_g * tq
    ovt = ovt_ref[...]
    p_slc = sum(jnp.dot(ovt, piece.astype(BF16), preferred_element_type=F32)
                for piece in _split3(jnp.concatenate(psums, axis=1)))
    sc = p_slc[SEL_LANE0:SEL_LANE0 + MAX_SEL_BLOCKS, :]
    j = lax.broadcasted_iota(jnp.int32, (MAX_SEL_BLOCKS, wide), 0)
    cur = (i * tq + (lax.broadcasted_iota(jnp.int32, (MAX_SEL_BLOCKS, wide), 1) & (tq - 1))) // SEL_BLOCK
    forced = (j == 0) | (j == cur) | (j == cur - 1)
    score = jnp.where(j <= cur, sc + FORCED_BONUS * forced.astype(F32), NEG_INF)
    j_f = j.astype(F32)
    sel = jnp.zeros((MAX_SEL_BLOCKS, wide), jnp.bool_)
    for _ in range(SEL_TOPN):
        mx = jnp.max(score, axis=0, keepdims=True)
        first = jnp.min(jnp.where(score == mx, j_f, float(MAX_SEL_BLOCKS)), axis=0, keepdims=True)
        chosen = j_f == first
        sel = sel | chosen
        score = jnp.where(chosen, -jnp.inf, score)
    bias_t = jnp.concatenate([jnp.zeros((SEL_LANE0, wide), F32), jnp.where(sel, 0.0, NEG_INF),
                              jnp.zeros((LANES - SEL_LANE0 - MAX_SEL_BLOCKS, wide), F32)], axis=0)
    for g in range(n_g):
        bias = bias_t[:, g * tq:(g + 1) * tq].T
        for h in range(hg):
            cols = slice((g * hg + h) * LANES, (g * hg + h + 1) * LANES)
            qs_ref[0, :, cols] = (q_ref[0, :, cols].astype(F32) + bias).astype(BF16)


def _cmpsel(nq3, kcmp, vcmp):
    b, seq, _ = nq3.shape
    g, hg = NSA_KV_GROUPS, NSA_HEADS_PER_GROUP
    tq = CMP_TQ
    nb = kcmp.shape[2]
    n_cmp = (seq - CMP_BLOCK) // CMP_STRIDE + 1
    n_sel = seq // SEL_BLOCK
    assert seq % tq == 0 and n_sel <= MAX_SEL_BLOCKS and n_sel >= SEL_TOPN
    ov = _overlap_matrix(nb, n_cmp, n_sel).T.astype(BF16)
    assert tq & (tq - 1) == 0
    qspec = pl.BlockSpec((1, tq, g * hg * LANES), lambda bi, i: (bi, i, 0))
    cspec = pl.BlockSpec((1, g, nb, LANES), lambda bi, i: (bi, 0, 0, 0))
    return pl.pallas_call(
        functools.partial(_cmpsel_kernel, tq=tq, n_cmp=n_cmp, nb=nb),
        grid=(b, seq // tq),
        in_specs=[qspec, cspec, cspec, _resident(ov.shape)],
        out_specs=[qspec, pl.BlockSpec((1, tq, NSA_WIDTH), lambda bi, i: (bi, i, 0))],
        out_shape=[jax.ShapeDtypeStruct(nq3.shape, BF16),
                   jax.ShapeDtypeStruct((b, seq, NSA_WIDTH), F32)],
        scratch_shapes=[pltpu.VMEM((g, LANES, nb), BF16)],
        compiler_params=pltpu.CompilerParams(dimension_semantics=("arbitrary",) * 2,
                                             vmem_limit_bytes=VMEM_LIMIT_BYTES),
        name="cmpsel",
    )(nq3, kcmp, vcmp, ov)


def _nsa_kernel(qs_ref, ks_ref, vs_ref, kw_ref, vw_ref, ocmp_ref, gates_ref, tab_ref, o_ref,
                vse_s, vso_s, vwe_s, vwo_s, m_s, acc_s, *, tq, tk, seq):
    i = pl.program_id(1)
    n_g, hg = NSA_KV_GROUPS, NSA_HEADS_PER_GROUP
    rows_all = hg * tq
    half = rows_all // 2
    wt = WINDOW // tq + 1

    @pl.when(i == 0)
    def _build_values():
        lane = lax.broadcasted_iota(jnp.int32, (tk, LANES), 1)
        one_e = (lane == HEAD_DIM).astype(F32)
        one_o = (lane == 0).astype(F32)

        def chunk(c, carry):
            rows = pl.ds(pl.multiple_of(c * tk, tk), tk)
            for g in range(n_g):
                cols = slice(g * LANES, (g + 1) * LANES)
                v = vs_ref[0, rows, cols].astype(F32)
                vse_s[g, c] = jnp.where(lane < HEAD_DIM, v, one_e).T.astype(BF16)
                vso_s[g, c] = jnp.where(lane < HEAD_DIM, one_o, v).T.astype(BF16)
                w = vw_ref[0, rows, cols].astype(F32)
                we = jnp.where(lane < HEAD_DIM, w, one_e).T.astype(BF16)
                wo = jnp.where(lane < HEAD_DIM, one_o, w).T.astype(BF16)
                for t in range(tk // tq):
                    vwe_s[g, c * (tk // tq) + t] = we[:, t * tq:(t + 1) * tq]
                    vwo_s[g, c * (tk // tq) + t] = wo[:, t * tq:(t + 1) * tq]
            return carry

        lax.fori_loop(0, seq // tk, chunk, 0)

    order = tuple(range(0, hg, 2)) + tuple(range(1, hg, 2))
    q = [jnp.concatenate([qs_ref[0, :, (g * hg + h) * LANES:(g * hg + h + 1) * LANES] for h in order], axis=0)
         for g in range(n_g)]

    n_full = (i * tq) // tk
    m_s[...] = jnp.full(m_s.shape, NEG_INF, F32)
    acc_s[...] = jnp.zeros(acc_s.shape, F32)

    def scores(j, g):
        rows = pl.ds(pl.multiple_of(j * tk, tk), tk)
        return _nt_dot(ks_ref[0, rows, g * LANES:(g + 1) * LANES], q[g])

    def consume(j, g, s, masked):
        if masked:
            s = s + _mask_bias(tab_ref, j * (tk // tq), tk // tq, i, hg)
        _softmax_step_t(s, lambda p: jnp.concatenate(
            [jnp.dot(vse_s[g, j], p[:, :half], preferred_element_type=F32),
             jnp.dot(vso_s[g, j], p[:, half:], preferred_element_type=F32)], axis=1), m_s, acc_s, g)

    st = jnp.maximum(i - (wt - 1), 0)

    win_items = [(g, x) for g in range(n_g) for x in range(2)]

    def win_scores(item):
        g, x = item
        wrows = pl.ds(pl.multiple_of(st * tq, tq), wt * tq)
        return (_nt_dot(kw_ref[0, wrows, g * LANES:(g + 1) * LANES], q[g][x * half:(x + 1) * half])
                + _mask_bias(tab_ref, st, wt, i, hg // 2))

    def win_consume(item, s_win):
        g, x = item
        v_s = vwo_s if x else vwe_s
        p = jnp.exp2(s_win - jnp.max(s_win, axis=0, keepdims=True)).astype(BF16)
        a_win = jnp.zeros((LANES, half), F32)
        for t in range(wt):
            a_win = a_win + jnp.dot(v_s[g, st + t], p[t * tq:(t + 1) * tq], preferred_element_type=F32)
        return a_win

    def key_step(j, s):
        for g in range(n_g):
            nxt = scores(j, g + 1) if g + 1 < n_g else scores(j + 1, 0)
            consume(j, g, s, False)
            s = nxt
        return s

    s = lax.fori_loop(0, n_full, key_step, scores(0, 0))

    for g in range(n_g):
        if g + 1 < n_g:
            nxt = scores(n_full, g + 1) + _mask_bias(tab_ref, n_full * (tk // tq), tk // tq, i, hg)
        else:
            nxt = win_scores(win_items[0])
        consume(n_full, g, s, g == 0)
        s = nxt
    parts = []
    for n, item in enumerate(win_items):
        nxt = win_scores(win_items[n + 1]) if n + 1 < len(win_items) else None
        parts.append(win_consume(item, s))
        s = nxt
    a_win = [jnp.concatenate(parts[2 * g:2 * g + 2], axis=1) for g in range(n_g)]

    lane = lax.broadcasted_iota(jnp.int32, (tq, LANES), 1)
    left = lane < HEAD_DIM
    top = lax.broadcasted_iota(jnp.int32, (LANES, tq), 0) < HEAD_DIM
    for g in range(n_g):
        gates = gates_ref[0, :, g * LANES:(g + 1) * LANES]
        a_sel = acc_s[g]
        for hp in range(hg // 2):
            h0, h1 = 2 * hp, 2 * hp + 1
            cols = slice((g * (hg // 2) + hp) * LANES, (g * (hg // 2) + hp + 1) * LANES)

            def pair(a):
                ae = a[:, hp * tq:(hp + 1) * tq]
                ao = a[:, half + hp * tq:half + (hp + 1) * tq]
                return jnp.where(top, ae / jnp.maximum(ae[HEAD_DIM:HEAD_DIM + 1, :], 1e-30),
                                 ao / jnp.maximum(ao[0:1, :], 1e-30)).T

            def gate(branch):
                return jnp.where(left, gates[:, 3 * h0 + branch:3 * h0 + branch + 1],
                                 gates[:, 3 * h1 + branch:3 * h1 + branch + 1])

            o = gate(0) * ocmp_ref[0, :, cols] + gate(1) * pair(a_sel) + gate(2) * pair(a_win[g])
            o_ref[0, :, cols] = o.astype(BF16)


def _nsa(qs, ksl, vsl, kwn, vwn, ocmp, gates):
    b, seq, _ = qs.shape
    g, hg = NSA_KV_GROUPS, NSA_HEADS_PER_GROUP
    tq, tk = NSA_TQ, KV_STEP
    assert seq % tk == 0 and tk % tq == 0 and WINDOW % tq == 0 and seq >= WINDOW + tq and tq & (tq - 1) == 0
    assert hg % 2 == 0
    assert WINDOW == 2 * tq
    kv = pl.BlockSpec((1, seq, g * LANES), lambda bi, i: (bi, 0, 0))
    ospec = pl.BlockSpec((1, tq, NSA_WIDTH), lambda bi, i: (bi, i, 0))
    return pl.pallas_call(
        functools.partial(_nsa_kernel, tq=tq, tk=tk, seq=seq),
        grid=(b, seq // tq),
        in_specs=[pl.BlockSpec((1, tq, g * hg * LANES), lambda bi, i: (bi, i, 0)), kv, kv, kv, kv,
                  ospec, pl.BlockSpec((1, tq, g * LANES), lambda bi, i: (bi, i, 0)), _resident((4, tq, tq))],
        out_specs=ospec,
        out_shape=jax.ShapeDtypeStruct((b, seq, NSA_WIDTH), BF16),
        scratch_shapes=[pltpu.VMEM((g, seq // tk, LANES, tk), BF16)] * 2
        + [pltpu.VMEM((g, seq // tq, LANES, tq), BF16)] * 2
        + [pltpu.VMEM((g, 1, hg * tq), F32), pltpu.VMEM((g, LANES, hg * tq), F32)],
        compiler_params=pltpu.CompilerParams(dimension_semantics=("arbitrary",) * 2,
                                             vmem_limit_bytes=VMEM_LIMIT_BYTES),
        name="nsa",
    )(qs, ksl, vsl, kwn, vwn, ocmp, gates, _mask_table(tq))


def _rms(y, g):
    return y * lax.rsqrt(jnp.mean(y * y, axis=-1, keepdims=True) + RMS_EPS) * g


def _tail_kernel(x_ref, fox_ref, nsa_ref, ga_ref, gb_ref, wf_ref, wn_ref, wo_ref, wup_ref, wdn_ref,
                 gmix_ref, gpre_ref, gpost_ref, o_ref, *, d_ff):
    a = jnp.dot(fox_ref[...], wf_ref[...], preferred_element_type=F32)
    b = jnp.dot(nsa_ref[...], wn_ref[...], preferred_element_type=F32)
    mix = ga_ref[...].astype(F32) * a + gb_ref[...].astype(F32) * b
    y = jnp.dot(mix.astype(BF16), wo_ref[...], preferred_element_type=F32)
    x1 = x_ref[...] + _rms(y, gmix_ref[...])
    h = _rms(x1, gpre_ref[...]).astype(BF16)
    acc = jnp.zeros(x1.shape, F32)
    for c in range(d_ff // FF_CHUNK):
        cols = slice(c * FF_CHUNK, (c + 1) * FF_CHUNK)
        u = jnp.maximum(jnp.dot(h, wup_ref[:, cols], preferred_element_type=F32), 0.0)
        acc = acc + jnp.dot((u * u).astype(BF16), wdn_ref[cols, :], preferred_element_type=F32)
    o_ref[...] = x1 + _rms(acc, gpost_ref[...])


def _tail(x2, fox, nsa, sga, sgb, wf, wn, wo, wup, wdn, g_mix_post, g_mlp_pre, g_mlp_post):
    n_tok, d = x2.shape
    d_ff = wup.shape[1]
    tm = TAIL_TM
    assert n_tok % tm == 0 and d_ff % FF_CHUNK == 0

    def row(w):
        return pl.BlockSpec((tm, w), lambda i: (i, 0))

    return pl.pallas_call(
        functools.partial(_tail_kernel, d_ff=d_ff),
        grid=(n_tok // tm,),
        in_specs=[row(d), row(fox.shape[1]), row(nsa.shape[1]), row(d), row(d),
                  _resident(wf.shape), _resident(wn.shape), _resident(wo.shape),
                  _resident(wup.shape), _resident(wdn.shape),
                  _resident((1, d)), _resident((1, d)), _resident((1, d))],
        out_specs=row(d),
        out_shape=jax.ShapeDtypeStruct((n_tok, d), F32),
        compiler_params=pltpu.CompilerParams(dimension_semantics=("arbitrary",),
                                             vmem_limit_bytes=VMEM_LIMIT_BYTES),
        name="tail",
    )(x2, fox, nsa, sga, sgb, wf, wn, wo, wup, wdn, g_mix_post, g_mlp_pre, g_mlp_post)


def _layer(x, norm_mix_pre, norm_mix_post, norm_mlp_pre, norm_mlp_post, w_in, b_forget,
           cmp_pos_k, cmp_w1_k, cmp_w2_k, cmp_pos_v, cmp_w1_v, cmp_w2_v,
           w_fox_out, w_nsa_out, w_o, w_up, w_down):
    b, seq, d = x.shape
    n_tok = b * seq
    x2 = x.reshape(n_tok, d)
    w_small, w_ga, w_gb = _rearrange_w_in(w_in, d)
    bmisc = jnp.concatenate([b_forget.astype(F32), jnp.zeros((LANES - FOX_HEADS,), F32)]).reshape(1, LANES)
    (fq, fk, fv, nq, ksl, vsl, kwn, vwn, kc, vc, misc, gates, sga, sgb) = _proj(
        x2, norm_mix_pre.reshape(1, d), w_small, w_ga, w_gb, _rope_tables(seq), bmisc, seq)

    def seq3(t):
        return t.reshape(b, seq, t.shape[-1])

    fox = _fox(seq3(fq), seq3(fk), seq3(fv), _cumsum(seq3(misc)))

    kcmp, vcmp = _compress(kc, vc, b, seq, cmp_pos_k, cmp_w1_k, cmp_w2_k, cmp_pos_v, cmp_w1_v, cmp_w2_v)
    qs, ocmp = _cmpsel(seq3(nq), kcmp, vcmp)
    nsa = _nsa(qs, seq3(ksl), seq3(vsl), seq3(kwn), seq3(vwn), ocmp, seq3(gates))

    out = _tail(x2, fox.reshape(n_tok, FOX_WIDTH), nsa.reshape(n_tok, NSA_WIDTH), sga, sgb,
                w_fox_out.astype(BF16), w_nsa_out.astype(BF16), w_o.astype(BF16),
                w_up.astype(BF16), w_down.astype(BF16),
                norm_mix_post.reshape(1, d), norm_mlp_pre.reshape(1, d), norm_mlp_post.reshape(1, d))
    return out.reshape(b, seq, d)


def kernel(x, norm_mix_pre, norm_mix_post, norm_mlp_pre, norm_mlp_post, w_in, b_forget, cmp_pos_k, cmp_w1_k, cmp_w2_k, cmp_pos_v, cmp_w1_v, cmp_w2_v, w_fox_out, w_nsa_out, w_o, w_up, w_down):
    params = (norm_mix_pre, norm_mix_post, norm_mlp_pre, norm_mlp_post, w_in, b_forget,
              cmp_pos_k, cmp_w1_k, cmp_w2_k, cmp_pos_v, cmp_w1_v, cmp_w2_v,
              w_fox_out, w_nsa_out, w_o, w_up, w_down)
    for layer in range(w_in.shape[0]):
        x = _layer(x, *[p[layer] for p in params])
    return x
```
